```python
import math
import jax, jax.numpy as jnp
from jax import lax
import numpy as np

D_MODEL = 2048
BATCH = 4
SEQ = 2048
DEPTH = 4
DEC_BATCH = 32
DEC_SEQ = 4
PAST_LEN = 16384
PAGE_SIZE = 128

N_META = 16
GROUP_W = D_MODEL // 4
MIX_W = 4 * GROUP_W
CHUNK = 128
CONV_K = 4
NORM_EPS = 1e-6

SSD_HEAD_DIM = 64
SSD_HEADS = GROUP_W // SSD_HEAD_DIM
SSD_GROUPS = 2
SSD_STATE = 128
SSD_CONV_DIM = GROUP_W + 2 * SSD_GROUPS * SSD_STATE

WINDOW = 128
SWA_HEAD_DIM = 64
SWA_HEADS = GROUP_W // SWA_HEAD_DIM
SWA_KV_HEADS = 2

RET_HEADS = 4
RET_DV = GROUP_W // RET_HEADS
RET_DK = RET_DV // 2

GDN_HEADS = 4
GDN_DK = GROUP_W // GDN_HEADS
GDN_DV = GROUP_W // GDN_HEADS
GDN_CONV_DIM = 2 * GDN_HEADS * GDN_DK + GDN_HEADS * GDN_DV

SPLIT_SIZES = (
    GROUP_W, SSD_CONV_DIM, SSD_HEADS,
    SWA_HEADS * SWA_HEAD_DIM, SWA_KV_HEADS * SWA_HEAD_DIM,
    SWA_KV_HEADS * SWA_HEAD_DIM, GROUP_W,
    RET_HEADS * RET_DK, RET_HEADS * RET_DK, RET_HEADS * RET_DV, GROUP_W,
    GDN_CONV_DIM, GROUP_W, GDN_HEADS, GDN_HEADS,
)
IN_W = sum(SPLIT_SIZES)
SPLIT_POINTS = tuple(int(s) for s in np.cumsum(SPLIT_SIZES)[:-1])

kernel_name = 'hybrid_ssd_swa_ret_gdn_step'


def rmsnorm(x, g):
    xf = x.astype(jnp.float32)
    y = xf * lax.rsqrt(jnp.mean(xf * xf, axis=-1, keepdims=True) + NORM_EPS)
    return (y * g.astype(jnp.float32)).astype(x.dtype)


def group_rmsnorm(x, g, n_groups):
    shp = x.shape
    xg = x.reshape(shp[:-1] + (n_groups, shp[-1] // n_groups))
    return rmsnorm(xg, g.reshape(n_groups, -1)).reshape(shp)


def l2norm(x):
    return x * lax.rsqrt(jnp.sum(x * x, axis=-1, keepdims=True) + 1e-6)


def causal_conv(u, prev, w, b=None):
    ext = jnp.concatenate([prev.astype(u.dtype), u], axis=1)
    out = lax.conv_general_dilated(ext, w[:, None, :].astype(u.dtype), window_strides=(1,), padding='VALID',
                                   dimension_numbers=('NWC', 'WIO', 'NWC'), feature_group_count=u.shape[-1])
    if b is not None:
        out = out + b.astype(u.dtype)
    return out, ext[:, -(CONV_K - 1):]


def _chunks(a, c):
    b, l = a.shape[:2]
    return jnp.moveaxis(a.reshape((b, l // c, c) + a.shape[2:]), 1, 0)


def _unchunk(a):
    nc, b, c = a.shape[:3]
    return jnp.moveaxis(a, 0, 1).reshape((b, nc * c) + a.shape[3:])


def decay_linear_scan(q, k, v, log_a, s0, chunk):
    tri = jnp.tril(jnp.ones((chunk, chunk), bool))

    def step(s, inp):
        qc, kc, vc, la = inp
        cum = jnp.cumsum(la, axis=1).transpose(0, 2, 1)
        seg = cum[..., :, None] - cum[..., None, :]
        decay = jnp.where(tri, jnp.exp(jnp.where(tri, seg, 0.0)), 0.0)
        scores = jnp.einsum('bihd,bjhd->bhij', qc, kc) * decay
        o = (jnp.einsum('bhij,bjhe->bihe', scores, vc)
             + jnp.einsum('bihd,bhi,bhde->bihe', qc, jnp.exp(cum), s))
        s_new = (s * jnp.exp(cum[..., -1])[..., None, None]
                 + jnp.einsum('bjhd,bhj,bjhe->bhde', kc, jnp.exp(cum[..., -1:] - cum), vc))
        return s_new, o

    s, o = lax.scan(step, s0, (_chunks(q, chunk), _chunks(k, chunk), _chunks(v, chunk), _chunks(log_a, chunk)))
    return _unchunk(o), s


def gated_delta_scan(q, k, v, beta, log_a, s0, chunk):
    tri = jnp.tril(jnp.ones((chunk, chunk), bool))
    eye = jnp.eye(chunk, dtype=jnp.float32)

    def step(s, inp):
        qc, kc, vc, bc, la = inp
        cum = jnp.cumsum(la, axis=1).transpose(0, 2, 1)
        seg = cum[..., :, None] - cum[..., None, :]
        decay = jnp.where(tri, jnp.exp(jnp.where(tri, seg, 0.0)), 0.0)
        kb = kc * bc[..., None]
        a = jnp.einsum('bihd,bjhd->bhij', kb, kc) * decay * (1.0 - eye)
        t = lax.linalg.triangular_solve(eye + a, jnp.broadcast_to(eye, a.shape),
                                        left_side=True, lower=True, unit_diagonal=True)
        u = jnp.einsum('bhij,bjhe->bhie', t, vc * bc[..., None])
        w = jnp.einsum('bhij,bjhd->bhid', t, kb * jnp.exp(cum).transpose(0, 2, 1)[..., None])
        v_new = u - jnp.einsum('bhid,bhde->bhie', w, s)
        qk = jnp.einsum('bihd,bjhd->bhij', qc, kc) * decay
        o = (jnp.einsum('bihd,bhi,bhde->bihe', qc, jnp.exp(cum), s)
             + jnp.einsum('bhij,bhje->bihe', qk, v_new))
        s_new = (s * jnp.exp(cum[..., -1])[..., None, None]
                 + jnp.einsum('bjhd,bhj,bhje->bhde', kc, jnp.exp(cum[..., -1:] - cum), v_new))
        return s_new, o

    s, o = lax.scan(step, s0, (_chunks(q, chunk), _chunks(k, chunk), _chunks(v, chunk),
                               _chunks(beta, chunk), _chunks(log_a, chunk)))
    return _unchunk(o), s


def scan_with_meta(fn, seqs, s0, prompt):
    l = seqs[0].shape[1]
    if prompt:
        o_meta, s = fn(*[a[:, :N_META] for a in seqs], s0, N_META)
        o_tok, s = fn(*[a[:, N_META:] for a in seqs], s, math.gcd(l - N_META, CHUNK))
        return jnp.concatenate([o_meta, o_tok], axis=1), s
    return fn(*seqs, s0, math.gcd(l, CHUNK))


def sink_attention(q, k, v, q_pos, k_pos, sinks):
    g = q.shape[4]
    f32 = jnp.float32
    slopes = jnp.exp2(-8.0 * jnp.arange(1, SWA_HEADS + 1, dtype=f32) / SWA_HEADS).reshape(SWA_KV_HEADS, g)
    dist = q_pos[:, :, None] - k_pos[:, None, :]
    visible = (dist >= 0) & (dist <= WINDOW) & (k_pos[:, None, :] >= 0)
    s = jnp.einsum('bnqkgd,bnskd->bnkgqs', q, k).astype(f32) * (SWA_HEAD_DIM ** -0.5)
    s = s - slopes[None, None, :, :, None, None] * dist.astype(f32)[None, :, None, None]
    s = jnp.where(visible[None, :, None, None], s, jnp.finfo(f32).min)
    sink = jnp.broadcast_to(sinks.astype(f32).reshape(SWA_KV_HEADS, g)[None, None, :, :, None, None],
                            s.shape[:-1] + (1,))
    p = jax.nn.softmax(jnp.concatenate([s, sink], axis=-1), axis=-1)[..., :-1]
    return jnp.einsum('bnkgqs,bnskd->bnqkgd', p.astype(v.dtype), v)


def swa_prompt(q, k, v, sinks):
    b, l = q.shape[:2]
    nb = -(-l // WINDOW)
    pad = nb * WINDOW - l
    qb = jnp.pad(q, ((0, 0), (0, pad), (0, 0), (0, 0), (0, 0))).reshape((b, nb, WINDOW) + q.shape[2:])

    def key_blocks(a):
        ap = jnp.pad(a, ((0, 0), (WINDOW, pad), (0, 0), (0, 0)))
        prev = ap[:, :nb * WINDOW].reshape((b, nb, WINDOW) + a.shape[2:])
        cur = ap[:, WINDOW:].reshape((b, nb, WINDOW) + a.shape[2:])
        return jnp.concatenate([prev, cur], axis=2)

    start = jnp.arange(nb)[:, None] * WINDOW
    q_pos = start + jnp.arange(WINDOW)[None]
    k_pos = start - WINDOW + jnp.arange(2 * WINDOW)[None]
    o = sink_attention(qb, key_blocks(k), key_blocks(v), q_pos, k_pos, sinks)
    return o.reshape((b, nb * WINDOW) + q.shape[2:])[:, :l]


def hybrid_mixer(h, w_in, w_out, ssd_conv_w, ssd_conv_b, ssd_dt_bias, ssd_a_log, ssd_d, ssd_norm,
                 swa_sinks, ret_norm, gdn_conv_w, gdn_dt_bias, gdn_a_log, gdn_norm, state, prompt):
    b, l, _ = h.shape
    f32 = jnp.float32
    if prompt:
        state = (jnp.zeros((b, SSD_HEADS, SSD_STATE, SSD_HEAD_DIM), f32),
                 jnp.zeros((b, CONV_K - 1, SSD_CONV_DIM), h.dtype), None, None,
                 jnp.zeros((b, RET_HEADS, RET_DK, RET_DV), f32),
                 jnp.zeros((b, GDN_HEADS, GDN_DK, GDN_DV), f32),
                 jnp.zeros((b, CONV_K - 1, GDN_CONV_DIM), h.dtype))
    ssd_s0, ssd_c0, k_buf, v_buf, ret_s0, gdn_s0, gdn_c0 = state

    (z, xbc, dt_raw, qa, ka, va, ga, qr, kr, vr, gr, qkv_d, gd, bd, ad) = jnp.split(h @ w_in, SPLIT_POINTS, axis=-1)

    xbc, ssd_c1 = causal_conv(xbc, ssd_c0, ssd_conv_w, ssd_conv_b)
    xbc = jax.nn.silu(xbc)
    xs = xbc[..., :GROUP_W].reshape(b, l, SSD_HEADS, SSD_HEAD_DIM).astype(f32)
    bcm = xbc[..., GROUP_W:].reshape(b, l, 2, SSD_GROUPS, SSD_STATE).astype(f32)
    rep = SSD_HEADS // SSD_GROUPS
    b_in = jnp.repeat(bcm[:, :, 0], rep, axis=2)
    c_out = jnp.repeat(bcm[:, :, 1], rep, axis=2)
    dt = jax.nn.softplus(dt_raw.astype(f32) + ssd_dt_bias.astype(f32))
    log_a = -jnp.exp(ssd_a_log.astype(f32)) * dt
    o, ssd_s1 = scan_with_meta(decay_linear_scan, (c_out, b_in, xs * dt[..., None], log_a),
                               ssd_s0.astype(f32), prompt)
    y = (o + ssd_d.astype(f32)[:, None] * xs).reshape(b, l, GROUP_W).astype(h.dtype)
    y_ssd = group_rmsnorm(y * jax.nn.silu(z), ssd_norm, SSD_GROUPS)

    grp = SWA_HEADS // SWA_KV_HEADS
    q = qa.reshape(b, l, SWA_KV_HEADS, grp, SWA_HEAD_DIM)
    k = ka.reshape(b, l, SWA_KV_HEADS, SWA_HEAD_DIM)
    v = va.reshape(b, l, SWA_KV_HEADS, SWA_HEAD_DIM)
    if prompt:
        o = swa_prompt(q, k, v, swa_sinks)
        k_win, v_win = k[:, -WINDOW:], v[:, -WINDOW:]
    else:
        k_all = jnp.concatenate([k_buf.astype(k.dtype), k], axis=1)
        v_all = jnp.concatenate([v_buf.astype(v.dtype), v], axis=1)
        q_pos = PAST_LEN + jnp.arange(l)
        k_pos = PAST_LEN - WINDOW + jnp.arange(WINDOW + l)
        o = sink_attention(q[:, None], k_all[:, None], v_all[:, None], q_pos[None], k_pos[None], swa_sinks)[:, 0]
        k_win, v_win = k_all[:, -WINDOW:], v_all[:, -WINDOW:]
    y_swa = o.reshape(b, l, GROUP_W) * jax.nn.silu(ga)

    qr = qr.reshape(b, l, RET_HEADS, RET_DK).astype(f32)
    kr = kr.reshape(b, l, RET_HEADS, RET_DK).astype(f32) * (RET_DK ** -0.5)
    vr = vr.reshape(b, l, RET_HEADS, RET_DV).astype(f32)
    log_gamma = jnp.log1p(-jnp.exp2(-5.0 - jnp.arange(RET_HEADS, dtype=f32)))
    o, ret_s1 = scan_with_meta(decay_linear_scan, (qr, kr, vr, jnp.broadcast_to(log_gamma, (b, l, RET_HEADS))),
                               ret_s0.astype(f32), prompt)
    y_ret = group_rmsnorm(o.reshape(b, l, GROUP_W), ret_norm, RET_HEADS).astype(h.dtype) * jax.nn.silu(gr)

    qkv, gdn_c1 = causal_conv(qkv_d, gdn_c0, gdn_conv_w)
    qkv = jax.nn.silu(qkv).astype(f32)
    qd, kd, vd = jnp.split(qkv, [GDN_HEADS * GDN_DK, 2 * GDN_HEADS * GDN_DK], axis=-1)
    qd = l2norm(qd.reshape(b, l, GDN_HEADS, GDN_DK)) * (GDN_DK ** -0.5)
    kd = l2norm(kd.reshape(b, l, GDN_HEADS, GDN_DK))
    vd = vd.reshape(b, l, GDN_HEADS, GDN_DV)
    beta = jax.nn.sigmoid(bd.astype(f32))
    log_alpha = -jnp.exp(gdn_a_log.astype(f32)) * jax.nn.softplus(ad.astype(f32) + gdn_dt_bias.astype(f32))
    o, gdn_s1 = scan_with_meta(gated_delta_scan, (qd, kd, vd, beta, log_alpha), gdn_s0.astype(f32), prompt)
    y_gdn = rmsnorm(o, gdn_norm).reshape(b, l, GROUP_W).astype(h.dtype) * jax.nn.silu(gd)

    out = jnp.concatenate([y_ssd, y_swa, y_ret, y_gdn], axis=-1) @ w_out
    return out, (ssd_s1, ssd_c1, k_win, v_win, ret_s1, gdn_s1, gdn_c1)


def trunk(x, weights, states, prompt):
    pre_norm, post_norm = weights[0], weights[1]
    new = []
    for layer in range(DEPTH):
        st = None if prompt else tuple(s[layer] for s in states)
        h = rmsnorm(x, pre_norm[layer])
        m, st_new = hybrid_mixer(h, *[w[layer] for w in weights[2:]], st, prompt)
        x = x + rmsnorm(m, post_norm[layer])
        new.append(st_new)
    stacked = [jnp.stack([n[i] for n in new]) for i in range(7)]
    return x, stacked


def setup_inputs(seed: int = 0) -> dict:
    key = jax.random.key(seed)
    ks = iter(jax.random.split(key, 40))
    f32 = jnp.float32

    def nrm(shape, scale):
        return scale * jax.random.normal(next(ks), shape, f32)

    def gain(shape):
        return 1.0 + nrm(shape, 0.02)

    def dt_bias(shape):
        dt = jnp.exp(jax.random.uniform(next(ks), shape, f32, math.log(1e-3), math.log(1e-1)))
        return dt + jnp.log(-jnp.expm1(-dt))

    def a_log(shape):
        return jnp.log(jax.random.uniform(next(ks), shape, f32, 1.0, 16.0))

    return {
        'x_prompt': nrm((BATCH, SEQ, D_MODEL), 1.0),
        'x_sample': nrm((DEC_BATCH, DEC_SEQ, D_MODEL), 1.0),
        'state_ssd': nrm((DEPTH, DEC_BATCH, SSD_HEADS, SSD_STATE, SSD_HEAD_DIM), 0.1),
        'state_ssd_conv': nrm((DEPTH, DEC_BATCH, CONV_K - 1, SSD_CONV_DIM), 1.0),
        'cache_swa_k': nrm((DEPTH, DEC_BATCH, WINDOW, SWA_KV_HEADS, SWA_HEAD_DIM), 1.0),
        'cache_swa_v': nrm((DEPTH, DEC_BATCH, WINDOW, SWA_KV_HEADS, SWA_HEAD_DIM), 1.0),
        'state_ret': nrm((DEPTH, DEC_BATCH, RET_HEADS, RET_DK, RET_DV), 0.1),
        'state_gdn': nrm((DEPTH, DEC_BATCH, GDN_HEADS, GDN_DK, GDN_DV), 0.1),
        'state_gdn_conv': nrm((DEPTH, DEC_BATCH, CONV_K - 1, GDN_CONV_DIM), 1.0),
        'meta_tokens': nrm((N_META, D_MODEL), 1.0),
        'pre_norm': gain((DEPTH, D_MODEL)),
        'post_norm': gain((DEPTH, D_MODEL)),
        'w_in': nrm((DEPTH, D_MODEL, IN_W), D_MODEL ** -0.5),
        'w_out': nrm((DEPTH, MIX_W, D_MODEL), MIX_W ** -0.5),
        'ssd_conv_w': nrm((DEPTH, CONV_K, SSD_CONV_DIM), CONV_K ** -0.5),
        'ssd_conv_b': nrm((DEPTH, SSD_CONV_DIM), 0.02),
        'ssd_dt_bias': dt_bias((DEPTH, SSD_HEADS)),
        'ssd_a_log': a_log((DEPTH, SSD_HEADS)),
        'ssd_d': gain((DEPTH, SSD_HEADS)),
        'ssd_norm': gain((DEPTH, GROUP_W)),
        'swa_sinks': nrm((DEPTH, SWA_HEADS), 0.5),
        'ret_norm': gain((DEPTH, GROUP_W)),
        'gdn_conv_w': nrm((DEPTH, CONV_K, GDN_CONV_DIM), CONV_K ** -0.5),
        'gdn_dt_bias': dt_bias((DEPTH, GDN_HEADS)),
        'gdn_a_log': a_log((DEPTH, GDN_HEADS)),
        'gdn_norm': gain((DEPTH, GDN_DV)),
    }


def reference(x_prompt, x_sample, state_ssd, state_ssd_conv, cache_swa_k, cache_swa_v, state_ret, state_gdn,
              state_gdn_conv, meta_tokens, pre_norm, post_norm, w_in, w_out, ssd_conv_w, ssd_conv_b, ssd_dt_bias,
              ssd_a_log, ssd_d, ssd_norm, swa_sinks, ret_norm, gdn_conv_w, gdn_dt_bias, gdn_a_log, gdn_norm):
    weights = (pre_norm, post_norm, w_in, w_out, ssd_conv_w, ssd_conv_b, ssd_dt_bias, ssd_a_log, ssd_d, ssd_norm,
               swa_sinks, ret_norm, gdn_conv_w, gdn_dt_bias, gdn_a_log, gdn_norm)
    b = x_prompt.shape[0]
    meta = jnp.broadcast_to(meta_tokens[None].astype(x_prompt.dtype), (b, N_META, D_MODEL))
    xp = jnp.concatenate([meta, x_prompt], axis=1)
    yp, (p_ssd, p_ssd_conv, p_swa_k, p_swa_v, p_ret, p_gdn, p_gdn_conv) = trunk(xp, weights, None, True)
    samp_states = (state_ssd, state_ssd_conv, cache_swa_k, cache_swa_v, state_ret, state_gdn, state_gdn_conv)
    ys, (s_ssd, s_ssd_conv, s_swa_k, s_swa_v, s_ret, s_gdn, s_gdn_conv) = trunk(x_sample, weights, samp_states, False)
    return (yp[:, N_META:], ys, p_ssd, p_ssd_conv, p_swa_k, p_swa_v, p_ret, p_gdn, p_gdn_conv,
            s_ssd, s_ssd_conv, s_swa_k, s_swa_v, s_ret, s_gdn, s_gdn_conv)
```

```python
import functools

import jax
import jax.numpy as jnp
from jax import lax
from jax.experimental import pallas as pl
from jax.experimental.pallas import tpu as pltpu

F32 = jnp.float32
BF16 = jnp.bfloat16
HIGHEST = lax.Precision.HIGHEST

D_MODEL = 2048
BATCH = 4
SEQ = 2048
DEPTH = 4
DEC_BATCH = 32
DEC_SEQ = 4
N_META = 16
GROUP_W = 512
CONV_K = 4
NORM_EPS = 1e-6
WINDOW = 128

SSD_HEADS, SSD_HEAD_DIM, SSD_GROUPS, SSD_STATE = 8, 64, 2, 128
SWA_HEADS, SWA_KV_HEADS, SWA_HEAD_DIM = 8, 2, 64
RET_HEADS, RET_DK, RET_DV = 4, 64, 128
GDN_HEADS, GDN_DK, GDN_DV = 4, 128, 128

CHUNK = 128
SUBLANES = 8
PROMPT_PAD = CHUNK - N_META
PROMPT_ROWS = PROMPT_PAD + N_META + SEQ
PROMPT_CHUNKS = PROMPT_ROWS // CHUNK
N_PROMPT = BATCH * PROMPT_ROWS
N_SAMPLE = DEC_BATCH * SUBLANES
N_ROWS = N_PROMPT + N_SAMPLE

IN_W_PAD = 6528
COL_Z, COL_X, COL_BC, COL_QA, COL_GA, COL_VR, COL_GR = 0, 512, 1024, 1536, 2048, 2560, 3072
COL_QD, COL_KD, COL_VD, COL_GD, COL_QKR, COL_KV, COL_SMALL = 3584, 4096, 4608, 5120, 5632, 6144, 6400
LANE_DT, LANE_BETA, LANE_DECAY = 0, 8, 12

INPROJ_TM, INPROJ_TN = 448, 2176
OUTPROJ_TM = 448
VMEM_LIMIT = 48 * 1024 * 1024


def _sigmoid(x):
    return 1.0 / (1.0 + jnp.exp(-x))


def _silu(x):
    return x * _sigmoid(x)


def _softplus(x):
    return jnp.maximum(x, 0.0) + jnp.log1p(jnp.exp(-jnp.abs(x)))


def _dot(a, b):
    return jnp.dot(a.astype(BF16), b.astype(BF16), preferred_element_type=F32)


def _dot_nt(a, b):
    return lax.dot_general(a.astype(BF16), b.astype(BF16), (((1,), (1,)), ((), ())),
                           preferred_element_type=F32)


def _dot_f32(a, b):
    return jnp.dot(a, b, precision=HIGHEST, preferred_element_type=F32)


def _pad_rows(a, rows):
    if a.shape[0] == rows:
        return a
    return jnp.concatenate([a, jnp.zeros((rows - a.shape[0], a.shape[1]), a.dtype)], axis=0)


def _col(a, lane, width):
    return jnp.broadcast_to(a[:, lane:lane + 1], (a.shape[0], width))


def _iotas():
    row = lax.broadcasted_iota(jnp.int32, (CHUNK, CHUNK), 0)
    col = lax.broadcasted_iota(jnp.int32, (CHUNK, CHUNK), 1)
    return row, col


def _valid_rows(chunk_idx, first_lo, hi):
    r = lax.broadcasted_iota(jnp.int32, (CHUNK, 1), 0)
    lo = jnp.where(chunk_idx == 0, first_lo, 0)
    return (r >= lo) & (r < hi)


def _decay_terms(la):
    row, col = _iotas()
    lower = (row >= col).astype(F32)
    cum = _dot_f32(lower, la)
    last = cum[CHUNK - 1:CHUNK, :]
    return cum, cum.T, jnp.exp(cum), jnp.exp(last - cum), jnp.exp(last)


def _decay_matrix(cum, cum_t, lane):
    row, col = _iotas()
    keep = row >= col
    seg = _col(cum, lane, CHUNK) - jnp.broadcast_to(cum_t[lane:lane + 1, :], (CHUNK, CHUNK))
    return jnp.where(keep, jnp.exp(jnp.where(keep, seg, 0.0)), 0.0)


def _causal_conv(ext_ref, cur, w_ref):
    rows = cur.shape[0]
    ext_ref[SUBLANES:SUBLANES + rows, :] = cur
    acc = cur * w_ref[CONV_K - 1:CONV_K, :]
    for j in range(CONV_K - 1):
        start = SUBLANES - (CONV_K - 1) + j
        acc = acc + ext_ref[start:start + rows, :] * w_ref[j:j + 1, :]
    ext_ref[0:SUBLANES, :] = ext_ref[rows:rows + SUBLANES, :]
    return acc


def _inproj_kernel(x_ref, g_ref, w_ref, o_ref):
    x = x_ref[...]
    ms = jnp.mean(x * x, axis=-1, keepdims=True)
    h = (x * lax.rsqrt(ms + NORM_EPS) * g_ref[...]).astype(BF16)
    o_ref[...] = jnp.dot(h, w_ref[...], preferred_element_type=F32)


def _inproj(x, g, w):
    return pl.pallas_call(
        _inproj_kernel,
        grid=(IN_W_PAD // INPROJ_TN, N_ROWS // INPROJ_TM),
        in_specs=[
            pl.BlockSpec((INPROJ_TM, D_MODEL), lambda j, i: (i, 0)),
            pl.BlockSpec((1, D_MODEL), lambda j, i: (0, 0)),
            pl.BlockSpec((D_MODEL, INPROJ_TN), lambda j, i: (0, j)),
        ],
        out_specs=pl.BlockSpec((INPROJ_TM, INPROJ_TN), lambda j, i: (i, j)),
        out_shape=jax.ShapeDtypeStruct((N_ROWS, IN_W_PAD), F32),
        compiler_params=pltpu.CompilerParams(
            dimension_semantics=("arbitrary", "arbitrary"), vmem_limit_bytes=VMEM_LIMIT),
        name="inproj",
    )(x, g, w)


def _outproj_kernel(ya_ref, yb_ref, yc_ref, yd_ref, w_ref, x_ref, g_ref, m_ref, o_ref):
    acc = jnp.dot(ya_ref[...], w_ref[0:GROUP_W, :], preferred_element_type=F32)
    acc = acc + jnp.dot(yb_ref[...], w_ref[GROUP_W:2 * GROUP_W, :], preferred_element_type=F32)
    acc = acc + jnp.dot(yc_ref[...], w_ref[2 * GROUP_W:3 * GROUP_W, :], preferred_element_type=F32)
    acc = acc + jnp.dot(yd_ref[...], w_ref[3 * GROUP_W:4 * GROUP_W, :], preferred_element_type=F32)
    ms = jnp.mean(acc * acc, axis=-1, keepdims=True)
    r = acc * lax.rsqrt(ms + NORM_EPS) * g_ref[...]
    o_ref[...] = jnp.where(m_ref[...] > 0.0, x_ref[...] + r, 0.0)


def _outproj(ys, w, x, g, rowmask):
    yspec = pl.BlockSpec((OUTPROJ_TM, GROUP_W), lambda i: (i, 0))
    return pl.pallas_call(
        _outproj_kernel,
        grid=(N_ROWS // OUTPROJ_TM,),
        in_specs=[yspec, yspec, yspec, yspec,
                  pl.BlockSpec((D_MODEL, D_MODEL), lambda i: (0, 0)),
                  pl.BlockSpec((OUTPROJ_TM, D_MODEL), lambda i: (i, 0)),
                  pl.BlockSpec((1, D_MODEL), lambda i: (0, 0)),
                  pl.BlockSpec((OUTPROJ_TM, 1), lambda i: (i, 0))],
        out_specs=pl.BlockSpec((OUTPROJ_TM, D_MODEL), lambda i: (i, 0)),
        out_shape=jax.ShapeDtypeStruct((N_ROWS, D_MODEL), F32),
        compiler_params=pltpu.CompilerParams(
            dimension_semantics=("arbitrary",), vmem_limit_bytes=VMEM_LIMIT),
        name="outproj",
    )(*ys, w, x, g, rowmask)


def _ssd_kernel(rows, has_init, nchunks, first_lo, hi, *refs):
    if has_init:
        (z_ref, x_ref, bc_ref, sm_ref, s0_ref, c0_ref, cw_ref, cb_ref, dtb_ref, alog_ref, nw_ref, d_ref,
         y_ref, so_ref, s_scr, ext) = refs
    else:
        (z_ref, x_ref, bc_ref, sm_ref, cw_ref, cb_ref, dtb_ref, alog_ref, nw_ref, d_ref,
         y_ref, so_ref, s_scr, ext) = refs
    c = pl.program_id(1)

    @pl.when(c == 0)
    def _():
        if has_init:
            s_scr[...] = s0_ref[0]
            ext[0:SUBLANES, :] = c0_ref[0]
        else:
            s_scr[...] = jnp.zeros_like(s_scr)
            ext[0:SUBLANES, :] = jnp.zeros((SUBLANES, ext.shape[1]), F32)

    cur = jnp.concatenate([x_ref[...], bc_ref[...]], axis=1)
    act = _pad_rows(_silu(_causal_conv(ext, cur, cw_ref) + cb_ref[...]), CHUNK)
    z = _pad_rows(z_ref[...], CHUNK)
    sm = _pad_rows(sm_ref[...], CHUNK)

    valid = _valid_rows(c, first_lo, hi)
    lane = lax.broadcasted_iota(jnp.int32, (1, CHUNK), 1)
    head_lane = (lane >= LANE_DT) & (lane < LANE_DT + SSD_HEADS)
    dt = _softplus(sm + dtb_ref[...])
    la = jnp.where(valid & head_lane, -jnp.exp(alog_ref[...]) * dt, 0.0)
    dt = jnp.where(valid, dt, 0.0)
    cum, cum_t, e_cum, e_rem, e_last = _decay_terms(la)

    xa = act[:, 0:GROUP_W]
    ys = []
    for g in range(SSD_GROUPS):
        b_g = act[:, GROUP_W + g * SSD_STATE:GROUP_W + (g + 1) * SSD_STATE]
        c_off = GROUP_W + SSD_GROUPS * SSD_STATE
        c_g = act[:, c_off + g * SSD_STATE:c_off + (g + 1) * SSD_STATE]
        scores = _dot_nt(c_g, b_g)
        for hh in range(SSD_HEADS // SSD_GROUPS):
            h = g * (SSD_HEADS // SSD_GROUPS) + hh
            ln = LANE_DT + h
            x_h = xa[:, h * SSD_HEAD_DIM:(h + 1) * SSD_HEAD_DIM]
            v_h = x_h * _col(dt, ln, SSD_HEAD_DIM)
            s_h = s_scr[h]
            o = (_dot(scores * _decay_matrix(cum, cum_t, ln), v_h)
                 + _dot(c_g * _col(e_cum, ln, SSD_STATE), s_h))
            k_w = b_g * _col(e_rem, ln, SSD_STATE)
            s_scr[h] = s_h * e_last[:, ln:ln + 1] + _dot(k_w.T, v_h)
            ys.append(o + d_ref[h] * x_h)
    y = jnp.concatenate(ys, axis=1) * _silu(z)
    gw = GROUP_W // SSD_GROUPS
    outs = []
    for g in range(SSD_GROUPS):
        y_g = y[:, g * gw:(g + 1) * gw]
        ms = jnp.mean(y_g * y_g, axis=-1, keepdims=True)
        outs.append(y_g * lax.rsqrt(ms + NORM_EPS) * nw_ref[:, g * gw:(g + 1) * gw])
    y_ref[...] = jnp.concatenate(outs, axis=1)[0:rows].astype(y_ref.dtype)

    @pl.when(c == nchunks - 1)
    def _():
        so_ref[0] = s_scr[...]


def _ret_kernel(rows, has_init, nchunks, first_lo, hi, *refs):
    if has_init:
        qk_ref, v_ref, g_ref, s0_ref, lg_ref, nw_ref, y_ref, so_ref, s_scr = refs
    else:
        qk_ref, v_ref, g_ref, lg_ref, nw_ref, y_ref, so_ref, s_scr = refs
    c = pl.program_id(1)

    @pl.when(c == 0)
    def _():
        if has_init:
            s_scr[...] = s0_ref[0]
        else:
            s_scr[...] = jnp.zeros_like(s_scr)

    qk = _pad_rows(qk_ref[...], CHUNK)
    v = _pad_rows(v_ref[...], CHUNK)
    gate = _pad_rows(g_ref[...], CHUNK)
    valid = _valid_rows(c, first_lo, hi)
    lane = lax.broadcasted_iota(jnp.int32, (1, CHUNK), 1)
    la = jnp.where(valid & (lane < RET_HEADS), jnp.broadcast_to(lg_ref[...], (CHUNK, CHUNK)), 0.0)
    cum, cum_t, e_cum, e_rem, e_last = _decay_terms(la)
    v = jnp.where(valid, v, 0.0)

    ys = []
    for h in range(RET_HEADS):
        q_h = qk[:, h * RET_DK:(h + 1) * RET_DK]
        k_h = qk[:, RET_HEADS * RET_DK + h * RET_DK:RET_HEADS * RET_DK + (h + 1) * RET_DK] * (RET_DK ** -0.5)
        v_h = v[:, h * RET_DV:(h + 1) * RET_DV]
        s_h = s_scr[h]
        o = (_dot(_dot_nt(q_h, k_h) * _decay_matrix(cum, cum_t, h), v_h)
             + _dot(q_h * _col(e_cum, h, RET_DK), s_h))
        k_w = k_h * _col(e_rem, h, RET_DK)
        s_scr[h] = s_h * e_last[:, h:h + 1] + _dot(k_w.T, v_h)
        ms = jnp.mean(o * o, axis=-1, keepdims=True)
        ys.append(o * lax.rsqrt(ms + NORM_EPS) * nw_ref[:, h * RET_DV:(h + 1) * RET_DV])
    y = jnp.concatenate(ys, axis=1) * _silu(gate)
    y_ref[...] = y[0:rows].astype(y_ref.dtype)

    @pl.when(c == nchunks - 1)
    def _():
        so_ref[0] = s_scr[...]


def _unit_lower_inverse(a):
    row, col = _iotas()
    t = (row == col).astype(F32)
    size, shift = 1, 0
    while size < CHUNK:
        same_big = lax.shift_right_logical(row, shift + 1) == lax.shift_right_logical(col, shift + 1)
        same_small = lax.shift_right_logical(row, shift) == lax.shift_right_logical(col, shift)
        off = jnp.where(same_big & jnp.logical_not(same_small) & (row > col), a, 0.0)
        t = t - _dot_f32(t, _dot_f32(off, t))
        size, shift = size * 2, shift + 1
    return t


def _gdn_kernel(rows, has_init, nchunks, first_lo, hi, *refs):
    if has_init:
        (q_ref, k_ref, v_ref, g_ref, sm_ref, s0_ref, c0_ref, cw_ref, dtb_ref, alog_ref, nw_ref,
         y_ref, so_ref, s_scr, ext) = refs
    else:
        (q_ref, k_ref, v_ref, g_ref, sm_ref, cw_ref, dtb_ref, alog_ref, nw_ref,
         y_ref, so_ref, s_scr, ext) = refs
    c = pl.program_id(1)

    @pl.when(c == 0)
    def _():
        if has_init:
            s_scr[...] = s0_ref[0]
            ext[0:SUBLANES, :] = c0_ref[0]
        else:
            s_scr[...] = jnp.zeros_like(s_scr)
            ext[0:SUBLANES, :] = jnp.zeros((SUBLANES, ext.shape[1]), F32)

    cur = jnp.concatenate([q_ref[...], k_ref[...], v_ref[...]], axis=1)
    act = _pad_rows(_silu(_causal_conv(ext, cur, cw_ref)), CHUNK)
    gate = _pad_rows(g_ref[...], CHUNK)
    sm = _pad_rows(sm_ref[...], CHUNK)

    valid = _valid_rows(c, first_lo, hi)
    lane = lax.broadcasted_iota(jnp.int32, (1, CHUNK), 1)
    head_lane = (lane >= LANE_DECAY) & (lane < LANE_DECAY + GDN_HEADS)
    beta = jnp.where(valid, _sigmoid(sm), 0.0)
    la = jnp.where(valid & head_lane, -jnp.exp(alog_ref[...]) * _softplus(sm + dtb_ref[...]), 0.0)
    cum, cum_t, e_cum, e_rem, e_last = _decay_terms(la)
    row, col = _iotas()

    ys = []
    for h in range(GDN_HEADS):
        ln = LANE_DECAY + h
        q_h = act[:, h * GDN_DK:(h + 1) * GDN_DK]
        k_h = act[:, GROUP_W + h * GDN_DK:GROUP_W + (h + 1) * GDN_DK]
        v_h = act[:, 2 * GROUP_W + h * GDN_DV:2 * GROUP_W + (h + 1) * GDN_DV]
        q_h = q_h * lax.rsqrt(jnp.sum(q_h * q_h, axis=-1, keepdims=True) + 1e-6) * (GDN_DK ** -0.5)
        k_h = k_h * lax.rsqrt(jnp.sum(k_h * k_h, axis=-1, keepdims=True) + 1e-6)
        k_h = jnp.where(valid, k_h, 0.0)
        b_h = _col(beta, LANE_BETA + h, GDN_DK)
        k_b = k_h * b_h
        v_b = jnp.where(valid, v_h, 0.0) * b_h
        decay = _decay_matrix(cum, cum_t, ln)
        a = jnp.where(row > col, _dot_nt(k_b, k_h) * decay, 0.0)
        t = _unit_lower_inverse(a)
        u = _dot_f32(t, v_b)
        w = _dot_f32(t, k_b * _col(e_cum, ln, GDN_DK))
        s_h = s_scr[h]
        v_new = u - _dot(w, s_h)
        o = _dot(q_h * _col(e_cum, ln, GDN_DK), s_h) + _dot(_dot_nt(q_h, k_h) * decay, v_new)
        k_w = k_h * _col(e_rem, ln, GDN_DK)
        s_scr[h] = s_h * e_last[:, ln:ln + 1] + _dot(k_w.T, v_new)
        ms = jnp.mean(o * o, axis=-1, keepdims=True)
        ys.append(o * lax.rsqrt(ms + NORM_EPS) * nw_ref[...])
    y = jnp.concatenate(ys, axis=1) * _silu(gate)
    y_ref[...] = y[0:rows].astype(y_ref.dtype)

    @pl.when(c == nchunks - 1)
    def _():
        so_ref[0] = s_scr[...]


def _swa_kernel(rows, is_prompt, *refs):
    if is_prompt:
        q_ref, g_ref, kvc_ref, kvp_ref, sink_ref, y_ref = refs
    else:
        q_ref, g_ref, kvc_ref, kp_ref, vp_ref, sink_ref, y_ref = refs
    n = pl.program_id(1)
    q = _pad_rows(q_ref[...], CHUNK)
    gate = _pad_rows(g_ref[...], CHUNK)
    kvc = _pad_rows(kvc_ref[...], CHUNK)
    kv_w = SWA_KV_HEADS * SWA_HEAD_DIM
    if is_prompt:
        k_prev, v_prev = kvp_ref[:, 0:kv_w], kvp_ref[:, kv_w:2 * kv_w]
        lo_prev = jnp.where(n == 0, CHUNK, jnp.where(n == 1, PROMPT_PAD, 0))
        lo_cur = jnp.where(n == 0, PROMPT_PAD, 0)
    else:
        k_prev, v_prev = kp_ref[0], vp_ref[0]
        lo_prev, lo_cur = 0, 0
    k_all = jnp.concatenate([k_prev, kvc[:, 0:kv_w]], axis=0)
    v_all = jnp.concatenate([v_prev, kvc[:, kv_w:2 * kv_w]], axis=0)

    qi = lax.broadcasted_iota(jnp.int32, (CHUNK, 2 * CHUNK), 0)
    kj = lax.broadcasted_iota(jnp.int32, (CHUNK, 2 * CHUNK), 1)
    dist = qi + WINDOW - kj
    key_ok = ((kj < CHUNK) & (kj >= lo_prev)) | (kj >= CHUNK + lo_cur)
    visible = (dist >= 0) & (dist <= WINDOW) & key_ok
    dist_f = dist.astype(F32)

    grp = SWA_HEADS // SWA_KV_HEADS
    ys = []
    for kvh in range(SWA_KV_HEADS):
        k_h = k_all[:, kvh * SWA_HEAD_DIM:(kvh + 1) * SWA_HEAD_DIM]
        v_h = v_all[:, kvh * SWA_HEAD_DIM:(kvh + 1) * SWA_HEAD_DIM]
        for g in range(grp):
            h = kvh * grp + g
            slope = 2.0 ** (-8.0 * (h + 1) / SWA_HEADS)
            q_h = q[:, h * SWA_HEAD_DIM:(h + 1) * SWA_HEAD_DIM]
            s = _dot_nt(q_h, k_h) * (SWA_HEAD_DIM ** -0.5) - slope * dist_f
            s = jnp.where(visible, s, -1e30)
            sink = sink_ref[h]
            m = jnp.maximum(jnp.max(s, axis=-1, keepdims=True), sink)
            e = jnp.exp(s - m)
            den = jnp.sum(e, axis=-1, keepdims=True) + jnp.exp(sink - m)
            ys.append(_dot(e, v_h) / den)
    y = jnp.concatenate(ys, axis=1) * _silu(gate)
    y_ref[...] = y[0:rows].astype(y_ref.dtype)


def _mixer_geometry(sample):
    if sample:
        grid = (DEC_BATCH, 1)
        rows = SUBLANES
        base = N_PROMPT // SUBLANES

        def rowblk(b, c):
            return base + b
        y_shape = jax.ShapeDtypeStruct((N_SAMPLE, GROUP_W), F32)
        return grid, rows, rowblk, y_shape, 0, DEC_SEQ
    grid = (BATCH, PROMPT_CHUNKS)
    rows = CHUNK

    def rowblk(b, c):
        return b * PROMPT_CHUNKS + c
    y_shape = jax.ShapeDtypeStruct((N_PROMPT, GROUP_W), BF16)
    return grid, rows, rowblk, y_shape, PROMPT_PAD, CHUNK


def _pcol(rows, rowblk, width, col):
    assert col % width == 0
    return pl.BlockSpec((rows, width), lambda b, c: (rowblk(b, c), col // width))


def _whole(shape):
    return pl.BlockSpec(shape, lambda b, c: (0,) * len(shape))


def _per_seq(shape):
    return pl.BlockSpec((1,) + shape, lambda b, c: (b,) + (0,) * len(shape))


_SMEM = pl.BlockSpec(memory_space=pltpu.SMEM)
_MIXER_PARAMS = pltpu.CompilerParams(dimension_semantics=("arbitrary", "arbitrary"))


def _ssd(p, prm, init):
    sample = init is not None
    grid, rows, rowblk, y_shape, first_lo, hi = _mixer_geometry(sample)
    nseq = grid[0]
    conv_w = GROUP_W + 2 * SSD_GROUPS * SSD_STATE
    state = (SSD_HEADS, SSD_STATE, SSD_HEAD_DIM)
    in_specs = [_pcol(rows, rowblk, GROUP_W, COL_Z), _pcol(rows, rowblk, GROUP_W, COL_X),
                _pcol(rows, rowblk, GROUP_W, COL_BC), _pcol(rows, rowblk, CHUNK, COL_SMALL)]
    args = [p, p, p, p]
    if sample:
        in_specs += [_per_seq(state), _per_seq((SUBLANES, conv_w))]
        args += list(init)
    in_specs += [_whole((CONV_K, conv_w)), _whole((1, conv_w)), _whole((1, CHUNK)), _whole((1, CHUNK)),
                 _whole((1, GROUP_W)), _SMEM]
    args += [prm["conv_w"], prm["conv_b"], prm["dt_bias"], prm["a_log"], prm["norm"], prm["d"]]
    return pl.pallas_call(
        functools.partial(_ssd_kernel, rows, sample, grid[1], first_lo, hi),
        grid=grid, in_specs=in_specs,
        out_specs=[pl.BlockSpec((rows, GROUP_W), lambda b, c: (rowblk(b, c) - (N_PROMPT // SUBLANES if sample else 0), 0)),
                   _per_seq(state)],
        out_shape=[y_shape, jax.ShapeDtypeStruct((nseq,) + state, F32)],
        scratch_shapes=[pltpu.VMEM(state, F32), pltpu.VMEM((SUBLANES + rows, conv_w), F32)],
        compiler_params=_MIXER_PARAMS,
        name="ssd_sample" if sample else "ssd_prompt",
    )(*args)


def _ret(p, prm, init):
    sample = init is not None
    grid, rows, rowblk, y_shape, first_lo, hi = _mixer_geometry(sample)
    nseq = grid[0]
    state = (RET_HEADS, RET_DK, RET_DV)
    in_specs = [_pcol(rows, rowblk, GROUP_W, COL_QKR), _pcol(rows, rowblk, GROUP_W, COL_VR),
                _pcol(rows, rowblk, GROUP_W, COL_GR)]
    args = [p, p, p]
    if sample:
        in_specs += [_per_seq(state)]
        args += list(init)
    in_specs += [_whole((1, CHUNK)), _whole((1, GROUP_W))]
    args += [prm["log_gamma"], prm["norm"]]
    return pl.pallas_call(
        functools.partial(_ret_kernel, rows, sample, grid[1], first_lo, hi),
        grid=grid, in_specs=in_specs,
        out_specs=[pl.BlockSpec((rows, GROUP_W), lambda b, c: (rowblk(b, c) - (N_PROMPT // SUBLANES if sample else 0), 0)),
                   _per_seq(state)],
        out_shape=[y_shape, jax.ShapeDtypeStruct((nseq,) + state, F32)],
        scratch_shapes=[pltpu.VMEM(state, F32)],
        compiler_params=_MIXER_PARAMS,
        name="ret_sample" if sample else "ret_prompt",
    )(*args)


def _gdn(p, prm, init):
    sample = init is not None
    grid, rows, rowblk, y_shape, first_lo, hi = _mixer_geometry(sample)
    nseq = grid[0]
    conv_w = 3 * GROUP_W
    state = (GDN_HEADS, GDN_DK, GDN_DV)
    in_specs = [_pcol(rows, rowblk, GROUP_W, COL_QD), _pcol(rows, rowblk, GROUP_W, COL_KD),
                _pcol(rows, rowblk, GROUP_W, COL_VD), _pcol(rows, rowblk, GROUP_W, COL_GD),
                _pcol(rows, rowblk, CHUNK, COL_SMALL)]
    args = [p, p, p, p, p]
    if sample:
        in_specs += [_per_seq(state), _per_seq((SUBLANES, conv_w))]
        args += list(init)
    in_specs += [_whole((CONV_K, conv_w)), _whole((1, CHUNK)), _whole((1, CHUNK)), _whole((1, GDN_DV))]
    args += [prm["conv_w"], prm["dt_bias"], prm["a_log"], prm["norm"]]
    return pl.pallas_call(
        functools.partial(_gdn_kernel, rows, sample, grid[1], first_lo, hi),
        grid=grid, in_specs=in_specs,
        out_specs=[pl.BlockSpec((rows, GROUP_W), lambda b, c: (rowblk(b, c) - (N_PROMPT // SUBLANES if sample else 0), 0)),
                   _per_seq(state)],
        out_shape=[y_shape, jax.ShapeDtypeStruct((nseq,) + state, F32)],
        scratch_shapes=[pltpu.VMEM(state, F32), pltpu.VMEM((SUBLANES + rows, conv_w), F32)],
        compiler_params=_MIXER_PARAMS,
        name="gdn_sample" if sample else "gdn_prompt",
    )(*args)


def _swa(p, sinks, cache):
    sample = cache is not None
    grid, rows, rowblk, y_shape, _, _ = _mixer_geometry(sample)
    kv_w = 2 * SWA_KV_HEADS * SWA_HEAD_DIM
    in_specs = [_pcol(rows, rowblk, GROUP_W, COL_QA), _pcol(rows, rowblk, GROUP_W, COL_GA),
                _pcol(rows, rowblk, kv_w, COL_KV)]
    args = [p, p, p]
    if sample:
        in_specs += [_per_seq((WINDOW, kv_w // 2)), _per_seq((WINDOW, kv_w // 2))]
        args += list(cache)
    else:
        in_specs += [pl.BlockSpec((rows, kv_w), lambda b, c: (rowblk(b, jnp.maximum(c - 1, 0)), COL_KV // kv_w))]
        args += [p]
    in_specs += [_SMEM]
    args += [sinks]
    return pl.pallas_call(
        functools.partial(_swa_kernel, rows, not sample),
        grid=grid, in_specs=in_specs,
        out_specs=pl.BlockSpec((rows, GROUP_W), lambda b, c: (rowblk(b, c) - (N_PROMPT // SUBLANES if sample else 0), 0)),
        out_shape=y_shape,
        compiler_params=_MIXER_PARAMS,
        name="swa_sample" if sample else "swa_prompt",
    )(*args)


def _lane_pad(v, lane):
    return jnp.zeros((1, CHUNK), F32).at[0, lane:lane + v.shape[0]].set(v.astype(F32))


def _permute_w_in(w_in):
    s = lambda a, b: w_in[:, :, a:b]
    pad = jnp.zeros(w_in.shape[:2] + (IN_W_PAD - 6416,), w_in.dtype)
    cols = [s(0, 1536),
            s(1544, 2056),
            s(2312, 2824),
            s(3336, 3848),
            s(3848, 4360),
            s(4360, 5896),
            s(5896, 6408),
            s(2824, 3336),
            s(2056, 2312),
            s(1536, 1544),
            s(6408, 6416),
            pad]
    return jnp.concatenate(cols, axis=-1).astype(BF16)


def _conv_state_pad(state):
    return jnp.pad(state, ((0, 0), (SUBLANES - (CONV_K - 1), 0), (0, 0)))


def kernel(x_prompt, x_sample, state_ssd, state_ssd_conv, cache_swa_k, cache_swa_v, state_ret, state_gdn,
           state_gdn_conv, meta_tokens, pre_norm, post_norm, w_in, w_out, ssd_conv_w, ssd_conv_b, ssd_dt_bias,
           ssd_a_log, ssd_d, ssd_norm, swa_sinks, ret_norm, gdn_conv_w, gdn_dt_bias, gdn_a_log, gdn_norm):
    meta = jnp.broadcast_to(meta_tokens[None].astype(F32), (BATCH, N_META, D_MODEL))
    xp = jnp.concatenate([jnp.zeros((BATCH, PROMPT_PAD, D_MODEL), F32), meta, x_prompt], axis=1)
    xs = jnp.pad(x_sample, ((0, 0), (0, SUBLANES - DEC_SEQ), (0, 0)))
    x = jnp.concatenate([xp.reshape(N_PROMPT, D_MODEL), xs.reshape(N_SAMPLE, D_MODEL)], axis=0)

    rp = jnp.arange(PROMPT_ROWS) >= PROMPT_PAD
    rs = jnp.arange(SUBLANES) < DEC_SEQ
    rowmask = jnp.concatenate([jnp.tile(rp, BATCH), jnp.tile(rs, DEC_BATCH)]).astype(F32)[:, None]

    w_in_p = _permute_w_in(w_in)
    w_out_b = w_out.astype(BF16)
    log_gamma = _lane_pad(jnp.log1p(-jnp.exp2(-5.0 - jnp.arange(RET_HEADS, dtype=F32))), 0)
    kv_w = SWA_KV_HEADS * SWA_HEAD_DIM

    outs = [[] for _ in range(14)]
    for l in range(DEPTH):
        p = _inproj(x, pre_norm[l][None], w_in_p[l])

        ssd_prm = dict(conv_w=ssd_conv_w[l], conv_b=ssd_conv_b[l][None], dt_bias=_lane_pad(ssd_dt_bias[l], LANE_DT),
                       a_log=_lane_pad(ssd_a_log[l], LANE_DT), norm=ssd_norm[l][None], d=ssd_d[l])
        ret_prm = dict(log_gamma=log_gamma, norm=ret_norm[l][None])
        gdn_prm = dict(conv_w=gdn_conv_w[l], dt_bias=_lane_pad(gdn_dt_bias[l], LANE_DECAY),
                       a_log=_lane_pad(gdn_a_log[l], LANE_DECAY), norm=gdn_norm[l][None])

        ya_p, p_ssd = _ssd(p, ssd_prm, None)
        yb_p = _swa(p, swa_sinks[l], None)
        yc_p, p_ret = _ret(p, ret_prm, None)
        yd_p, p_gdn = _gdn(p, gdn_prm, None)

        ya_s, s_ssd = _ssd(p, ssd_prm, (state_ssd[l], _conv_state_pad(state_ssd_conv[l])))
        yb_s = _swa(p, swa_sinks[l], (cache_swa_k[l].reshape(DEC_BATCH, WINDOW, kv_w),
                                      cache_swa_v[l].reshape(DEC_BATCH, WINDOW, kv_w)))
        yc_s, s_ret = _ret(p, ret_prm, (state_ret[l],))
        yd_s, s_gdn = _gdn(p, gdn_prm, (state_gdn[l], _conv_state_pad(state_gdn_conv[l])))

        ys = [jnp.concatenate([a, b.astype(BF16)], axis=0)
              for a, b in ((ya_p, ya_s), (yb_p, yb_s), (yc_p, yc_s), (yd_p, yd_s))]
        x = _outproj(ys, w_out_b[l], x, post_norm[l][None], rowmask)

        pp = p[:N_PROMPT].reshape(BATCH, PROMPT_ROWS, IN_W_PAD)
        ps = p[N_PROMPT:].reshape(DEC_BATCH, SUBLANES, IN_W_PAD)
        tail = slice(PROMPT_ROWS - (CONV_K - 1), PROMPT_ROWS)
        win = slice(PROMPT_ROWS - WINDOW, PROMPT_ROWS)
        k_new = ps[:, :DEC_SEQ, COL_KV:COL_KV + kv_w].reshape(DEC_BATCH, DEC_SEQ, SWA_KV_HEADS, SWA_HEAD_DIM)
        v_new = ps[:, :DEC_SEQ, COL_KV + kv_w:COL_KV + 2 * kv_w].reshape(DEC_BATCH, DEC_SEQ, SWA_KV_HEADS, SWA_HEAD_DIM)
        layer_out = [
            p_ssd,
            pp[:, tail, COL_X:COL_X + 2 * GROUP_W],
            pp[:, win, COL_KV:COL_KV + kv_w].reshape(BATCH, WINDOW, SWA_KV_HEADS, SWA_HEAD_DIM),
            pp[:, win, COL_KV + kv_w:COL_KV + 2 * kv_w].reshape(BATCH, WINDOW, SWA_KV_HEADS, SWA_HEAD_DIM),
            p_ret,
            p_gdn,
            pp[:, tail, COL_QD:COL_QD + 3 * GROUP_W],
            s_ssd,
            ps[:, 1:DEC_SEQ, COL_X:COL_X + 2 * GROUP_W],
            jnp.concatenate([cache_swa_k[l][:, DEC_SEQ:], k_new], axis=1),
            jnp.concatenate([cache_swa_v[l][:, DEC_SEQ:], v_new], axis=1),
            s_ret,
            s_gdn,
            ps[:, 1:DEC_SEQ, COL_QD:COL_QD + 3 * GROUP_W],
        ]
        for acc, o in zip(outs, layer_out):
            acc.append(o)

    y_prompt = x[:N_PROMPT].reshape(BATCH, PROMPT_ROWS, D_MODEL)[:, PROMPT_PAD + N_META:]
    y_sample = x[N_PROMPT:].reshape(DEC_BATCH, SUBLANES, D_MODEL)[:, :DEC_SEQ]
    return (y_prompt, y_sample) + tuple(jnp.stack(o) for o in outs)
```

```python
import functools

import jax
import jax.numpy as jnp
import numpy as np
from jax import lax
from jax.experimental import pallas as pl
from jax.experimental.pallas import tpu as pltpu

F32 = jnp.float32
BF16 = jnp.bfloat16
HIGHEST = lax.Precision.HIGHEST

D_MODEL = 2048
BATCH = 4
SEQ = 2048
DEPTH = 4
DEC_BATCH = 32
DEC_SEQ = 4
N_META = 16
GROUP_W = 512
CONV_K = 4
NORM_EPS = 1e-6
WINDOW = 128

SSD_HEADS, SSD_HEAD_DIM, SSD_GROUPS, SSD_STATE = 8, 64, 2, 128
SWA_HEADS, SWA_KV_HEADS, SWA_HEAD_DIM = 8, 2, 64
RET_HEADS, RET_DK, RET_DV = 4, 64, 128
GDN_HEADS, GDN_DK, GDN_DV = 4, 128, 128
SSD_CONV_W = GROUP_W + 2 * SSD_GROUPS * SSD_STATE
GDN_CONV_W = 3 * GROUP_W
KV_W = SWA_KV_HEADS * SWA_HEAD_DIM

CHUNK = 128
SUBLANES = 8
PROMPT_PAD = CHUNK - N_META
PROMPT_ROWS = PROMPT_PAD + N_META + SEQ
PROMPT_CHUNKS = PROMPT_ROWS // CHUNK
N_PROMPT = BATCH * PROMPT_ROWS
N_SAMPLE = DEC_BATCH * SUBLANES

IN_W = 6416
IN_W_PAD = 6528
COL_Z, COL_X, COL_BC, COL_QA, COL_GA, COL_VR, COL_GR = 0, 512, 1024, 1536, 2048, 2560, 3072
COL_QD, COL_KD, COL_VD, COL_GD, COL_QKR, COL_KV, COL_SMALL = 3584, 4096, 4608, 5120, 5632, 6144, 6400
LANE_DT, LANE_BETA, LANE_DECAY = 0, 8, 12
ROW_SSD_DTB, ROW_SSD_ALOG, ROW_GDN_DTB, ROW_GDN_ALOG, ROW_LOG_GAMMA = 0, 1, 2, 3, 4

PROJ_TN = 2176
PROMPT_TM = 512
VMEM_LIMIT = 48 * 1024 * 1024


def _sigmoid(x):
    return 1.0 / (1.0 + jnp.exp(-x))


def _silu(x):
    return x * _sigmoid(x)


def _softplus(x):
    return jnp.maximum(x, 0.0) + jnp.log1p(jnp.exp(-jnp.abs(x)))


def _dot(a, b):
    return jnp.dot(a.astype(BF16), b.astype(BF16), preferred_element_type=F32)


def _dot_nt(a, b):
    return lax.dot_general(a.astype(BF16), b.astype(BF16), (((1,), (1,)), ((), ())),
                           preferred_element_type=F32)


def _dot_f32(a, b):
    return jnp.dot(a, b, precision=HIGHEST, preferred_element_type=F32)


def _pad_rows(a, rows):
    if a.shape[0] == rows:
        return a
    return jnp.concatenate([a, jnp.zeros((rows - a.shape[0], a.shape[1]), a.dtype)], axis=0)


def _col(a, lane, width):
    return jnp.broadcast_to(a[:, lane:lane + 1], (a.shape[0], width))


def _iotas():
    row = lax.broadcasted_iota(jnp.int32, (CHUNK, CHUNK), 0)
    col = lax.broadcasted_iota(jnp.int32, (CHUNK, CHUNK), 1)
    return row, col


def _valid_rows(chunk_idx, first_lo, hi):
    r = lax.broadcasted_iota(jnp.int32, (CHUNK, 1), 0)
    lo = jnp.where(chunk_idx == 0, first_lo, 0)
    return (r >= lo) & (r < hi)


def _decay_terms(la):
    row, col = _iotas()
    lower = (row >= col).astype(F32)
    cum = _dot_f32(lower, la)
    last = cum[CHUNK - 1:CHUNK, :]
    return cum, cum.T, jnp.exp(cum), jnp.exp(last - cum), jnp.exp(last)


def _decay_matrix(cum, cum_t, lane):
    row, col = _iotas()
    keep = row >= col
    seg = _col(cum, lane, CHUNK) - jnp.broadcast_to(cum_t[lane:lane + 1, :], (CHUNK, CHUNK))
    return jnp.where(keep, jnp.exp(jnp.where(keep, seg, 0.0)), 0.0)


def _causal_conv(ext_ref, cur, w_ref):
    rows = cur.shape[0]
    ext_ref[SUBLANES:SUBLANES + rows, :] = cur
    acc = cur * w_ref[CONV_K - 1:CONV_K, :]
    for j in range(CONV_K - 1):
        start = SUBLANES - (CONV_K - 1) + j
        acc = acc + ext_ref[start:start + rows, :] * w_ref[j:j + 1, :]
    ext_ref[0:SUBLANES, :] = ext_ref[rows:rows + SUBLANES, :]
    return acc


def _inproj_kernel(x_ref, g_ref, w_ref, o_ref):
    x = x_ref[...]
    ms = jnp.mean(x * x, axis=-1, keepdims=True)
    h = (x * lax.rsqrt(ms + NORM_EPS) * g_ref[...]).astype(BF16)
    o_ref[...] = jnp.dot(h, w_ref[...], preferred_element_type=F32)


def _inproj(x, g_all, w_all, layer, tm):
    n = x.shape[0]
    return pl.pallas_call(
        _inproj_kernel,
        grid=(IN_W_PAD // PROJ_TN, n // tm),
        in_specs=[
            pl.BlockSpec((tm, D_MODEL), lambda j, i: (i, 0)),
            pl.BlockSpec((None, 1, D_MODEL), lambda j, i: (layer, 0, 0)),
            pl.BlockSpec((None, D_MODEL, PROJ_TN), lambda j, i: (layer, 0, j)),
        ],
        out_specs=pl.BlockSpec((tm, PROJ_TN), lambda j, i: (i, j)),
        out_shape=jax.ShapeDtypeStruct((n, IN_W_PAD), F32),
        compiler_params=pltpu.CompilerParams(
            dimension_semantics=("arbitrary", "arbitrary"), vmem_limit_bytes=VMEM_LIMIT),
        name="inproj",
    )(x, g_all, w_all)


def _outproj_kernel(ya_ref, yb_ref, yc_ref, yd_ref, w_ref, x_ref, g_ref, m_ref, o_ref):
    acc = None
    for g, y_ref in enumerate((ya_ref, yb_ref, yc_ref, yd_ref)):
        part = jnp.dot(y_ref[...].astype(BF16), w_ref[g * GROUP_W:(g + 1) * GROUP_W, :],
                       preferred_element_type=F32)
        acc = part if acc is None else acc + part
    ms = jnp.mean(acc * acc, axis=-1, keepdims=True)
    r = acc * lax.rsqrt(ms + NORM_EPS) * g_ref[...]
    o_ref[...] = jnp.where(m_ref[...] > 0.0, x_ref[...] + r, 0.0)


def _outproj(ys, w_all, x, g_all, rowmask, layer, tm):
    n = x.shape[0]
    yspec = pl.BlockSpec((tm, GROUP_W), lambda i: (i, 0))
    return pl.pallas_call(
        _outproj_kernel,
        grid=(n // tm,),
        in_specs=[yspec, yspec, yspec, yspec,
                  pl.BlockSpec((None, D_MODEL, D_MODEL), lambda i: (layer, 0, 0)),
                  pl.BlockSpec((tm, D_MODEL), lambda i: (i, 0)),
                  pl.BlockSpec((None, 1, D_MODEL), lambda i: (layer, 0, 0)),
                  pl.BlockSpec((tm, 1), lambda i: (i, 0))],
        out_specs=pl.BlockSpec((tm, D_MODEL), lambda i: (i, 0)),
        out_shape=jax.ShapeDtypeStruct((n, D_MODEL), F32),
        compiler_params=pltpu.CompilerParams(
            dimension_semantics=("arbitrary",), vmem_limit_bytes=VMEM_LIMIT),
        name="outproj",
    )(*ys, w_all, x, g_all, rowmask)


def _ssd_kernel(rows, layer, has_init, nchunks, first_lo, hi, *refs):
    if has_init:
        (z_ref, x_ref, bc_ref, sm_ref, s0_ref, c0_ref, cw_ref, cb_ref, lp_ref, nw_ref, d_ref,
         y_ref, so_ref, s_scr, ext) = refs
    else:
        (z_ref, x_ref, bc_ref, sm_ref, cw_ref, cb_ref, lp_ref, nw_ref, d_ref,
         y_ref, so_ref, s_scr, ext) = refs
    c = pl.program_id(1)

    @pl.when(c == 0)
    def _():
        if has_init:
            s_scr[...] = s0_ref[0]
            ext[0:SUBLANES, :] = c0_ref[0]
        else:
            s_scr[...] = jnp.zeros_like(s_scr)
            ext[0:SUBLANES, :] = jnp.zeros((SUBLANES, ext.shape[1]), F32)

    cur = jnp.concatenate([x_ref[...], bc_ref[...]], axis=1)
    act = _pad_rows(_silu(_causal_conv(ext, cur, cw_ref) + cb_ref[...]), CHUNK)
    z = _pad_rows(z_ref[...], CHUNK)
    sm = _pad_rows(sm_ref[...], CHUNK)

    valid = _valid_rows(c, first_lo, hi)
    lane = lax.broadcasted_iota(jnp.int32, (1, CHUNK), 1)
    head_lane = (lane >= LANE_DT) & (lane < LANE_DT + SSD_HEADS)
    dt = _softplus(sm + lp_ref[ROW_SSD_DTB:ROW_SSD_DTB + 1, :])
    la = jnp.where(valid & head_lane, -jnp.exp(lp_ref[ROW_SSD_ALOG:ROW_SSD_ALOG + 1, :]) * dt, 0.0)
    dt = jnp.where(valid, dt, 0.0)
    cum, cum_t, e_cum, e_rem, e_last = _decay_terms(la)

    xa = act[:, 0:GROUP_W]
    ys = []
    for g in range(SSD_GROUPS):
        b_g = act[:, GROUP_W + g * SSD_STATE:GROUP_W + (g + 1) * SSD_STATE]
        c_off = GROUP_W + SSD_GROUPS * SSD_STATE
        c_g = act[:, c_off + g * SSD_STATE:c_off + (g + 1) * SSD_STATE]
        scores = _dot_nt(c_g, b_g)
        for hh in range(SSD_HEADS // SSD_GROUPS):
            h = g * (SSD_HEADS // SSD_GROUPS) + hh
            ln = LANE_DT + h
            x_h = xa[:, h * SSD_HEAD_DIM:(h + 1) * SSD_HEAD_DIM]
            v_h = x_h * _col(dt, ln, SSD_HEAD_DIM)
            s_h = s_scr[h]
            o = (_dot(scores * _decay_matrix(cum, cum_t, ln), v_h)
                 + _dot(c_g * _col(e_cum, ln, SSD_STATE), s_h))
            k_w = b_g * _col(e_rem, ln, SSD_STATE)
            s_scr[h] = s_h * e_last[:, ln:ln + 1] + _dot(k_w.T, v_h)
            ys.append(o + d_ref[layer, h] * x_h)
    y = jnp.concatenate(ys, axis=1) * _silu(z)
    gw = GROUP_W // SSD_GROUPS
    outs = []
    for g in range(SSD_GROUPS):
        y_g = y[:, g * gw:(g + 1) * gw]
        ms = jnp.mean(y_g * y_g, axis=-1, keepdims=True)
        outs.append(y_g * lax.rsqrt(ms + NORM_EPS) * nw_ref[:, g * gw:(g + 1) * gw])
    y_ref[...] = jnp.concatenate(outs, axis=1)[0:rows].astype(y_ref.dtype)

    @pl.when(c == nchunks - 1)
    def _():
        so_ref[0] = s_scr[...]


def _ret_kernel(rows, has_init, nchunks, first_lo, hi, *refs):
    if has_init:
        qk_ref, v_ref, g_ref, s0_ref, lp_ref, nw_ref, y_ref, so_ref, s_scr = refs
    else:
        qk_ref, v_ref, g_ref, lp_ref, nw_ref, y_ref, so_ref, s_scr = refs
    c = pl.program_id(1)

    @pl.when(c == 0)
    def _():
        if has_init:
            s_scr[...] = s0_ref[0]
        else:
            s_scr[...] = jnp.zeros_like(s_scr)

    qk = _pad_rows(qk_ref[...], CHUNK)
    v = _pad_rows(v_ref[...], CHUNK)
    gate = _pad_rows(g_ref[...], CHUNK)
    valid = _valid_rows(c, first_lo, hi)
    lane = lax.broadcasted_iota(jnp.int32, (1, CHUNK), 1)
    log_gamma = jnp.broadcast_to(lp_ref[ROW_LOG_GAMMA:ROW_LOG_GAMMA + 1, :], (CHUNK, CHUNK))
    la = jnp.where(valid & (lane < RET_HEADS), log_gamma, 0.0)
    cum, cum_t, e_cum, e_rem, e_last = _decay_terms(la)
    v = jnp.where(valid, v, 0.0)

    ys = []
    for h in range(RET_HEADS):
        q_h = qk[:, h * RET_DK:(h + 1) * RET_DK]
        k_h = qk[:, RET_HEADS * RET_DK + h * RET_DK:RET_HEADS * RET_DK + (h + 1) * RET_DK] * (RET_DK ** -0.5)
        v_h = v[:, h * RET_DV:(h + 1) * RET_DV]
        s_h = s_scr[h]
        o = (_dot(_dot_nt(q_h, k_h) * _decay_matrix(cum, cum_t, h), v_h)
             + _dot(q_h * _col(e_cum, h, RET_DK), s_h))
        k_w = k_h * _col(e_rem, h, RET_DK)
        s_scr[h] = s_h * e_last[:, h:h + 1] + _dot(k_w.T, v_h)
        ms = jnp.mean(o * o, axis=-1, keepdims=True)
        ys.append(o * lax.rsqrt(ms + NORM_EPS) * nw_ref[:, h * RET_DV:(h + 1) * RET_DV])
    y = jnp.concatenate(ys, axis=1) * _silu(gate)
    y_ref[...] = y[0:rows].astype(y_ref.dtype)

    @pl.when(c == nchunks - 1)
    def _():
        so_ref[0] = s_scr[...]


def _unit_lower_inverses(mats):
    row, col = _iotas()

    def joining_mask(shift):
        same_big = lax.shift_right_logical(row, shift + 1) == lax.shift_right_logical(col, shift + 1)
        same_small = lax.shift_right_logical(row, shift) == lax.shift_right_logical(col, shift)
        return same_big & jnp.logical_not(same_small) & (row > col)

    eye = (row == col).astype(F32)
    mask = joining_mask(0)
    ts = [eye - jnp.where(mask, a, 0.0) for a in mats]
    shift = 1
    while (1 << shift) < CHUNK:
        mask = joining_mask(shift)
        inner = [_dot(jnp.where(mask, a, 0.0), t) for a, t in zip(mats, ts)]
        ts = [t - _dot(t, x) for t, x in zip(ts, inner)]
        shift += 1
    return ts


def _gdn_kernel(rows, has_init, nchunks, first_lo, hi, *refs):
    if has_init:
        (q_ref, k_ref, v_ref, g_ref, sm_ref, s0_ref, c0_ref, cw_ref, lp_ref, nw_ref,
         y_ref, so_ref, s_scr, ext) = refs
    else:
        (q_ref, k_ref, v_ref, g_ref, sm_ref, cw_ref, lp_ref, nw_ref,
         y_ref, so_ref, s_scr, ext) = refs
    c = pl.program_id(1)

    @pl.when(c == 0)
    def _():
        if has_init:
            s_scr[...] = s0_ref[0]
            ext[0:SUBLANES, :] = c0_ref[0]
        else:
            s_scr[...] = jnp.zeros_like(s_scr)
            ext[0:SUBLANES, :] = jnp.zeros((SUBLANES, ext.shape[1]), F32)

    cur = jnp.concatenate([q_ref[...], k_ref[...], v_ref[...]], axis=1)
    act = _pad_rows(_silu(_causal_conv(ext, cur, cw_ref)), CHUNK)
    gate = _pad_rows(g_ref[...], CHUNK)
    sm = _pad_rows(sm_ref[...], CHUNK)

    valid = _valid_rows(c, first_lo, hi)
    lane = lax.broadcasted_iota(jnp.int32, (1, CHUNK), 1)
    head_lane = (lane >= LANE_DECAY) & (lane < LANE_DECAY + GDN_HEADS)
    beta = jnp.where(valid, _sigmoid(sm), 0.0)
    la = jnp.where(valid & head_lane,
                   -jnp.exp(lp_ref[ROW_GDN_ALOG:ROW_GDN_ALOG + 1, :])
                   * _softplus(sm + lp_ref[ROW_GDN_DTB:ROW_GDN_DTB + 1, :]), 0.0)
    cum, cum_t, e_cum, e_rem, e_last = _decay_terms(la)
    row, col = _iotas()

    heads = []
    for h in range(GDN_HEADS):
        ln = LANE_DECAY + h
        q_h = act[:, h * GDN_DK:(h + 1) * GDN_DK]
        k_h = act[:, GROUP_W + h * GDN_DK:GROUP_W + (h + 1) * GDN_DK]
        v_h = act[:, 2 * GROUP_W + h * GDN_DV:2 * GROUP_W + (h + 1) * GDN_DV]
        q_h = q_h * lax.rsqrt(jnp.sum(q_h * q_h, axis=-1, keepdims=True) + 1e-6) * (GDN_DK ** -0.5)
        k_h = k_h * lax.rsqrt(jnp.sum(k_h * k_h, axis=-1, keepdims=True) + 1e-6)
        k_h = jnp.where(valid, k_h, 0.0)
        b_h = _col(beta, LANE_BETA + h, GDN_DK)
        k_b = k_h * b_h
        v_b = jnp.where(valid, v_h, 0.0) * b_h
        decay = _decay_matrix(cum, cum_t, ln)
        a = jnp.where(row > col, _dot_nt(k_b, k_h) * decay, 0.0)
        s_h = s_scr[h]
        heads.append(dict(
            a=a, v_b=v_b, s=s_h,
            k_bw=k_b * _col(e_cum, ln, GDN_DK),
            qk=_dot_nt(q_h, k_h) * decay,
            q_s=_dot(q_h * _col(e_cum, ln, GDN_DK), s_h),
            k_wt=(k_h * _col(e_rem, ln, GDN_DK)).T,
            s_dec=s_h * e_last[:, ln:ln + 1]))
    ts = _unit_lower_inverses([hd["a"] for hd in heads])
    us = [_dot(t, hd["v_b"]) for t, hd in zip(ts, heads)]
    ws = [_dot(t, hd["k_bw"]) for t, hd in zip(ts, heads)]
    v_news = [u - _dot(w, hd["s"]) for u, w, hd in zip(us, ws, heads)]
    os_ = [hd["q_s"] + _dot(hd["qk"], v_new) for hd, v_new in zip(heads, v_news)]
    for h, (hd, v_new) in enumerate(zip(heads, v_news)):
        s_scr[h] = hd["s_dec"] + _dot(hd["k_wt"], v_new)
    ys = []
    for o in os_:
        ms = jnp.mean(o * o, axis=-1, keepdims=True)
        ys.append(o * lax.rsqrt(ms + NORM_EPS) * nw_ref[...])
    y = jnp.concatenate(ys, axis=1) * _silu(gate)
    y_ref[...] = y[0:rows].astype(y_ref.dtype)

    @pl.when(c == nchunks - 1)
    def _():
        so_ref[0] = s_scr[...]


def _swa_kernel(rows, layer, is_prompt, *refs):
    if is_prompt:
        q_ref, g_ref, kvc_ref, kvp_ref, sink_ref, y_ref = refs
    else:
        q_ref, g_ref, kvc_ref, kp_ref, vp_ref, sink_ref, y_ref = refs
    n = pl.program_id(1)
    q = _pad_rows(q_ref[...], CHUNK)
    gate = _pad_rows(g_ref[...], CHUNK)
    kvc = _pad_rows(kvc_ref[...], CHUNK)
    if is_prompt:
        k_prev, v_prev = kvp_ref[:, 0:KV_W], kvp_ref[:, KV_W:2 * KV_W]
        lo_prev = jnp.where(n == 0, CHUNK, jnp.where(n == 1, PROMPT_PAD, 0))
        lo_cur = jnp.where(n == 0, PROMPT_PAD, 0)
    else:
        k_prev, v_prev = kp_ref[0], vp_ref[0]
        lo_prev, lo_cur = 0, 0
    k_all = jnp.concatenate([k_prev, kvc[:, 0:KV_W]], axis=0)
    v_all = jnp.concatenate([v_prev, kvc[:, KV_W:2 * KV_W]], axis=0)

    qi = lax.broadcasted_iota(jnp.int32, (CHUNK, 2 * CHUNK), 0)
    kj = lax.broadcasted_iota(jnp.int32, (CHUNK, 2 * CHUNK), 1)
    dist = qi + WINDOW - kj
    key_ok = ((kj < CHUNK) & (kj >= lo_prev)) | (kj >= CHUNK + lo_cur)
    visible = (dist >= 0) & (dist <= WINDOW) & key_ok
    dist_f = dist.astype(F32)

    grp = SWA_HEADS // SWA_KV_HEADS
    ys = []
    for kvh in range(SWA_KV_HEADS):
        k_h = k_all[:, kvh * SWA_HEAD_DIM:(kvh + 1) * SWA_HEAD_DIM]
        v_h = v_all[:, kvh * SWA_HEAD_DIM:(kvh + 1) * SWA_HEAD_DIM]
        for g in range(grp):
            h = kvh * grp + g
            slope = 2.0 ** (-8.0 * (h + 1) / SWA_HEADS)
            q_h = q[:, h * SWA_HEAD_DIM:(h + 1) * SWA_HEAD_DIM]
            s = _dot_nt(q_h, k_h) * (SWA_HEAD_DIM ** -0.5) - slope * dist_f
            s = jnp.where(visible, s, -1e30)
            sink = sink_ref[layer, h]
            m = jnp.maximum(jnp.max(s, axis=-1, keepdims=True), sink)
            e = jnp.exp(s - m)
            den = jnp.sum(e, axis=-1, keepdims=True) + jnp.exp(sink - m)
            ys.append(_dot(e, v_h) / den)
    y = jnp.concatenate(ys, axis=1) * _silu(gate)
    y_ref[...] = y[0:rows].astype(y_ref.dtype)


def _geometry(sample):
    if sample:
        return ((DEC_BATCH, 1), SUBLANES, lambda b, c: b,
                jax.ShapeDtypeStruct((N_SAMPLE, GROUP_W), F32), 0, DEC_SEQ)
    return ((BATCH, PROMPT_CHUNKS), CHUNK, lambda b, c: b * PROMPT_CHUNKS + c,
            jax.ShapeDtypeStruct((N_PROMPT, GROUP_W), BF16), PROMPT_PAD, CHUNK)


def _pcol(rows, rowblk, width, col):
    assert col % width == 0
    return pl.BlockSpec((rows, width), lambda b, c: (rowblk(b, c), col // width))


def _layer_param(shape, layer):
    return pl.BlockSpec((None,) + shape, lambda b, c: (layer,) + (0,) * len(shape))


def _layer_seq(shape, layer):
    return pl.BlockSpec((None, 1) + shape, lambda b, c: (layer, b) + (0,) * len(shape))


def _seq_out(shape):
    return pl.BlockSpec((1,) + shape, lambda b, c: (b,) + (0,) * len(shape))


_SMEM = pl.BlockSpec(memory_space=pltpu.SMEM)
_MIXER_PARAMS = pltpu.CompilerParams(dimension_semantics=("arbitrary", "arbitrary"))


def _ssd(p, prm, layer, init):
    sample = init is not None
    grid, rows, rowblk, y_shape, first_lo, hi = _geometry(sample)
    state = (SSD_HEADS, SSD_STATE, SSD_HEAD_DIM)
    in_specs = [_pcol(rows, rowblk, GROUP_W, COL_Z), _pcol(rows, rowblk, GROUP_W, COL_X),
                _pcol(rows, rowblk, GROUP_W, COL_BC), _pcol(rows, rowblk, CHUNK, COL_SMALL)]
    args = [p, p, p, p]
    if sample:
        in_specs += [_layer_seq(state, layer), _layer_seq((SUBLANES, SSD_CONV_W), layer)]
        args += list(init)
    in_specs += [_layer_param((CONV_K, SSD_CONV_W), layer), _layer_param((1, SSD_CONV_W), layer),
                 _layer_param((SUBLANES, CHUNK), layer), _layer_param((1, GROUP_W), layer), _SMEM]
    args += [prm["ssd_conv_w"], prm["ssd_conv_b"], prm["lanes"], prm["ssd_norm"], prm["ssd_d"]]
    return pl.pallas_call(
        functools.partial(_ssd_kernel, rows, layer, sample, grid[1], first_lo, hi),
        grid=grid, in_specs=in_specs,
        out_specs=[pl.BlockSpec((rows, GROUP_W), lambda b, c: (rowblk(b, c), 0)), _seq_out(state)],
        out_shape=[y_shape, jax.ShapeDtypeStruct((grid[0],) + state, F32)],
        scratch_shapes=[pltpu.VMEM(state, F32), pltpu.VMEM((SUBLANES + rows, SSD_CONV_W), F32)],
        compiler_params=_MIXER_PARAMS,
        name="ssd_sample" if sample else "ssd_prompt",
    )(*args)


def _ret(p, prm, layer, init):
    sample = init is not None
    grid, rows, rowblk, y_shape, first_lo, hi = _geometry(sample)
    state = (RET_HEADS, RET_DK, RET_DV)
    in_specs = [_pcol(rows, rowblk, GROUP_W, COL_QKR), _pcol(rows, rowblk, GROUP_W, COL_VR),
                _pcol(rows, rowblk, GROUP_W, COL_GR)]
    args = [p, p, p]
    if sample:
        in_specs += [_layer_seq(state, layer)]
        args += list(init)
    in_specs += [_layer_param((SUBLANES, CHUNK), layer), _layer_param((1, GROUP_W), layer)]
    args += [prm["lanes"], prm["ret_norm"]]
    return pl.pallas_call(
        functools.partial(_ret_kernel, rows, sample, grid[1], first_lo, hi),
        grid=grid, in_specs=in_specs,
        out_specs=[pl.BlockSpec((rows, GROUP_W), lambda b, c: (rowblk(b, c), 0)), _seq_out(state)],
        out_shape=[y_shape, jax.ShapeDtypeStruct((grid[0],) + state, F32)],
        scratch_shapes=[pltpu.VMEM(state, F32)],
        compiler_params=_MIXER_PARAMS,
        name="ret_sample" if sample else "ret_prompt",
    )(*args)


def _gdn(p, prm, layer, init):
    sample = init is not None
    grid, rows, rowblk, y_shape, first_lo, hi = _geometry(sample)
    state = (GDN_HEADS, GDN_DK, GDN_DV)
    in_specs = [_pcol(rows, rowblk, GROUP_W, COL_QD), _pcol(rows, rowblk, GROUP_W, COL_KD),
                _pcol(rows, rowblk, GROUP_W, COL_VD), _pcol(rows, rowblk, GROUP_W, COL_GD),
                _pcol(rows, rowblk, CHUNK, COL_SMALL)]
    args = [p, p, p, p, p]
    if sample:
        in_specs += [_layer_seq(state, layer), _layer_seq((SUBLANES, GDN_CONV_W), layer)]
        args += list(init)
    in_specs += [_layer_param((CONV_K, GDN_CONV_W), layer), _layer_param((SUBLANES, CHUNK), layer),
                 _layer_param((1, GDN_DV), layer)]
    args += [prm["gdn_conv_w"], prm["lanes"], prm["gdn_norm"]]
    return pl.pallas_call(
        functools.partial(_gdn_kernel, rows, sample, grid[1], first_lo, hi),
        grid=grid, in_specs=in_specs,
        out_specs=[pl.BlockSpec((rows, GROUP_W), lambda b, c: (rowblk(b, c), 0)), _seq_out(state)],
        out_shape=[y_shape, jax.ShapeDtypeStruct((grid[0],) + state, F32)],
        scratch_shapes=[pltpu.VMEM(state, F32), pltpu.VMEM((SUBLANES + rows, GDN_CONV_W), F32)],
        compiler_params=_MIXER_PARAMS,
        name="gdn_sample" if sample else "gdn_prompt",
    )(*args)


def _swa(p, prm, layer, cache):
    sample = cache is not None
    grid, rows, rowblk, y_shape, _, _ = _geometry(sample)
    in_specs = [_pcol(rows, rowblk, GROUP_W, COL_QA), _pcol(rows, rowblk, GROUP_W, COL_GA),
                _pcol(rows, rowblk, 2 * KV_W, COL_KV)]
    args = [p, p, p]
    if sample:
        in_specs += [_layer_seq((WINDOW, KV_W), layer), _layer_seq((WINDOW, KV_W), layer)]
        args += list(cache)
    else:
        in_specs += [pl.BlockSpec((rows, 2 * KV_W),
                                  lambda b, c: (rowblk(b, jnp.maximum(c - 1, 0)), COL_KV // (2 * KV_W)))]
        args += [p]
    in_specs += [_SMEM]
    args += [prm["swa_sinks"]]
    return pl.pallas_call(
        functools.partial(_swa_kernel, rows, layer, not sample),
        grid=grid, in_specs=in_specs,
        out_specs=pl.BlockSpec((rows, GROUP_W), lambda b, c: (rowblk(b, c), 0)),
        out_shape=y_shape,
        compiler_params=_MIXER_PARAMS,
        name="swa_sample" if sample else "swa_prompt",
    )(*args)


def _permute_w_in(w_in):
    s = lambda a, b: w_in[:, :, a:b]
    pad = jnp.zeros(w_in.shape[:2] + (IN_W_PAD - IN_W,), w_in.dtype)
    cols = [s(0, 1536),
            s(1544, 2056),
            s(2312, 2824),
            s(3336, 3848),
            s(3848, 4360),
            s(4360, 5896),
            s(5896, 6408),
            s(2824, 3336),
            s(2056, 2312),
            s(1536, 1544),
            s(6408, 6416),
            pad]
    return jnp.concatenate(cols, axis=-1).astype(BF16)


def _lane_table(ssd_dt_bias, ssd_a_log, gdn_dt_bias, gdn_a_log):
    def put(v, lane):
        return jnp.pad(v.astype(F32), ((0, 0), (lane, CHUNK - lane - v.shape[1])))
    log_gamma = jnp.log1p(-jnp.exp2(-5.0 - jnp.arange(RET_HEADS, dtype=F32)))
    rows = [put(ssd_dt_bias, LANE_DT), put(ssd_a_log, LANE_DT), put(gdn_dt_bias, LANE_DECAY),
            put(gdn_a_log, LANE_DECAY), put(jnp.broadcast_to(log_gamma, (DEPTH, RET_HEADS)), 0)]
    rows += [jnp.zeros((DEPTH, CHUNK), F32)] * (SUBLANES - len(rows))
    return jnp.stack(rows, axis=1)


def _conv_state_pad(state):
    return jnp.pad(state, ((0, 0), (0, 0), (SUBLANES - (CONV_K - 1), 0), (0, 0)))


def kernel(x_prompt, x_sample, state_ssd, state_ssd_conv, cache_swa_k, cache_swa_v, state_ret, state_gdn,
           state_gdn_conv, meta_tokens, pre_norm, post_norm, w_in, w_out, ssd_conv_w, ssd_conv_b, ssd_dt_bias,
           ssd_a_log, ssd_d, ssd_norm, swa_sinks, ret_norm, gdn_conv_w, gdn_dt_bias, gdn_a_log, gdn_norm):
    meta = jnp.broadcast_to(meta_tokens[None].astype(F32), (BATCH, N_META, D_MODEL))
    xp = jnp.concatenate([jnp.zeros((BATCH, PROMPT_PAD, D_MODEL), F32), meta, x_prompt], axis=1)
    xp = xp.reshape(N_PROMPT, D_MODEL)
    xs = jnp.pad(x_sample, ((0, 0), (0, SUBLANES - DEC_SEQ), (0, 0))).reshape(N_SAMPLE, D_MODEL)
    mask_p = jnp.asarray(np.tile(np.arange(PROMPT_ROWS) >= PROMPT_PAD, BATCH)[:, None], F32)
    mask_s = jnp.asarray(np.tile(np.arange(SUBLANES) < DEC_SEQ, DEC_BATCH)[:, None], F32)

    w_in_p = _permute_w_in(w_in)
    w_out_b = w_out.astype(BF16)
    pre_g = pre_norm.reshape(DEPTH, 1, D_MODEL)
    post_g = post_norm.reshape(DEPTH, 1, D_MODEL)
    prm = dict(
        lanes=_lane_table(ssd_dt_bias, ssd_a_log, gdn_dt_bias, gdn_a_log),
        ssd_conv_w=ssd_conv_w, ssd_conv_b=ssd_conv_b.reshape(DEPTH, 1, SSD_CONV_W),
        ssd_norm=ssd_norm.reshape(DEPTH, 1, GROUP_W), ssd_d=ssd_d, swa_sinks=swa_sinks,
        ret_norm=ret_norm.reshape(DEPTH, 1, GROUP_W), gdn_conv_w=gdn_conv_w,
        gdn_norm=gdn_norm.reshape(DEPTH, 1, GDN_DV))
    ssd_c0 = _conv_state_pad(state_ssd_conv)
    gdn_c0 = _conv_state_pad(state_gdn_conv)
    cache_k = cache_swa_k.reshape(DEPTH, DEC_BATCH, WINDOW, KV_W)
    cache_v = cache_swa_v.reshape(DEPTH, DEC_BATCH, WINDOW, KV_W)

    names = ("ssd", "ssd_conv", "swa_k", "swa_v", "ret", "gdn", "gdn_conv")
    p_out = {k: [] for k in names}
    s_out = {k: [] for k in names}
    tail = slice(PROMPT_ROWS - (CONV_K - 1), PROMPT_ROWS)
    win = slice(PROMPT_ROWS - WINDOW, PROMPT_ROWS)
    for l in range(DEPTH):
        pp = _inproj(xp, pre_g, w_in_p, l, PROMPT_TM)
        ya, p_ssd = _ssd(pp, prm, l, None)
        yb = _swa(pp, prm, l, None)
        yc, p_ret = _ret(pp, prm, l, None)
        yd, p_gdn = _gdn(pp, prm, l, None)
        xp = _outproj((ya, yb, yc, yd), w_out_b, xp, post_g, mask_p, l, PROMPT_TM)

        ps = _inproj(xs, pre_g, w_in_p, l, N_SAMPLE)
        ya, s_ssd = _ssd(ps, prm, l, (state_ssd, ssd_c0))
        yb = _swa(ps, prm, l, (cache_k, cache_v))
        yc, s_ret = _ret(ps, prm, l, (state_ret,))
        yd, s_gdn = _gdn(ps, prm, l, (state_gdn, gdn_c0))
        xs = _outproj((ya, yb, yc, yd), w_out_b, xs, post_g, mask_s, l, N_SAMPLE)

        pp3 = pp.reshape(BATCH, PROMPT_ROWS, IN_W_PAD)
        ps3 = ps.reshape(DEC_BATCH, SUBLANES, IN_W_PAD)
        p_out["ssd"].append(p_ssd)
        p_out["ssd_conv"].append(pp3[:, tail, COL_X:COL_X + SSD_CONV_W])
        p_out["swa_k"].append(pp3[:, win, COL_KV:COL_KV + KV_W])
        p_out["swa_v"].append(pp3[:, win, COL_KV + KV_W:COL_KV + 2 * KV_W])
        p_out["ret"].append(p_ret)
        p_out["gdn"].append(p_gdn)
        p_out["gdn_conv"].append(pp3[:, tail, COL_QD:COL_QD + GDN_CONV_W])
        s_out["ssd"].append(s_ssd)
        s_out["ssd_conv"].append(ps3[:, 1:DEC_SEQ, COL_X:COL_X + SSD_CONV_W])
        s_out["swa_k"].append(ps3[:, :DEC_SEQ, COL_KV:COL_KV + KV_W])
        s_out["swa_v"].append(ps3[:, :DEC_SEQ, COL_KV + KV_W:COL_KV + 2 * KV_W])
        s_out["ret"].append(s_ret)
        s_out["gdn"].append(s_gdn)
        s_out["gdn_conv"].append(ps3[:, 1:DEC_SEQ, COL_QD:COL_QD + GDN_CONV_W])

    p_st = {k: jnp.stack(v) for k, v in p_out.items()}
    s_st = {k: jnp.stack(v) for k, v in s_out.items()}
    kv_shape = (SWA_KV_HEADS, SWA_HEAD_DIM)
    y_prompt = xp.reshape(BATCH, PROMPT_ROWS, D_MODEL)[:, PROMPT_PAD + N_META:]
    y_sample = xs.reshape(DEC_BATCH, SUBLANES, D_MODEL)[:, :DEC_SEQ]
    return (
        y_prompt, y_sample,
        p_st["ssd"], p_st["ssd_conv"],
        p_st["swa_k"].reshape((DEPTH, BATCH, WINDOW) + kv_shape),
        p_st["swa_v"].reshape((DEPTH, BATCH, WINDOW) + kv_shape),
        p_st["ret"], p_st["gdn"], p_st["gdn_conv"],
        s_st["ssd"], s_st["ssd_conv"],
        jnp.concatenate([cache_swa_k[:, :, DEC_SEQ:], s_st["swa_k"].reshape((DEPTH, DEC_BATCH, DEC_SEQ) + kv_shape)], axis=2),
        jnp.concatenate([cache_swa_v[:, :, DEC_SEQ:], s_st["swa_v"].reshape((DEPTH, DEC_BATCH, DEC_SEQ) + kv_shape)], axis=2),
        s_st["ret"], s_st["gdn"], s_st["gdn_conv"],
    )
```

```python
import functools

import jax
import jax.numpy as jnp
import numpy as np
from jax import lax
from jax.experimental import pallas as pl
from jax.experimental.pallas import tpu as pltpu

F32 = jnp.float32
BF16 = jnp.bfloat16
HIGHEST = lax.Precision.HIGHEST

D_MODEL = 2048
BATCH = 4
SEQ = 2048
DEPTH = 4
DEC_BATCH = 32
DEC_SEQ = 4
N_META = 16
GROUP_W = 512
CONV_K = 4
NORM_EPS = 1e-6
WINDOW = 128

SSD_HEADS, SSD_HEAD_DIM, SSD_GROUPS, SSD_STATE = 8, 64, 2, 128
SWA_HEADS, SWA_KV_HEADS, SWA_HEAD_DIM = 8, 2, 64
RET_HEADS, RET_DK, RET_DV = 4, 64, 128
GDN_HEADS, GDN_DK, GDN_DV = 4, 128, 128
SSD_CONV_W = GROUP_W + 2 * SSD_GROUPS * SSD_STATE
GDN_CONV_W = 3 * GROUP_W
KV_W = SWA_KV_HEADS * SWA_HEAD_DIM

CHUNK = 128
SUBLANES = 8
PROMPT_PAD = CHUNK - N_META
PROMPT_ROWS = PROMPT_PAD + N_META + SEQ
PROMPT_CHUNKS = PROMPT_ROWS // CHUNK
N_PROMPT = BATCH * PROMPT_ROWS
N_SAMPLE = DEC_BATCH * SUBLANES
SAMPLE_SEQS = 8

IN_W = 6416
IN_W_PAD = 6528
COL_Z, COL_X, COL_BC, COL_QA, COL_GA, COL_VR, COL_GR = 0, 512, 1024, 1536, 2048, 2560, 3072
COL_QD, COL_KD, COL_VD, COL_GD, COL_QKR, COL_KV, COL_SMALL = 3584, 4096, 4608, 5120, 5632, 6144, 6400
LANE_DT, LANE_BETA, LANE_DECAY = 0, 8, 12
ROW_SSD_DTB, ROW_SSD_ALOG, ROW_GDN_DTB, ROW_GDN_ALOG, ROW_LOG_GAMMA = 0, 1, 2, 3, 4

PROJ_TN = 2176
PROMPT_TM = 512
VMEM_LIMIT = 48 * 1024 * 1024


def _sigmoid(x):
    return 1.0 / (1.0 + jnp.exp(-x))


def _silu(x):
    return x * _sigmoid(x)


def _softplus(x):
    return jnp.maximum(x, 0.0) + jnp.log1p(jnp.exp(-jnp.abs(x)))


def _dot(a, b):
    return jnp.dot(a.astype(BF16), b.astype(BF16), preferred_element_type=F32)


def _dot_nt(a, b):
    return lax.dot_general(a.astype(BF16), b.astype(BF16), (((1,), (1,)), ((), ())),
                           preferred_element_type=F32)


def _dot_tn(a, b):
    return lax.dot_general(a.astype(BF16), b.astype(BF16), (((0,), (0,)), ((), ())),
                           preferred_element_type=F32)


def _dot_f32(a, b):
    return jnp.dot(a, b, precision=HIGHEST, preferred_element_type=F32)


def _pad_rows(a, rows=CHUNK):
    if a.shape[0] == rows:
        return a
    return jnp.concatenate([a, jnp.zeros((rows - a.shape[0], a.shape[1]), a.dtype)], axis=0)


def _col(a, lane, width):
    return jnp.broadcast_to(a[:, lane:lane + 1], (a.shape[0], width))


def _iotas(m):
    row = lax.broadcasted_iota(jnp.int32, (m, CHUNK), 0)
    col = lax.broadcasted_iota(jnp.int32, (m, CHUNK), 1)
    return row, col


def _valid_rows(m, chunk_idx, first_lo, hi):
    r = lax.broadcasted_iota(jnp.int32, (m, 1), 0)
    lo = jnp.where(chunk_idx == 0, first_lo, 0)
    return (r >= lo) & (r < hi)


def _pack_lanes(per_seq):
    stride = CHUNK // len(per_seq)
    out = per_seq[0]
    for b in range(1, len(per_seq)):
        out = out + pltpu.roll(per_seq[b], b * stride, 1)
    return out, stride


def _decay_terms(la):
    m = la.shape[0]
    row, col = _iotas(m)
    lower = (row >= col).astype(F32)
    cum = _dot_f32(lower, _pad_rows(la))
    last = cum[m - 1:m, :]
    return cum, _pad_rows(cum).T, jnp.exp(cum), jnp.exp(last - cum), jnp.exp(last)


def _decay_matrix(cum, cum_t, lane):
    m = cum.shape[0]
    row, col = _iotas(m)
    keep = row >= col
    seg = _col(cum, lane, CHUNK) - jnp.broadcast_to(cum_t[lane:lane + 1, :], (m, CHUNK))
    return jnp.where(keep, jnp.exp(jnp.where(keep, seg, 0.0)), 0.0)


def _causal_conv(ext_ref, cur, w_ref):
    rows = cur.shape[0]
    ext_ref[SUBLANES:SUBLANES + rows, :] = cur
    acc = cur * w_ref[CONV_K - 1:CONV_K, :]
    for j in range(CONV_K - 1):
        start = SUBLANES - (CONV_K - 1) + j
        acc = acc + ext_ref[start:start + rows, :] * w_ref[j:j + 1, :]
    ext_ref[0:SUBLANES, :] = ext_ref[rows:rows + SUBLANES, :]
    return acc


def _rms(x, w):
    ms = jnp.mean(x * x, axis=-1, keepdims=True)
    return x * lax.rsqrt(ms + NORM_EPS) * w


def _prenorm_kernel(x_ref, g_ref, h_ref):
    h_ref[...] = _rms(x_ref[...], g_ref[...]).astype(BF16)


def _prenorm(x, g_all, layer, tm):
    n = x.shape[0]
    return pl.pallas_call(
        _prenorm_kernel,
        grid=(n // tm,),
        in_specs=[pl.BlockSpec((tm, D_MODEL), lambda i: (i, 0)),
                  pl.BlockSpec((None, 1, D_MODEL), lambda i: (layer, 0, 0))],
        out_specs=pl.BlockSpec((tm, D_MODEL), lambda i: (i, 0)),
        out_shape=jax.ShapeDtypeStruct((n, D_MODEL), BF16),
        compiler_params=pltpu.CompilerParams(dimension_semantics=("arbitrary",)),
        name="prenorm",
    )(x, g_all)


def _inproj_kernel(h_ref, w_ref, o_ref):
    o_ref[...] = jnp.dot(h_ref[...], w_ref[...], preferred_element_type=F32)


def _inproj(h, w_all, layer, tm):
    n = h.shape[0]
    return pl.pallas_call(
        _inproj_kernel,
        grid=(IN_W_PAD // PROJ_TN, n // tm),
        in_specs=[
            pl.BlockSpec((tm, D_MODEL), lambda j, i: (i, 0)),
            pl.BlockSpec((None, D_MODEL, PROJ_TN), lambda j, i: (layer, 0, j)),
        ],
        out_specs=pl.BlockSpec((tm, PROJ_TN), lambda j, i: (i, j)),
        out_shape=jax.ShapeDtypeStruct((n, IN_W_PAD), F32),
        compiler_params=pltpu.CompilerParams(
            dimension_semantics=("arbitrary", "arbitrary"), vmem_limit_bytes=VMEM_LIMIT),
        name="inproj",
    )(h, w_all)


def _outproj_kernel(with_next, *refs):
    if with_next:
        ya_ref, yb_ref, yc_ref, yd_ref, w_ref, x_ref, g_ref, m_ref, gn_ref, o_ref, h_ref = refs
    else:
        ya_ref, yb_ref, yc_ref, yd_ref, w_ref, x_ref, g_ref, m_ref, o_ref = refs
    acc = None
    for g, y_ref in enumerate((ya_ref, yb_ref, yc_ref, yd_ref)):
        part = jnp.dot(y_ref[...], w_ref[g * GROUP_W:(g + 1) * GROUP_W, :], preferred_element_type=F32)
        acc = part if acc is None else acc + part
    x_new = jnp.where(m_ref[...] > 0.0, x_ref[...] + _rms(acc, g_ref[...]), 0.0)
    o_ref[...] = x_new
    if with_next:
        h_ref[...] = _rms(x_new, gn_ref[...]).astype(BF16)


def _outproj(ys, w_all, x, post_all, rowmask, pre_all, layer, tm):
    n = x.shape[0]
    with_next = layer + 1 < DEPTH
    yspec = pl.BlockSpec((tm, GROUP_W), lambda i: (i, 0))
    row = pl.BlockSpec((tm, D_MODEL), lambda i: (i, 0))
    in_specs = [yspec, yspec, yspec, yspec,
                pl.BlockSpec((None, D_MODEL, D_MODEL), lambda i: (layer, 0, 0)),
                row,
                pl.BlockSpec((None, 1, D_MODEL), lambda i: (layer, 0, 0)),
                pl.BlockSpec((tm, 1), lambda i: (i, 0))]
    args = list(ys) + [w_all, x, post_all, rowmask]
    out_specs = [row]
    out_shape = [jax.ShapeDtypeStruct((n, D_MODEL), F32)]
    if with_next:
        in_specs.append(pl.BlockSpec((None, 1, D_MODEL), lambda i: (layer + 1, 0, 0)))
        args.append(pre_all)
        out_specs.append(row)
        out_shape.append(jax.ShapeDtypeStruct((n, D_MODEL), BF16))
    res = pl.pallas_call(
        functools.partial(_outproj_kernel, with_next),
        grid=(n // tm,), in_specs=in_specs, out_specs=out_specs, out_shape=out_shape,
        compiler_params=pltpu.CompilerParams(
            dimension_semantics=("arbitrary",), vmem_limit_bytes=VMEM_LIMIT),
        name="outproj",
    )(*args)
    return (res[0], res[1]) if with_next else (res[0], None)


def _init_state(c, has_init, s_scr, s0_ref, ext=None, c0_ref=None):
    @pl.when(c == 0)
    def _():
        if has_init:
            s_scr[...] = s0_ref[...]
            if ext is not None:
                ext[:, 0:SUBLANES, :] = c0_ref[...]
        else:
            s_scr[...] = jnp.zeros_like(s_scr)
            if ext is not None:
                ext[:, 0:SUBLANES, :] = jnp.zeros((ext.shape[0], SUBLANES, ext.shape[2]), F32)


def _ssd_kernel(nseq, rows, layer, has_init, nchunks, first_lo, hi, *refs):
    if has_init:
        (z_ref, x_ref, bc_ref, sm_ref, s0_ref, c0_ref, cw_ref, cb_ref, lp_ref, nw_ref, d_ref,
         y_ref, so_ref, s_scr, ext) = refs
    else:
        (z_ref, x_ref, bc_ref, sm_ref, cw_ref, cb_ref, lp_ref, nw_ref, d_ref,
         y_ref, so_ref, s_scr, ext) = refs
        s0_ref = c0_ref = None
    c = pl.program_id(1)
    m = rows
    _init_state(c, has_init, s_scr, s0_ref, ext, c0_ref)

    valid = _valid_rows(m, c, first_lo, hi)
    lane = lax.broadcasted_iota(jnp.int32, (1, CHUNK), 1)
    head_lane = (lane >= LANE_DT) & (lane < LANE_DT + SSD_HEADS)
    dt_all = _softplus(sm_ref[...] + lp_ref[ROW_SSD_DTB:ROW_SSD_DTB + 1, :])
    la_all = -jnp.exp(lp_ref[ROW_SSD_ALOG:ROW_SSD_ALOG + 1, :]) * dt_all
    dts = [jnp.where(valid, dt_all[b * m:(b + 1) * m], 0.0) for b in range(nseq)]
    la, stride = _pack_lanes([jnp.where(valid & head_lane, la_all[b * m:(b + 1) * m], 0.0) for b in range(nseq)])
    cum, cum_t, e_cum, e_rem, e_last = _decay_terms(la)

    cur = jnp.concatenate([x_ref[...], bc_ref[...]], axis=1)
    conv = jnp.concatenate([_causal_conv(ext.at[b], cur[b * m:(b + 1) * m], cw_ref) for b in range(nseq)], axis=0)
    act = _silu(conv + cb_ref[...])

    hpg = SSD_HEADS // SSD_GROUPS
    c_off = GROUP_W + SSD_GROUPS * SSD_STATE
    scores = {}
    for b in range(nseq):
        a_b = act[b * m:(b + 1) * m]
        for g in range(SSD_GROUPS):
            b_g = a_b[:, GROUP_W + g * SSD_STATE:GROUP_W + (g + 1) * SSD_STATE]
            c_g = a_b[:, c_off + g * SSD_STATE:c_off + (g + 1) * SSD_STATE]
            scores[b, g] = (_dot_nt(c_g, _pad_rows(b_g)), b_g, c_g)
    ys = []
    for b in range(nseq):
        a_b = act[b * m:(b + 1) * m]
        heads = []
        for h in range(SSD_HEADS):
            sc, b_g, c_g = scores[b, h // hpg]
            ln = b * stride + LANE_DT + h
            x_h = a_b[:, h * SSD_HEAD_DIM:(h + 1) * SSD_HEAD_DIM]
            v_h = x_h * _col(dts[b], LANE_DT + h, SSD_HEAD_DIM)
            s_h = s_scr[b, h]
            o = (_dot(sc * _decay_matrix(cum, cum_t, ln), _pad_rows(v_h))
                 + _dot(c_g, s_h) * _col(e_cum, ln, SSD_HEAD_DIM))
            s_scr[b, h] = s_h * e_last[:, ln:ln + 1] + _dot_tn(b_g, v_h * _col(e_rem, ln, SSD_HEAD_DIM))
            heads.append(o + d_ref[layer, h] * x_h)
        ys.append(jnp.concatenate(heads, axis=1))
    y = jnp.concatenate(ys, axis=0) * _silu(z_ref[...])
    gw = GROUP_W // SSD_GROUPS
    outs = [_rms(y[:, g * gw:(g + 1) * gw], nw_ref[:, g * gw:(g + 1) * gw]) for g in range(SSD_GROUPS)]
    y_ref[...] = jnp.concatenate(outs, axis=1).astype(y_ref.dtype)

    @pl.when(c == nchunks - 1)
    def _():
        so_ref[...] = s_scr[...]


def _ret_kernel(nseq, seq_group, rows, has_init, nchunks, first_lo, hi, *refs):
    if has_init:
        qk_ref, v_ref, g_ref, s0_ref, lp_ref, nw_ref, y_ref, so_ref, s_scr = refs
    else:
        qk_ref, v_ref, g_ref, lp_ref, nw_ref, y_ref, so_ref, s_scr = refs
        s0_ref = None
    c = pl.program_id(1)
    m = rows
    _init_state(c, has_init, s_scr, s0_ref)

    valid = _valid_rows(m, c, first_lo, hi)
    lane = lax.broadcasted_iota(jnp.int32, (1, CHUNK), 1)
    log_gamma = jnp.broadcast_to(lp_ref[ROW_LOG_GAMMA:ROW_LOG_GAMMA + 1, :], (m, CHUNK))
    la_one = jnp.where(valid & (lane < RET_HEADS), log_gamma, 0.0)
    la, stride = _pack_lanes([la_one] * nseq)
    cum, cum_t, e_cum, e_rem, e_last = _decay_terms(la)

    qk_all = qk_ref[...]
    v_all = v_ref[...]
    all_units = []
    for b in range(nseq):
        qk = qk_all[b * m:(b + 1) * m]
        v = jnp.where(valid, v_all[b * m:(b + 1) * m], 0.0)
        for h in range(RET_HEADS):
            q_h = qk[:, h * RET_DK:(h + 1) * RET_DK]
            k_h = qk[:, (RET_HEADS + h) * RET_DK:(RET_HEADS + h + 1) * RET_DK] * (RET_DK ** -0.5)
            s_h = s_scr[b, h]
            all_units.append(dict(b=b, h=h, ln=b * stride + h, k=k_h, v=v[:, h * RET_DV:(h + 1) * RET_DV],
                                  s=s_h, raw=_dot_nt(q_h, _pad_rows(k_h)), q_s=_dot(q_h, s_h)))
    ys = []
    for b0 in range(0, nseq, seq_group):
        units = all_units[b0 * RET_HEADS:(b0 + seq_group) * RET_HEADS]
        scs = [u["raw"] * _decay_matrix(cum, cum_t, u["ln"]) for u in units]
        os_ = [_dot(sc, _pad_rows(u["v"])) + u["q_s"] * _col(e_cum, u["ln"], RET_DV)
               for sc, u in zip(scs, units)]
        for u in units:
            ln = u["ln"]
            s_scr[u["b"], u["h"]] = (u["s"] * e_last[:, ln:ln + 1]
                                     + _dot_tn(u["k"] * _col(e_rem, ln, RET_DK), u["v"]))
        for i in range(seq_group):
            heads = [_rms(os_[i * RET_HEADS + h], nw_ref[:, h * RET_DV:(h + 1) * RET_DV])
                     for h in range(RET_HEADS)]
            ys.append(jnp.concatenate(heads, axis=1))
    y_ref[...] = (jnp.concatenate(ys, axis=0) * _silu(g_ref[...])).astype(y_ref.dtype)

    @pl.when(c == nchunks - 1)
    def _():
        so_ref[...] = s_scr[...]


def _unit_lower_inverses(mats, levels):
    m = mats[0].shape[0]
    row, col = _iotas(m)

    def joining_mask(shift):
        same_big = lax.shift_right_logical(row, shift + 1) == lax.shift_right_logical(col, shift + 1)
        same_small = lax.shift_right_logical(row, shift) == lax.shift_right_logical(col, shift)
        return same_big & jnp.logical_not(same_small) & (row > col)

    eye = (row == col).astype(F32)
    mask = joining_mask(0)
    ts = [eye - jnp.where(mask, a, 0.0) for a in mats]
    for shift in range(1, levels):
        mask = joining_mask(shift)
        inner = [_dot(jnp.where(mask, a, 0.0), _pad_rows(t)) for a, t in zip(mats, ts)]
        ts = [t - _dot(t, _pad_rows(x)) for t, x in zip(ts, inner)]
    return ts


def _gdn_kernel(nseq, seq_group, rows, has_init, nchunks, first_lo, hi, *refs):
    if has_init:
        (q_ref, k_ref, v_ref, g_ref, sm_ref, s0_ref, c0_ref, cw_ref, lp_ref, nw_ref,
         y_ref, so_ref, s_scr, ext) = refs
    else:
        (q_ref, k_ref, v_ref, g_ref, sm_ref, cw_ref, lp_ref, nw_ref,
         y_ref, so_ref, s_scr, ext) = refs
        s0_ref = c0_ref = None
    c = pl.program_id(1)
    m = rows
    _init_state(c, has_init, s_scr, s0_ref, ext, c0_ref)

    cur = jnp.concatenate([q_ref[...], k_ref[...], v_ref[...]], axis=1)
    act = _silu(jnp.concatenate(
        [_causal_conv(ext.at[b], cur[b * m:(b + 1) * m], cw_ref) for b in range(nseq)], axis=0))
    valid = _valid_rows(m, c, first_lo, hi)
    lane = lax.broadcasted_iota(jnp.int32, (1, CHUNK), 1)
    head_lane = (lane >= LANE_DECAY) & (lane < LANE_DECAY + GDN_HEADS)
    sm = sm_ref[...]
    beta_all = _sigmoid(sm)
    la_all = (-jnp.exp(lp_ref[ROW_GDN_ALOG:ROW_GDN_ALOG + 1, :])
              * _softplus(sm + lp_ref[ROW_GDN_DTB:ROW_GDN_DTB + 1, :]))
    betas = [jnp.where(valid, beta_all[b * m:(b + 1) * m], 0.0) for b in range(nseq)]
    la, stride = _pack_lanes([jnp.where(valid & head_lane, la_all[b * m:(b + 1) * m], 0.0) for b in range(nseq)])
    cum, cum_t, e_cum, e_rem, e_last = _decay_terms(la)
    row, col = _iotas(m)
    levels = max(1, (min(m, hi) - 1).bit_length())

    ys = [None] * nseq
    for b0 in range(0, nseq, seq_group):
        units = []
        for b in range(b0, b0 + seq_group):
            a_b = act[b * m:(b + 1) * m]
            for h in range(GDN_HEADS):
                ln = b * stride + LANE_DECAY + h
                q_h = a_b[:, h * GDN_DK:(h + 1) * GDN_DK]
                k_h = a_b[:, GROUP_W + h * GDN_DK:GROUP_W + (h + 1) * GDN_DK]
                v_h = a_b[:, 2 * GROUP_W + h * GDN_DV:2 * GROUP_W + (h + 1) * GDN_DV]
                q_h = q_h * lax.rsqrt(jnp.sum(q_h * q_h, axis=-1, keepdims=True) + 1e-6) * (GDN_DK ** -0.5)
                k_h = k_h * lax.rsqrt(jnp.sum(k_h * k_h, axis=-1, keepdims=True) + 1e-6)
                k_h = jnp.where(valid, k_h, 0.0)
                b_h = _col(betas[b], LANE_BETA + h, GDN_DK)
                k_b = k_h * b_h
                decay = _decay_matrix(cum, cum_t, ln)
                k_pad = _pad_rows(k_h)
                s_h = s_scr[b, h]
                units.append(dict(
                    b=b, h=h, s=s_h,
                    a=jnp.where(row > col, _dot_nt(k_b, k_pad) * decay, 0.0),
                    v_b=_pad_rows(jnp.where(valid, v_h, 0.0) * b_h),
                    k_bw=_pad_rows(k_b * _col(e_cum, ln, GDN_DK)),
                    qk=_dot_nt(q_h, k_pad) * decay,
                    q_s=_dot(q_h * _col(e_cum, ln, GDN_DK), s_h),
                    k_w=k_h * _col(e_rem, ln, GDN_DK),
                    s_dec=s_h * e_last[:, ln:ln + 1]))
        ts = _unit_lower_inverses([u["a"] for u in units], levels)
        us = [_dot(t, u["v_b"]) for t, u in zip(ts, units)]
        ws = [_dot(t, u["k_bw"]) for t, u in zip(ts, units)]
        v_news = [x - _dot(w, u["s"]) for x, w, u in zip(us, ws, units)]
        os_ = [u["q_s"] + _dot(u["qk"], _pad_rows(v_new)) for u, v_new in zip(units, v_news)]
        for u, v_new in zip(units, v_news):
            s_scr[u["b"], u["h"]] = u["s_dec"] + _dot_tn(u["k_w"], v_new)
        for i in range(seq_group):
            heads = [_rms(o, nw_ref[...]) for o in os_[i * GDN_HEADS:(i + 1) * GDN_HEADS]]
            ys[b0 + i] = jnp.concatenate(heads, axis=1)
    y_ref[...] = (jnp.concatenate(ys, axis=0) * _silu(g_ref[...])).astype(y_ref.dtype)

    @pl.when(c == nchunks - 1)
    def _():
        so_ref[...] = s_scr[...]


def _swa_kernel(nseq, seq_group, rows, layer, is_prompt, *refs):
    if is_prompt:
        q_ref, g_ref, kvc_ref, kvp_ref, sink_ref, y_ref = refs
    else:
        q_ref, g_ref, kvc_ref, kp_ref, vp_ref, sink_ref, y_ref = refs
    n = pl.program_id(1)
    m = rows
    if is_prompt:
        lo_prev = jnp.where(n == 0, CHUNK, jnp.where(n == 1, PROMPT_PAD, 0))
        lo_cur = jnp.where(n == 0, PROMPT_PAD, 0)
    else:
        lo_prev, lo_cur = 0, 0
    qi = lax.broadcasted_iota(jnp.int32, (m, 2 * CHUNK), 0)
    kj = lax.broadcasted_iota(jnp.int32, (m, 2 * CHUNK), 1)
    dist = qi + WINDOW - kj
    key_ok = ((kj < CHUNK) & (kj >= lo_prev)) | (kj >= CHUNK + lo_cur)
    visible = (dist >= 0) & (dist <= WINDOW) & key_ok
    dist_f = dist.astype(F32)

    q_all = q_ref[...]
    kvc_all = kvc_ref[...]
    grp = SWA_HEADS // SWA_KV_HEADS
    ys = []
    for b0 in range(0, nseq, seq_group):
        units = []
        for b in range(b0, b0 + seq_group):
            q = q_all[b * m:(b + 1) * m]
            kvc = _pad_rows(kvc_all[b * m:(b + 1) * m])
            if is_prompt:
                k_prev = kvp_ref[b * CHUNK:(b + 1) * CHUNK, 0:KV_W]
                v_prev = kvp_ref[b * CHUNK:(b + 1) * CHUNK, KV_W:2 * KV_W]
            else:
                k_prev, v_prev = kp_ref[b], vp_ref[b]
            k_all = jnp.concatenate([k_prev, kvc[:, 0:KV_W]], axis=0)
            v_all = jnp.concatenate([v_prev, kvc[:, KV_W:2 * KV_W]], axis=0)
            for h in range(SWA_HEADS):
                kvh = h // grp
                units.append(dict(h=h, q=q[:, h * SWA_HEAD_DIM:(h + 1) * SWA_HEAD_DIM],
                                  k=k_all[:, kvh * SWA_HEAD_DIM:(kvh + 1) * SWA_HEAD_DIM],
                                  v=v_all[:, kvh * SWA_HEAD_DIM:(kvh + 1) * SWA_HEAD_DIM]))
        ss = [jnp.where(visible,
                        _dot_nt(u["q"], u["k"]) * (SWA_HEAD_DIM ** -0.5)
                        - 2.0 ** (-8.0 * (u["h"] + 1) / SWA_HEADS) * dist_f, -1e30) for u in units]
        mxs = [jnp.maximum(jnp.max(s, axis=-1, keepdims=True), sink_ref[layer, u["h"]])
               for s, u in zip(ss, units)]
        es = [jnp.exp(s - mx) for s, mx in zip(ss, mxs)]
        dens = [jnp.sum(e, axis=-1, keepdims=True) + jnp.exp(sink_ref[layer, u["h"]] - mx)
                for e, mx, u in zip(es, mxs, units)]
        os_ = [_dot(e, u["v"]) / den for e, u, den in zip(es, units, dens)]
        for i in range(seq_group):
            ys.append(jnp.concatenate(os_[i * SWA_HEADS:(i + 1) * SWA_HEADS], axis=1))
    y_ref[...] = (jnp.concatenate(ys, axis=0) * _silu(g_ref[...])).astype(y_ref.dtype)


def _geometry(sample):
    if sample:
        return (DEC_BATCH // SAMPLE_SEQS, 1), SAMPLE_SEQS, SUBLANES, N_SAMPLE, 0, DEC_SEQ
    return (1, PROMPT_CHUNKS), BATCH, CHUNK, N_PROMPT, PROMPT_PAD, CHUNK


def _rowblk(sample):
    return (lambda blk, c: blk) if sample else (lambda blk, c: c)


def _pcol(nrows, rowblk, width, col):
    assert col % width == 0
    return pl.BlockSpec((nrows, width), lambda blk, c: (rowblk(blk, c), col // width))


def _layer_param(shape, layer):
    return pl.BlockSpec((None,) + shape, lambda blk, c: (layer,) + (0,) * len(shape))


def _layer_seqs(nseq, shape, layer):
    return pl.BlockSpec((None, nseq) + shape, lambda blk, c: (layer, blk) + (0,) * len(shape))


def _seqs_out(nseq, shape):
    return pl.BlockSpec((nseq,) + shape, lambda blk, c: (blk,) + (0,) * len(shape))


_SMEM = pl.BlockSpec(memory_space=pltpu.SMEM)
_MIXER_PARAMS = pltpu.CompilerParams(dimension_semantics=("arbitrary", "arbitrary"),
                                     vmem_limit_bytes=VMEM_LIMIT)


def _ssd(p, prm, layer, init):
    sample = init is not None
    grid, nseq, rows, total, first_lo, hi = _geometry(sample)
    rb, n = _rowblk(sample), nseq * rows
    state = (SSD_HEADS, SSD_STATE, SSD_HEAD_DIM)
    in_specs = [_pcol(n, rb, GROUP_W, COL_Z), _pcol(n, rb, GROUP_W, COL_X),
                _pcol(n, rb, GROUP_W, COL_BC), _pcol(n, rb, CHUNK, COL_SMALL)]
    args = [p, p, p, p]
    if sample:
        in_specs += [_layer_seqs(nseq, state, layer), _layer_seqs(nseq, (SUBLANES, SSD_CONV_W), layer)]
        args += list(init)
    in_specs += [_layer_param((CONV_K, SSD_CONV_W), layer), _layer_param((1, SSD_CONV_W), layer),
                 _layer_param((SUBLANES, CHUNK), layer), _layer_param((1, GROUP_W), layer), _SMEM]
    args += [prm["ssd_conv_w"], prm["ssd_conv_b"], prm["lanes"], prm["ssd_norm"], prm["ssd_d"]]
    return pl.pallas_call(
        functools.partial(_ssd_kernel, nseq, rows, layer, sample, grid[1], first_lo, hi),
        grid=grid, in_specs=in_specs,
        out_specs=[pl.BlockSpec((n, GROUP_W), lambda blk, c: (rb(blk, c), 0)), _seqs_out(nseq, state)],
        out_shape=[jax.ShapeDtypeStruct((total, GROUP_W), BF16),
                   jax.ShapeDtypeStruct((grid[0] * nseq,) + state, F32)],
        scratch_shapes=[pltpu.VMEM((nseq,) + state, F32), pltpu.VMEM((nseq, SUBLANES + rows, SSD_CONV_W), F32)],
        compiler_params=_MIXER_PARAMS,
        name="ssd_sample" if sample else "ssd_prompt",
    )(*args)


def _ret(p, prm, layer, init):
    sample = init is not None
    grid, nseq, rows, total, first_lo, hi = _geometry(sample)
    rb, n = _rowblk(sample), nseq * rows
    state = (RET_HEADS, RET_DK, RET_DV)
    in_specs = [_pcol(n, rb, GROUP_W, COL_QKR), _pcol(n, rb, GROUP_W, COL_VR), _pcol(n, rb, GROUP_W, COL_GR)]
    args = [p, p, p]
    if sample:
        in_specs += [_layer_seqs(nseq, state, layer)]
        args += list(init)
    in_specs += [_layer_param((SUBLANES, CHUNK), layer), _layer_param((1, GROUP_W), layer)]
    args += [prm["lanes"], prm["ret_norm"]]
    return pl.pallas_call(
        functools.partial(_ret_kernel, nseq, nseq if sample else 2, rows, sample, grid[1], first_lo, hi),
        grid=grid, in_specs=in_specs,
        out_specs=[pl.BlockSpec((n, GROUP_W), lambda blk, c: (rb(blk, c), 0)), _seqs_out(nseq, state)],
        out_shape=[jax.ShapeDtypeStruct((total, GROUP_W), BF16),
                   jax.ShapeDtypeStruct((grid[0] * nseq,) + state, F32)],
        scratch_shapes=[pltpu.VMEM((nseq,) + state, F32)],
        compiler_params=_MIXER_PARAMS,
        name="ret_sample" if sample else "ret_prompt",
    )(*args)


def _gdn(p, prm, layer, init):
    sample = init is not None
    grid, nseq, rows, total, first_lo, hi = _geometry(sample)
    rb, n = _rowblk(sample), nseq * rows
    state = (GDN_HEADS, GDN_DK, GDN_DV)
    in_specs = [_pcol(n, rb, GROUP_W, COL_QD), _pcol(n, rb, GROUP_W, COL_KD), _pcol(n, rb, GROUP_W, COL_VD),
                _pcol(n, rb, GROUP_W, COL_GD), _pcol(n, rb, CHUNK, COL_SMALL)]
    args = [p, p, p, p, p]
    if sample:
        in_specs += [_layer_seqs(nseq, state, layer), _layer_seqs(nseq, (SUBLANES, GDN_CONV_W), layer)]
        args += list(init)
    in_specs += [_layer_param((CONV_K, GDN_CONV_W), layer), _layer_param((SUBLANES, CHUNK), layer),
                 _layer_param((1, GDN_DV), layer)]
    args += [prm["gdn_conv_w"], prm["lanes"], prm["gdn_norm"]]
    return pl.pallas_call(
        functools.partial(_gdn_kernel, nseq, 2, rows, sample, grid[1], first_lo, hi),
        grid=grid, in_specs=in_specs,
        out_specs=[pl.BlockSpec((n, GROUP_W), lambda blk, c: (rb(blk, c), 0)), _seqs_out(nseq, state)],
        out_shape=[jax.ShapeDtypeStruct((total, GROUP_W), BF16),
                   jax.ShapeDtypeStruct((grid[0] * nseq,) + state, F32)],
        scratch_shapes=[pltpu.VMEM((nseq,) + state, F32), pltpu.VMEM((nseq, SUBLANES + rows, GDN_CONV_W), F32)],
        compiler_params=_MIXER_PARAMS,
        name="gdn_sample" if sample else "gdn_prompt",
    )(*args)


def _swa(p, prm, layer, cache):
    sample = cache is not None
    grid, nseq, rows, total, _, _ = _geometry(sample)
    rb, n = _rowblk(sample), nseq * rows
    in_specs = [_pcol(n, rb, GROUP_W, COL_QA), _pcol(n, rb, GROUP_W, COL_GA), _pcol(n, rb, 2 * KV_W, COL_KV)]
    args = [p, p, p]
    if sample:
        in_specs += [_layer_seqs(nseq, (WINDOW, KV_W), layer), _layer_seqs(nseq, (WINDOW, KV_W), layer)]
        args += list(cache)
    else:
        in_specs += [pl.BlockSpec((n, 2 * KV_W), lambda blk, c: (jnp.maximum(c - 1, 0), COL_KV // (2 * KV_W)))]
        args += [p]
    in_specs += [_SMEM]
    args += [prm["swa_sinks"]]
    return pl.pallas_call(
        functools.partial(_swa_kernel, nseq, nseq if sample else 1, rows, layer, not sample),
        grid=grid, in_specs=in_specs,
        out_specs=pl.BlockSpec((n, GROUP_W), lambda blk, c: (rb(blk, c), 0)),
        out_shape=jax.ShapeDtypeStruct((total, GROUP_W), BF16),
        compiler_params=_MIXER_PARAMS,
        name="swa_sample" if sample else "swa_prompt",
    )(*args)


def _permute_w_in(w_in):
    s = lambda a, b: w_in[:, :, a:b]
    pad = jnp.zeros(w_in.shape[:2] + (IN_W_PAD - IN_W,), w_in.dtype)
    cols = [s(0, 1536),
            s(1544, 2056),
            s(2312, 2824),
            s(3336, 3848),
            s(3848, 4360),
            s(4360, 5896),
            s(5896, 6408),
            s(2824, 3336),
            s(2056, 2312),
            s(1536, 1544),
            s(6408, 6416),
            pad]
    return jnp.concatenate(cols, axis=-1).astype(BF16)


def _lane_table(ssd_dt_bias, ssd_a_log, gdn_dt_bias, gdn_a_log):
    def put(v, lane):
        return jnp.pad(v.astype(F32), ((0, 0), (lane, CHUNK - lane - v.shape[1])))
    log_gamma = jnp.log1p(-jnp.exp2(-5.0 - jnp.arange(RET_HEADS, dtype=F32)))
    rows = [put(ssd_dt_bias, LANE_DT), put(ssd_a_log, LANE_DT), put(gdn_dt_bias, LANE_DECAY),
            put(gdn_a_log, LANE_DECAY), put(jnp.broadcast_to(log_gamma, (DEPTH, RET_HEADS)), 0)]
    rows += [jnp.zeros((DEPTH, CHUNK), F32)] * (SUBLANES - len(rows))
    return jnp.stack(rows, axis=1)


def _conv_state_pad(state):
    return jnp.pad(state, ((0, 0), (0, 0), (SUBLANES - (CONV_K - 1), 0), (0, 0)))


def kernel(x_prompt, x_sample, state_ssd, state_ssd_conv, cache_swa_k, cache_swa_v, state_ret, state_gdn,
           state_gdn_conv, meta_tokens, pre_norm, post_norm, w_in, w_out, ssd_conv_w, ssd_conv_b, ssd_dt_bias,
           ssd_a_log, ssd_d, ssd_norm, swa_sinks, ret_norm, gdn_conv_w, gdn_dt_bias, gdn_a_log, gdn_norm):
    meta = jnp.broadcast_to(meta_tokens[None].astype(F32), (BATCH, N_META, D_MODEL))
    xp = jnp.concatenate([jnp.zeros((BATCH, PROMPT_PAD, D_MODEL), F32), meta, x_prompt], axis=1)
    xp = xp.reshape(BATCH, PROMPT_CHUNKS, CHUNK, D_MODEL).transpose(1, 0, 2, 3).reshape(N_PROMPT, D_MODEL)
    xs = jnp.pad(x_sample, ((0, 0), (0, SUBLANES - DEC_SEQ), (0, 0))).reshape(N_SAMPLE, D_MODEL)
    mask_np = np.ones((PROMPT_CHUNKS, BATCH, CHUNK), np.float32)
    mask_np[0, :, :PROMPT_PAD] = 0.0
    mask_p = jnp.asarray(mask_np.reshape(N_PROMPT, 1))
    mask_s = jnp.asarray(np.tile(np.arange(SUBLANES) < DEC_SEQ, DEC_BATCH)[:, None], F32)

    w_in_p = _permute_w_in(w_in)
    w_out_b = w_out.astype(BF16)
    pre_g = pre_norm.reshape(DEPTH, 1, D_MODEL)
    post_g = post_norm.reshape(DEPTH, 1, D_MODEL)
    prm = dict(
        lanes=_lane_table(ssd_dt_bias, ssd_a_log, gdn_dt_bias, gdn_a_log),
        ssd_conv_w=ssd_conv_w, ssd_conv_b=ssd_conv_b.reshape(DEPTH, 1, SSD_CONV_W),
        ssd_norm=ssd_norm.reshape(DEPTH, 1, GROUP_W), ssd_d=ssd_d, swa_sinks=swa_sinks,
        ret_norm=ret_norm.reshape(DEPTH, 1, GROUP_W), gdn_conv_w=gdn_conv_w,
        gdn_norm=gdn_norm.reshape(DEPTH, 1, GDN_DV))
    ssd_c0 = _conv_state_pad(state_ssd_conv)
    gdn_c0 = _conv_state_pad(state_gdn_conv)
    cache_k = cache_swa_k.reshape(DEPTH, DEC_BATCH, WINDOW, KV_W)
    cache_v = cache_swa_v.reshape(DEPTH, DEC_BATCH, WINDOW, KV_W)

    names = ("ssd", "ssd_conv", "swa_k", "swa_v", "ret", "gdn", "gdn_conv")
    p_out = {k: [] for k in names}
    s_out = {k: [] for k in names}
    tail = slice(CHUNK - (CONV_K - 1), CHUNK)
    hp = _prenorm(xp, pre_g, 0, PROMPT_TM)
    hs = _prenorm(xs, pre_g, 0, N_SAMPLE)
    for l in range(DEPTH):
        pp = _inproj(hp, w_in_p, l, PROMPT_TM)
        ya, p_ssd = _ssd(pp, prm, l, None)
        yb = _swa(pp, prm, l, None)
        yc, p_ret = _ret(pp, prm, l, None)
        yd, p_gdn = _gdn(pp, prm, l, None)
        xp, hp = _outproj((ya, yb, yc, yd), w_out_b, xp, post_g, mask_p, pre_g, l, PROMPT_TM)

        ps = _inproj(hs, w_in_p, l, N_SAMPLE)
        ya, s_ssd = _ssd(ps, prm, l, (state_ssd, ssd_c0))
        yb = _swa(ps, prm, l, (cache_k, cache_v))
        yc, s_ret = _ret(ps, prm, l, (state_ret,))
        yd, s_gdn = _gdn(ps, prm, l, (state_gdn, gdn_c0))
        xs, hs = _outproj((ya, yb, yc, yd), w_out_b, xs, post_g, mask_s, pre_g, l, N_SAMPLE)

        last = pp[N_PROMPT - BATCH * CHUNK:].reshape(BATCH, CHUNK, IN_W_PAD)
        ps3 = ps.reshape(DEC_BATCH, SUBLANES, IN_W_PAD)
        p_out["ssd"].append(p_ssd)
        p_out["ssd_conv"].append(last[:, tail, COL_X:COL_X + SSD_CONV_W])
        p_out["swa_k"].append(last[:, :, COL_KV:COL_KV + KV_W])
        p_out["swa_v"].append(last[:, :, COL_KV + KV_W:COL_KV + 2 * KV_W])
        p_out["ret"].append(p_ret)
        p_out["gdn"].append(p_gdn)
        p_out["gdn_conv"].append(last[:, tail, COL_QD:COL_QD + GDN_CONV_W])
        s_out["ssd"].append(s_ssd)
        s_out["ssd_conv"].append(ps3[:, 1:DEC_SEQ, COL_X:COL_X + SSD_CONV_W])
        s_out["swa_k"].append(ps3[:, :DEC_SEQ, COL_KV:COL_KV + KV_W])
        s_out["swa_v"].append(ps3[:, :DEC_SEQ, COL_KV + KV_W:COL_KV + 2 * KV_W])
        s_out["ret"].append(s_ret)
        s_out["gdn"].append(s_gdn)
        s_out["gdn_conv"].append(ps3[:, 1:DEC_SEQ, COL_QD:COL_QD + GDN_CONV_W])

    p_st = {k: jnp.stack(v) for k, v in p_out.items()}
    s_st = {k: jnp.stack(v) for k, v in s_out.items()}
    kv_shape = (SWA_KV_HEADS, SWA_HEAD_DIM)
    y_prompt = xp.reshape(PROMPT_CHUNKS, BATCH, CHUNK, D_MODEL)[1:].transpose(1, 0, 2, 3).reshape(BATCH, SEQ, D_MODEL)
    y_sample = xs.reshape(DEC_BATCH, SUBLANES, D_MODEL)[:, :DEC_SEQ]
    return (
        y_prompt, y_sample,
        p_st["ssd"], p_st["ssd_conv"],
        p_st["swa_k"].reshape((DEPTH, BATCH, WINDOW) + kv_shape),
        p_st["swa_v"].reshape((DEPTH, BATCH, WINDOW) + kv_shape),
        p_st["ret"], p_st["gdn"], p_st["gdn_conv"],
        s_st["ssd"], s_st["ssd_conv"],
        jnp.concatenate([cache_swa_k[:, :, DEC_SEQ:], s_st["swa_k"].reshape((DEPTH, DEC_BATCH, DEC_SEQ) + kv_shape)], axis=2),
        jnp.concatenate([cache_swa_v[:, :, DEC_SEQ:], s_st["swa_v"].reshape((DEPTH, DEC_BATCH, DEC_SEQ) + kv_shape)], axis=2),
        s_st["ret"], s_st["gdn"], s_st["gdn_conv"],
    )
```

```python
import functools

import jax
import jax.numpy as jnp
import numpy as np
from jax import lax
from jax.experimental import pallas as pl
from jax.experimental.pallas import tpu as pltpu

F32 = jnp.float32
BF16 = jnp.bfloat16
HIGHEST = lax.Precision.HIGHEST

D_MODEL = 2048
BATCH = 4
SEQ = 2048
DEPTH = 4
DEC_BATCH = 32
DEC_SEQ = 4
N_META = 16
GROUP_W = 512
CONV_K = 4
NORM_EPS = 1e-6
WINDOW = 128

SSD_HEADS, SSD_HEAD_DIM, SSD_GROUPS, SSD_STATE = 8, 64, 2, 128
SWA_HEADS, SWA_KV_HEADS, SWA_HEAD_DIM = 8, 2, 64
RET_HEADS, RET_DK, RET_DV = 4, 64, 128
GDN_HEADS, GDN_DK, GDN_DV = 4, 128, 128
SSD_CONV_W = GROUP_W + 2 * SSD_GROUPS * SSD_STATE
GDN_CONV_W = 3 * GROUP_W
KV_W = SWA_KV_HEADS * SWA_HEAD_DIM

CHUNK = 128
SUBLANES = 8
PROMPT_PAD = CHUNK - N_META
PROMPT_ROWS = PROMPT_PAD + N_META + SEQ
PROMPT_CHUNKS = PROMPT_ROWS // CHUNK
N_PROMPT = BATCH * PROMPT_ROWS
N_SAMPLE = DEC_BATCH * SUBLANES
SAMPLE_SEQS = 8

IN_W = 6416
IN_W_PAD = 6528
COL_Z, COL_X, COL_BC, COL_QA, COL_GA, COL_VR, COL_GR = 0, 512, 1024, 1536, 2048, 2560, 3072
COL_QD, COL_KD, COL_VD, COL_GD, COL_QKR, COL_KV, COL_SMALL = 3584, 4096, 4608, 5120, 5632, 6144, 6400
LANE_DT, LANE_BETA, LANE_DECAY = 0, 8, 12
ROW_SSD_DTB, ROW_SSD_ALOG, ROW_GDN_DTB, ROW_GDN_ALOG, ROW_LOG_GAMMA = 0, 1, 2, 3, 4

W_SEGMENTS = ((0, 1536), (1544, 2056), (2312, 2824), (3336, 3848), (3848, 4360), (4360, 5896), (5896, 6408),
              (2824, 3336), (2056, 2312))
W_BLOCKS = IN_W_PAD // CHUNK

PROJ_TN = 2176
PROMPT_TM = 512
VMEM_LIMIT = 48 * 1024 * 1024


def _sigmoid(x):
    return 1.0 / (1.0 + jnp.exp(-x))


def _silu(x):
    return x * _sigmoid(x)


def _softplus(x):
    return jnp.maximum(x, 0.0) + jnp.log1p(jnp.exp(-jnp.abs(x)))


def _dot(a, b):
    return jnp.dot(a.astype(BF16), b.astype(BF16), preferred_element_type=F32)


def _dot_nt(a, b):
    return lax.dot_general(a.astype(BF16), b.astype(BF16), (((1,), (1,)), ((), ())),
                           preferred_element_type=F32)


def _dot_tn(a, b):
    return lax.dot_general(a.astype(BF16), b.astype(BF16), (((0,), (0,)), ((), ())),
                           preferred_element_type=F32)


def _dot_f32(a, b):
    return jnp.dot(a, b, precision=HIGHEST, preferred_element_type=F32)


def _pad_rows(a, rows=CHUNK):
    if a.shape[0] == rows:
        return a
    return jnp.concatenate([a, jnp.zeros((rows - a.shape[0], a.shape[1]), a.dtype)], axis=0)


def _col(a, lane, width):
    return jnp.broadcast_to(a[:, lane:lane + 1], (a.shape[0], width))


def _iotas(m):
    row = lax.broadcasted_iota(jnp.int32, (m, CHUNK), 0)
    col = lax.broadcasted_iota(jnp.int32, (m, CHUNK), 1)
    return row, col


def _valid_rows(m, chunk_idx, first_lo, hi):
    r = lax.broadcasted_iota(jnp.int32, (m, 1), 0)
    lo = jnp.where(chunk_idx == 0, first_lo, 0)
    return (r >= lo) & (r < hi)


def _pack_lanes(per_seq):
    stride = CHUNK // len(per_seq)
    out = per_seq[0]
    for b in range(1, len(per_seq)):
        out = out + pltpu.roll(per_seq[b], b * stride, 1)
    return out, stride


def _decay_terms(la):
    m = la.shape[0]
    row, col = _iotas(m)
    lower = (row >= col).astype(F32)
    cum = _dot_f32(lower, _pad_rows(la))
    last = cum[m - 1:m, :]
    return cum, _pad_rows(cum).T, jnp.exp(cum), jnp.exp(last - cum), jnp.exp(last)


def _decay_matrix(cum, cum_t, lane):
    m = cum.shape[0]
    row, col = _iotas(m)
    keep = row >= col
    seg = _col(cum, lane, CHUNK) - jnp.broadcast_to(cum_t[lane:lane + 1, :], (m, CHUNK))
    return jnp.where(keep, jnp.exp(jnp.where(keep, seg, 0.0)), 0.0)


def _shift_rows_matrix(rows):
    r = lax.broadcasted_iota(jnp.int32, (rows, rows), 0)
    c = lax.broadcasted_iota(jnp.int32, (rows, rows), 1)
    blocks = [jnp.where(r - c == CONV_K - 1 - j, 1.0, 0.0) for j in range(CONV_K - 1)]
    return jnp.concatenate(blocks, axis=0).astype(BF16)


def _causal_conv(ext_ref, cur, w_ref, shift_mat):
    rows = cur.shape[0]
    w_last = w_ref[CONV_K - 1:CONV_K, :]
    ext_ref[SUBLANES:2 * SUBLANES, :] = cur[0:SUBLANES]
    head = cur[0:SUBLANES] * w_last
    for j in range(CONV_K - 1):
        start = SUBLANES - (CONV_K - 1) + j
        head = head + ext_ref[start:start + SUBLANES, :] * w_ref[j:j + 1, :]
    ext_ref[0:SUBLANES, :] = cur[rows - SUBLANES:rows]
    if rows == SUBLANES:
        return head
    shifted = jnp.dot(shift_mat, cur.astype(BF16), preferred_element_type=F32)
    acc = cur * w_last
    for j in range(CONV_K - 1):
        acc = acc + shifted[j * rows:(j + 1) * rows] * w_ref[j:j + 1, :]
    return jnp.concatenate([head, acc[SUBLANES:]], axis=0)


def _rms(x, w):
    ms = jnp.mean(x * x, axis=-1, keepdims=True)
    return x * lax.rsqrt(ms + NORM_EPS) * w


def _prenorm_kernel(x_ref, g_ref, h_ref):
    h_ref[...] = _rms(x_ref[...], g_ref[...]).astype(BF16)


def _prenorm(x, g_all, layer, tm):
    n = x.shape[0]
    return pl.pallas_call(
        _prenorm_kernel,
        grid=(n // tm,),
        in_specs=[pl.BlockSpec((tm, D_MODEL), lambda i: (i, 0)),
                  pl.BlockSpec((None, 1, D_MODEL), lambda i: (layer, 0, 0))],
        out_specs=pl.BlockSpec((tm, D_MODEL), lambda i: (i, 0)),
        out_shape=jax.ShapeDtypeStruct((n, D_MODEL), BF16),
        compiler_params=pltpu.CompilerParams(dimension_semantics=("arbitrary",)),
        name="prenorm",
    )(x, g_all)


def _wprep_kernel(offs_ref, a_ref, b_ref, o_ref):
    j = pl.program_id(1)

    @pl.when(j < W_BLOCKS - 1)
    def _():
        o_ref[...] = a_ref[...].astype(BF16)

    @pl.when(j == W_BLOCKS - 1)
    def _():
        small = jnp.concatenate([a_ref[0:SUBLANES, :], b_ref[...],
                                 jnp.zeros((CHUNK - 2 * SUBLANES, D_MODEL), F32)], axis=0)
        o_ref[...] = small.astype(BF16)


def _wprep(w_t):
    offs = np.concatenate([np.arange(a, b, CHUNK) for a, b in W_SEGMENTS] + [[1536]]).astype(np.int32)
    assert offs.shape[0] == W_BLOCKS and not (offs % SUBLANES).any()
    offs = offs // SUBLANES
    grid_spec = pltpu.PrefetchScalarGridSpec(
        num_scalar_prefetch=1,
        grid=(DEPTH, W_BLOCKS),
        in_specs=[pl.BlockSpec((pl.Element(CHUNK), pl.Element(D_MODEL)),
                               lambda l, j, offs: ((l * (IN_W // SUBLANES) + offs[j]) * SUBLANES, 0)),
                  pl.BlockSpec((SUBLANES, D_MODEL), lambda l, j, offs: ((l * IN_W + 6408) // SUBLANES, 0))],
        out_specs=pl.BlockSpec((None, CHUNK, D_MODEL), lambda l, j, offs: (l, j, 0)),
    )
    w_rows = w_t.reshape(DEPTH * IN_W, D_MODEL)
    return pl.pallas_call(
        _wprep_kernel, grid_spec=grid_spec,
        out_shape=jax.ShapeDtypeStruct((DEPTH, IN_W_PAD, D_MODEL), BF16),
        compiler_params=pltpu.CompilerParams(dimension_semantics=("arbitrary", "arbitrary")),
        name="wprep",
    )(jnp.asarray(offs), w_rows, w_rows)


def _inproj_kernel(h_ref, w_ref, o_ref):
    o_ref[...] = lax.dot_general(h_ref[...], w_ref[...], (((1,), (1,)), ((), ())),
                                 preferred_element_type=F32)


def _inproj(h, w_all, layer, tm):
    n = h.shape[0]
    return pl.pallas_call(
        _inproj_kernel,
        grid=(IN_W_PAD // PROJ_TN, n // tm),
        in_specs=[
            pl.BlockSpec((tm, D_MODEL), lambda j, i: (i, 0)),
            pl.BlockSpec((None, PROJ_TN, D_MODEL), lambda j, i: (layer, j, 0)),
        ],
        out_specs=pl.BlockSpec((tm, PROJ_TN), lambda j, i: (i, j)),
        out_shape=jax.ShapeDtypeStruct((n, IN_W_PAD), F32),
        compiler_params=pltpu.CompilerParams(
            dimension_semantics=("arbitrary", "arbitrary"), vmem_limit_bytes=VMEM_LIMIT),
        name="inproj",
    )(h, w_all)


def _embed_prompt_kernel(x_ref, meta_ref, g_ref, o_ref, h_ref):
    c = pl.program_id(0)

    @pl.when(c == 0)
    def _():
        head = jnp.concatenate([jnp.zeros((PROMPT_PAD, D_MODEL), F32), meta_ref[...]], axis=0)
        for b in range(BATCH):
            o_ref[b * CHUNK:(b + 1) * CHUNK, :] = head

    @pl.when(c > 0)
    def _():
        for b in range(BATCH):
            o_ref[b * CHUNK:(b + 1) * CHUNK, :] = x_ref[b]

    h_ref[...] = _rms(o_ref[...], g_ref[...]).astype(BF16)


def _embed_prompt(x_prompt, meta_tokens, g_all):
    x4 = x_prompt.reshape(BATCH, SEQ // CHUNK, CHUNK, D_MODEL)
    row = pl.BlockSpec((BATCH * CHUNK, D_MODEL), lambda c: (c, 0))
    return pl.pallas_call(
        _embed_prompt_kernel,
        grid=(PROMPT_CHUNKS,),
        in_specs=[pl.BlockSpec((BATCH, None, CHUNK, D_MODEL), lambda c: (0, jnp.maximum(c - 1, 0), 0, 0)),
                  pl.BlockSpec((N_META, D_MODEL), lambda c: (0, 0)),
                  pl.BlockSpec((None, 1, D_MODEL), lambda c: (0, 0, 0))],
        out_specs=[row, row],
        out_shape=[jax.ShapeDtypeStruct((N_PROMPT, D_MODEL), F32),
                   jax.ShapeDtypeStruct((N_PROMPT, D_MODEL), BF16)],
        compiler_params=pltpu.CompilerParams(dimension_semantics=("arbitrary",), vmem_limit_bytes=VMEM_LIMIT),
        name="embed_prompt",
    )(x4, meta_tokens, g_all)


def _outproj_kernel(with_next, by_sequence, *refs):
    if with_next:
        ya_ref, yb_ref, yc_ref, yd_ref, w_ref, x_ref, g_ref, m_ref, gn_ref, o_ref, h_ref = refs
    else:
        ya_ref, yb_ref, yc_ref, yd_ref, w_ref, x_ref, g_ref, m_ref, o_ref = refs
    acc = None
    for g, y_ref in enumerate((ya_ref, yb_ref, yc_ref, yd_ref)):
        part = jnp.dot(y_ref[...], w_ref[g * GROUP_W:(g + 1) * GROUP_W, :], preferred_element_type=F32)
        acc = part if acc is None else acc + part
    x_new = jnp.where(m_ref[...] > 0.0, x_ref[...] + _rms(acc, g_ref[...]), 0.0)
    if by_sequence:
        for b in range(BATCH):
            o_ref[b] = x_new[b * CHUNK:(b + 1) * CHUNK]
    else:
        o_ref[...] = x_new
    if with_next:
        h_ref[...] = _rms(x_new, gn_ref[...]).astype(BF16)


def _outproj(ys, w_all, x, post_all, rowmask, pre_all, layer, tm, by_sequence=False):
    n = x.shape[0]
    with_next = layer + 1 < DEPTH
    yspec = pl.BlockSpec((tm, GROUP_W), lambda i: (i, 0))
    row = pl.BlockSpec((tm, D_MODEL), lambda i: (i, 0))
    in_specs = [yspec, yspec, yspec, yspec,
                pl.BlockSpec((None, D_MODEL, D_MODEL), lambda i: (layer, 0, 0)),
                row,
                pl.BlockSpec((None, 1, D_MODEL), lambda i: (layer, 0, 0)),
                pl.BlockSpec((tm, 1), lambda i: (i, 0))]
    args = list(ys) + [w_all, x, post_all, rowmask]
    if by_sequence:
        assert tm == BATCH * CHUNK and not with_next
        out_specs = [pl.BlockSpec((BATCH, None, CHUNK, D_MODEL), lambda i: (0, jnp.maximum(i - 1, 0), 0, 0))]
        out_shape = [jax.ShapeDtypeStruct((BATCH, SEQ // CHUNK, CHUNK, D_MODEL), F32)]
    else:
        out_specs = [row]
        out_shape = [jax.ShapeDtypeStruct((n, D_MODEL), F32)]
    if with_next:
        in_specs.append(pl.BlockSpec((None, 1, D_MODEL), lambda i: (layer + 1, 0, 0)))
        args.append(pre_all)
        out_specs.append(row)
        out_shape.append(jax.ShapeDtypeStruct((n, D_MODEL), BF16))
    res = pl.pallas_call(
        functools.partial(_outproj_kernel, with_next, by_sequence),
        grid=(n // tm,), in_specs=in_specs, out_specs=out_specs, out_shape=out_shape,
        compiler_params=pltpu.CompilerParams(
            dimension_semantics=("arbitrary",), vmem_limit_bytes=VMEM_LIMIT),
        name="outproj",
    )(*args)
    return (res[0], res[1]) if with_next else (res[0], None)


def _init_state(c, has_init, s_scr, s0_ref, ext=None, c0_ref=None):
    @pl.when(c == 0)
    def _():
        if has_init:
            s_scr[...] = s0_ref[...]
            if ext is not None:
                ext[:, 0:SUBLANES, :] = c0_ref[...]
        else:
            s_scr[...] = jnp.zeros_like(s_scr)
            if ext is not None:
                ext[:, 0:SUBLANES, :] = jnp.zeros((ext.shape[0], SUBLANES, ext.shape[2]), F32)


def _ssd_kernel(nseq, rows, layer, has_init, nchunks, first_lo, hi, *refs):
    if has_init:
        (z_ref, x_ref, bc_ref, sm_ref, s0_ref, c0_ref, cw_ref, cb_ref, lp_ref, nw_ref, d_ref,
         y_ref, so_ref, s_scr, ext) = refs
    else:
        (z_ref, x_ref, bc_ref, sm_ref, cw_ref, cb_ref, lp_ref, nw_ref, d_ref,
         y_ref, so_ref, s_scr, ext) = refs
        s0_ref = c0_ref = None
    c = pl.program_id(1)
    m = rows
    _init_state(c, has_init, s_scr, s0_ref, ext, c0_ref)

    valid = _valid_rows(m, c, first_lo, hi)
    lane = lax.broadcasted_iota(jnp.int32, (1, CHUNK), 1)
    head_lane = (lane >= LANE_DT) & (lane < LANE_DT + SSD_HEADS)
    dt_all = _softplus(sm_ref[...] + lp_ref[ROW_SSD_DTB:ROW_SSD_DTB + 1, :])
    la_all = -jnp.exp(lp_ref[ROW_SSD_ALOG:ROW_SSD_ALOG + 1, :]) * dt_all
    dts = [jnp.where(valid, dt_all[b * m:(b + 1) * m], 0.0) for b in range(nseq)]
    la, stride = _pack_lanes([jnp.where(valid & head_lane, la_all[b * m:(b + 1) * m], 0.0) for b in range(nseq)])
    cum, cum_t, e_cum, e_rem, e_last = _decay_terms(la)

    cur = jnp.concatenate([x_ref[...], bc_ref[...]], axis=1)
    shift_mat = _shift_rows_matrix(m) if m > SUBLANES else None
    conv = jnp.concatenate(
        [_causal_conv(ext.at[b], cur[b * m:(b + 1) * m], cw_ref, shift_mat) for b in range(nseq)], axis=0)
    act = _silu(conv + cb_ref[...])

    hpg = SSD_HEADS // SSD_GROUPS
    c_off = GROUP_W + SSD_GROUPS * SSD_STATE
    scores = {}
    for b in range(nseq):
        a_b = act[b * m:(b + 1) * m]
        for g in range(SSD_GROUPS):
            b_g = a_b[:, GROUP_W + g * SSD_STATE:GROUP_W + (g + 1) * SSD_STATE]
            c_g = a_b[:, c_off + g * SSD_STATE:c_off + (g + 1) * SSD_STATE]
            scores[b, g] = (_dot_nt(c_g, _pad_rows(b_g)), b_g, c_g)
    ys = []
    for b in range(nseq):
        a_b = act[b * m:(b + 1) * m]
        heads = []
        for h in range(SSD_HEADS):
            sc, b_g, c_g = scores[b, h // hpg]
            ln = b * stride + LANE_DT + h
            x_h = a_b[:, h * SSD_HEAD_DIM:(h + 1) * SSD_HEAD_DIM]
            v_h = x_h * _col(dts[b], LANE_DT + h, SSD_HEAD_DIM)
            s_h = s_scr[b, h]
            o = (_dot(sc * _decay_matrix(cum, cum_t, ln), _pad_rows(v_h))
                 + _dot(c_g, s_h) * _col(e_cum, ln, SSD_HEAD_DIM))
            s_scr[b, h] = s_h * e_last[:, ln:ln + 1] + _dot_tn(b_g, v_h * _col(e_rem, ln, SSD_HEAD_DIM))
            heads.append(o + d_ref[layer, h] * x_h)
        ys.append(jnp.concatenate(heads, axis=1))
    y = jnp.concatenate(ys, axis=0) * _silu(z_ref[...])
    gw = GROUP_W // SSD_GROUPS
    outs = [_rms(y[:, g * gw:(g + 1) * gw], nw_ref[:, g * gw:(g + 1) * gw]) for g in range(SSD_GROUPS)]
    y_ref[...] = jnp.concatenate(outs, axis=1).astype(y_ref.dtype)

    @pl.when(c == nchunks - 1)
    def _():
        so_ref[...] = s_scr[...]


def _ret_kernel(nseq, seq_group, rows, has_init, nchunks, first_lo, hi, *refs):
    if has_init:
        qk_ref, v_ref, g_ref, s0_ref, lp_ref, nw_ref, y_ref, so_ref, s_scr = refs
    else:
        qk_ref, v_ref, g_ref, lp_ref, nw_ref, y_ref, so_ref, s_scr = refs
        s0_ref = None
    c = pl.program_id(1)
    m = rows
    _init_state(c, has_init, s_scr, s0_ref)

    valid = _valid_rows(m, c, first_lo, hi)
    lane = lax.broadcasted_iota(jnp.int32, (1, CHUNK), 1)
    log_gamma = jnp.broadcast_to(lp_ref[ROW_LOG_GAMMA:ROW_LOG_GAMMA + 1, :], (m, CHUNK))
    la_one = jnp.where(valid & (lane < RET_HEADS), log_gamma, 0.0)
    la, stride = _pack_lanes([la_one] * nseq)
    cum, cum_t, e_cum, e_rem, e_last = _decay_terms(la)

    qk_all = qk_ref[...]
    v_all = v_ref[...]
    all_units = []
    for b in range(nseq):
        qk = qk_all[b * m:(b + 1) * m]
        v = jnp.where(valid, v_all[b * m:(b + 1) * m], 0.0)
        for h in range(RET_HEADS):
            q_h = qk[:, h * RET_DK:(h + 1) * RET_DK]
            k_h = qk[:, (RET_HEADS + h) * RET_DK:(RET_HEADS + h + 1) * RET_DK] * (RET_DK ** -0.5)
            s_h = s_scr[b, h]
            all_units.append(dict(b=b, h=h, ln=b * stride + h, k=k_h, v=v[:, h * RET_DV:(h + 1) * RET_DV],
                                  s=s_h, raw=_dot_nt(q_h, _pad_rows(k_h)), q_s=_dot(q_h, s_h)))
    ys = []
    for b0 in range(0, nseq, seq_group):
        units = all_units[b0 * RET_HEADS:(b0 + seq_group) * RET_HEADS]
        scs = [u["raw"] * _decay_matrix(cum, cum_t, u["ln"]) for u in units]
        os_ = [_dot(sc, _pad_rows(u["v"])) + u["q_s"] * _col(e_cum, u["ln"], RET_DV)
               for sc, u in zip(scs, units)]
        for u in units:
            ln = u["ln"]
            s_scr[u["b"], u["h"]] = (u["s"] * e_last[:, ln:ln + 1]
                                     + _dot_tn(u["k"] * _col(e_rem, ln, RET_DK), u["v"]))
        for i in range(seq_group):
            heads = [_rms(os_[i * RET_HEADS + h], nw_ref[:, h * RET_DV:(h + 1) * RET_DV])
                     for h in range(RET_HEADS)]
            ys.append(jnp.concatenate(heads, axis=1))
    y_ref[...] = (jnp.concatenate(ys, axis=0) * _silu(g_ref[...])).astype(y_ref.dtype)

    @pl.when(c == nchunks - 1)
    def _():
        so_ref[...] = s_scr[...]


def _unit_lower_inverses(mats, levels):
    m = mats[0].shape[0]
    row, col = _iotas(m)

    def joining_mask(shift):
        same_big = lax.shift_right_logical(row, shift + 1) == lax.shift_right_logical(col, shift + 1)
        same_small = lax.shift_right_logical(row, shift) == lax.shift_right_logical(col, shift)
        return same_big & jnp.logical_not(same_small) & (row > col)

    eye = (row == col).astype(F32)
    mask = joining_mask(0)
    ts = [eye - jnp.where(mask, a, 0.0) for a in mats]
    for shift in range(1, levels):
        mask = joining_mask(shift)
        inner = [_dot(jnp.where(mask, a, 0.0), _pad_rows(t)) for a, t in zip(mats, ts)]
        ts = [t - _dot(t, _pad_rows(x)) for t, x in zip(ts, inner)]
    return ts


def _gdn_kernel(nseq, seq_group, rows, has_init, nchunks, first_lo, hi, *refs):
    if has_init:
        (q_ref, k_ref, v_ref, g_ref, sm_ref, s0_ref, c0_ref, cw_ref, lp_ref, nw_ref,
         y_ref, so_ref, s_scr, ext) = refs
    else:
        (q_ref, k_ref, v_ref, g_ref, sm_ref, cw_ref, lp_ref, nw_ref,
         y_ref, so_ref, s_scr, ext) = refs
        s0_ref = c0_ref = None
    c = pl.program_id(1)
    m = rows
    _init_state(c, has_init, s_scr, s0_ref, ext, c0_ref)

    cur = jnp.concatenate([q_ref[...], k_ref[...], v_ref[...]], axis=1)
    shift_mat = _shift_rows_matrix(m) if m > SUBLANES else None
    act = _silu(jnp.concatenate(
        [_causal_conv(ext.at[b], cur[b * m:(b + 1) * m], cw_ref, shift_mat) for b in range(nseq)], axis=0))
    valid = _valid_rows(m, c, first_lo, hi)
    lane = lax.broadcasted_iota(jnp.int32, (1, CHUNK), 1)
    head_lane = (lane >= LANE_DECAY) & (lane < LANE_DECAY + GDN_HEADS)
    sm = sm_ref[...]
    beta_all = _sigmoid(sm)
    la_all = (-jnp.exp(lp_ref[ROW_GDN_ALOG:ROW_GDN_ALOG + 1, :])
              * _softplus(sm + lp_ref[ROW_GDN_DTB:ROW_GDN_DTB + 1, :]))
    betas = [jnp.where(valid, beta_all[b * m:(b + 1) * m], 0.0) for b in range(nseq)]
    la, stride = _pack_lanes([jnp.where(valid & head_lane, la_all[b * m:(b + 1) * m], 0.0) for b in range(nseq)])
    cum, cum_t, e_cum, e_rem, e_last = _decay_terms(la)
    row, col = _iotas(m)
    levels = max(1, (min(m, hi) - 1).bit_length())

    ys = [None] * nseq
    for b0 in range(0, nseq, seq_group):
        units = []
        for b in range(b0, b0 + seq_group):
            a_b = act[b * m:(b + 1) * m]
            for h in range(GDN_HEADS):
                ln = b * stride + LANE_DECAY + h
                q_h = a_b[:, h * GDN_DK:(h + 1) * GDN_DK]
                k_h = a_b[:, GROUP_W + h * GDN_DK:GROUP_W + (h + 1) * GDN_DK]
                v_h = a_b[:, 2 * GROUP_W + h * GDN_DV:2 * GROUP_W + (h + 1) * GDN_DV]
                q_h = q_h * lax.rsqrt(jnp.sum(q_h * q_h, axis=-1, keepdims=True) + 1e-6) * (GDN_DK ** -0.5)
                k_h = k_h * lax.rsqrt(jnp.sum(k_h * k_h, axis=-1, keepdims=True) + 1e-6)
                k_h = jnp.where(valid, k_h, 0.0)
                b_h = _col(betas[b], LANE_BETA + h, GDN_DK)
                k_b = k_h * b_h
                decay = _decay_matrix(cum, cum_t, ln)
                k_pad = _pad_rows(k_h)
                s_h = s_scr[b, h]
                units.append(dict(
                    b=b, h=h, s=s_h,
                    a=jnp.where(row > col, _dot_nt(k_b, k_pad) * decay, 0.0),
                    v_b=_pad_rows(jnp.where(valid, v_h, 0.0) * b_h),
                    k_bw=_pad_rows(k_b * _col(e_cum, ln, GDN_DK)),
                    qk=_dot_nt(q_h, k_pad) * decay,
                    q_s=_dot(q_h * _col(e_cum, ln, GDN_DK), s_h),
                    k_w=k_h * _col(e_rem, ln, GDN_DK),
                    s_dec=s_h * e_last[:, ln:ln + 1]))
        ts = _unit_lower_inverses([u["a"] for u in units], levels)
        us = [_dot(t, u["v_b"]) for t, u in zip(ts, units)]
        ws = [_dot(t, u["k_bw"]) for t, u in zip(ts, units)]
        v_news = [x - _dot(w, u["s"]) for x, w, u in zip(us, ws, units)]
        os_ = [u["q_s"] + _dot(u["qk"], _pad_rows(v_new)) for u, v_new in zip(units, v_news)]
        for u, v_new in zip(units, v_news):
            s_scr[u["b"], u["h"]] = u["s_dec"] + _dot_tn(u["k_w"], v_new)
        for i in range(seq_group):
            heads = [_rms(o, nw_ref[...]) for o in os_[i * GDN_HEADS:(i + 1) * GDN_HEADS]]
            ys[b0 + i] = jnp.concatenate(heads, axis=1)
    y_ref[...] = (jnp.concatenate(ys, axis=0) * _silu(g_ref[...])).astype(y_ref.dtype)

    @pl.when(c == nchunks - 1)
    def _():
        so_ref[...] = s_scr[...]


def _swa_kernel(nseq, seq_group, rows, layer, is_prompt, *refs):
    if is_prompt:
        q_ref, g_ref, kvc_ref, kvp_ref, sink_ref, y_ref = refs
    else:
        q_ref, g_ref, kvc_ref, kp_ref, vp_ref, sink_ref, y_ref = refs
    n = pl.program_id(1)
    m = rows
    if is_prompt:
        lo_prev = jnp.where(n == 0, CHUNK, jnp.where(n == 1, PROMPT_PAD, 0))
        lo_cur = jnp.where(n == 0, PROMPT_PAD, 0)
    else:
        lo_prev, lo_cur = 0, 0
    qi = lax.broadcasted_iota(jnp.int32, (m, 2 * CHUNK), 0)
    kj = lax.broadcasted_iota(jnp.int32, (m, 2 * CHUNK), 1)
    dist = qi + WINDOW - kj
    key_ok = ((kj < CHUNK) & (kj >= lo_prev)) | (kj >= CHUNK + lo_cur)
    visible = (dist >= 0) & (dist <= WINDOW) & key_ok
    dist_f = dist.astype(F32)

    q_all = q_ref[...]
    kvc_all = kvc_ref[...]
    grp = SWA_HEADS // SWA_KV_HEADS
    ys = []
    for b0 in range(0, nseq, seq_group):
        units = []
        for b in range(b0, b0 + seq_group):
            q = q_all[b * m:(b + 1) * m]
            kvc = _pad_rows(kvc_all[b * m:(b + 1) * m])
            if is_prompt:
                k_prev = kvp_ref[b * CHUNK:(b + 1) * CHUNK, 0:KV_W]
                v_prev = kvp_ref[b * CHUNK:(b + 1) * CHUNK, KV_W:2 * KV_W]
            else:
                k_prev, v_prev = kp_ref[b], vp_ref[b]
            k_all = jnp.concatenate([k_prev, kvc[:, 0:KV_W]], axis=0)
            v_all = jnp.concatenate([v_prev, kvc[:, KV_W:2 * KV_W]], axis=0)
            for h in range(SWA_HEADS):
                kvh = h // grp
                units.append(dict(h=h, q=q[:, h * SWA_HEAD_DIM:(h + 1) * SWA_HEAD_DIM],
                                  k=k_all[:, kvh * SWA_HEAD_DIM:(kvh + 1) * SWA_HEAD_DIM],
                                  v=v_all[:, kvh * SWA_HEAD_DIM:(kvh + 1) * SWA_HEAD_DIM]))
        ss = [jnp.where(visible,
                        _dot_nt(u["q"], u["k"]) * (SWA_HEAD_DIM ** -0.5)
                        - 2.0 ** (-8.0 * (u["h"] + 1) / SWA_HEADS) * dist_f, -1e30) for u in units]
        mxs = [jnp.maximum(jnp.max(s, axis=-1, keepdims=True), sink_ref[layer, u["h"]])
               for s, u in zip(ss, units)]
        es = [jnp.exp(s - mx) for s, mx in zip(ss, mxs)]
        dens = [jnp.sum(e, axis=-1, keepdims=True) + jnp.exp(sink_ref[layer, u["h"]] - mx)
                for e, mx, u in zip(es, mxs, units)]
        os_ = [_dot(e, u["v"]) / den for e, u, den in zip(es, units, dens)]
        for i in range(seq_group):
            ys.append(jnp.concatenate(os_[i * SWA_HEADS:(i + 1) * SWA_HEADS], axis=1))
    y_ref[...] = (jnp.concatenate(ys, axis=0) * _silu(g_ref[...])).astype(y_ref.dtype)


def _geometry(sample):
    if sample:
        return (DEC_BATCH // SAMPLE_SEQS, 1), SAMPLE_SEQS, SUBLANES, N_SAMPLE, 0, DEC_SEQ
    return (1, PROMPT_CHUNKS), BATCH, CHUNK, N_PROMPT, PROMPT_PAD, CHUNK


def _rowblk(sample):
    return (lambda blk, c: blk) if sample else (lambda blk, c: c)


def _pcol(nrows, rowblk, width, col):
    assert col % width == 0
    return pl.BlockSpec((nrows, width), lambda blk, c: (rowblk(blk, c), col // width))


def _layer_param(shape, layer):
    return pl.BlockSpec((None,) + shape, lambda blk, c: (layer,) + (0,) * len(shape))


def _layer_seqs(nseq, shape, layer):
    return pl.BlockSpec((None, nseq) + shape, lambda blk, c: (layer, blk) + (0,) * len(shape))


def _seqs_out(nseq, shape):
    return pl.BlockSpec((nseq,) + shape, lambda blk, c: (blk,) + (0,) * len(shape))


_SMEM = pl.BlockSpec(memory_space=pltpu.SMEM)
_MIXER_PARAMS = pltpu.CompilerParams(dimension_semantics=("arbitrary", "arbitrary"),
                                     vmem_limit_bytes=VMEM_LIMIT)


def _ssd(p, prm, layer, init):
    sample = init is not None
    grid, nseq, rows, total, first_lo, hi = _geometry(sample)
    rb, n = _rowblk(sample), nseq * rows
    state = (SSD_HEADS, SSD_STATE, SSD_HEAD_DIM)
    in_specs = [_pcol(n, rb, GROUP_W, COL_Z), _pcol(n, rb, GROUP_W, COL_X),
                _pcol(n, rb, GROUP_W, COL_BC), _pcol(n, rb, CHUNK, COL_SMALL)]
    args = [p, p, p, p]
    if sample:
        in_specs += [_layer_seqs(nseq, state, layer), _layer_seqs(nseq, (SUBLANES, SSD_CONV_W), layer)]
        args += list(init)
    in_specs += [_layer_param((CONV_K, SSD_CONV_W), layer), _layer_param((1, SSD_CONV_W), layer),
                 _layer_param((SUBLANES, CHUNK), layer), _layer_param((1, GROUP_W), layer), _SMEM]
    args += [prm["ssd_conv_w"], prm["ssd_conv_b"], prm["lanes"], prm["ssd_norm"], prm["ssd_d"]]
    return pl.pallas_call(
        functools.partial(_ssd_kernel, nseq, rows, layer, sample, grid[1], first_lo, hi),
        grid=grid, in_specs=in_specs,
        out_specs=[pl.BlockSpec((n, GROUP_W), lambda blk, c: (rb(blk, c), 0)), _seqs_out(nseq, state)],
        out_shape=[jax.ShapeDtypeStruct((total, GROUP_W), BF16),
                   jax.ShapeDtypeStruct((grid[0] * nseq,) + state, F32)],
        scratch_shapes=[pltpu.VMEM((nseq,) + state, F32), pltpu.VMEM((nseq, 2 * SUBLANES, SSD_CONV_W), F32)],
        compiler_params=_MIXER_PARAMS,
        name="ssd_sample" if sample else "ssd_prompt",
    )(*args)


def _ret(p, prm, layer, init):
    sample = init is not None
    grid, nseq, rows, total, first_lo, hi = _geometry(sample)
    rb, n = _rowblk(sample), nseq * rows
    state = (RET_HEADS, RET_DK, RET_DV)
    in_specs = [_pcol(n, rb, GROUP_W, COL_QKR), _pcol(n, rb, GROUP_W, COL_VR), _pcol(n, rb, GROUP_W, COL_GR)]
    args = [p, p, p]
    if sample:
        in_specs += [_layer_seqs(nseq, state, layer)]
        args += list(init)
    in_specs += [_layer_param((SUBLANES, CHUNK), layer), _layer_param((1, GROUP_W), layer)]
    args += [prm["lanes"], prm["ret_norm"]]
    return pl.pallas_call(
        functools.partial(_ret_kernel, nseq, nseq if sample else 2, rows, sample, grid[1], first_lo, hi),
        grid=grid, in_specs=in_specs,
        out_specs=[pl.BlockSpec((n, GROUP_W), lambda blk, c: (rb(blk, c), 0)), _seqs_out(nseq, state)],
        out_shape=[jax.ShapeDtypeStruct((total, GROUP_W), BF16),
                   jax.ShapeDtypeStruct((grid[0] * nseq,) + state, F32)],
        scratch_shapes=[pltpu.VMEM((nseq,) + state, F32)],
        compiler_params=_MIXER_PARAMS,
        name="ret_sample" if sample else "ret_prompt",
    )(*args)


def _gdn(p, prm, layer, init):
    sample = init is not None
    grid, nseq, rows, total, first_lo, hi = _geometry(sample)
    rb, n = _rowblk(sample), nseq * rows
    state = (GDN_HEADS, GDN_DK, GDN_DV)
    in_specs = [_pcol(n, rb, GROUP_W, COL_QD), _pcol(n, rb, GROUP_W, COL_KD), _pcol(n, rb, GROUP_W, COL_VD),
                _pcol(n, rb, GROUP_W, COL_GD), _pcol(n, rb, CHUNK, COL_SMALL)]
    args = [p, p, p, p, p]
    if sample:
        in_specs += [_layer_seqs(nseq, state, layer), _layer_seqs(nseq, (SUBLANES, GDN_CONV_W), layer)]
        args += list(init)
    in_specs += [_layer_param((CONV_K, GDN_CONV_W), layer), _layer_param((SUBLANES, CHUNK), layer),
                 _layer_param((1, GDN_DV), layer)]
    args += [prm["gdn_conv_w"], prm["lanes"], prm["gdn_norm"]]
    return pl.pallas_call(
        functools.partial(_gdn_kernel, nseq, 2, rows, sample, grid[1], first_lo, hi),
        grid=grid, in_specs=in_specs,
        out_specs=[pl.BlockSpec((n, GROUP_W), lambda blk, c: (rb(blk, c), 0)), _seqs_out(nseq, state)],
        out_shape=[jax.ShapeDtypeStruct((total, GROUP_W), BF16),
                   jax.ShapeDtypeStruct((grid[0] * nseq,) + state, F32)],
        scratch_shapes=[pltpu.VMEM((nseq,) + state, F32), pltpu.VMEM((nseq, 2 * SUBLANES, GDN_CONV_W), F32)],
        compiler_params=_MIXER_PARAMS,
        name="gdn_sample" if sample else "gdn_prompt",
    )(*args)


def _swa(p, prm, layer, cache):
    sample = cache is not None
    grid, nseq, rows, total, _, _ = _geometry(sample)
    rb, n = _rowblk(sample), nseq * rows
    in_specs = [_pcol(n, rb, GROUP_W, COL_QA), _pcol(n, rb, GROUP_W, COL_GA), _pcol(n, rb, 2 * KV_W, COL_KV)]
    args = [p, p, p]
    if sample:
        in_specs += [_layer_seqs(nseq, (WINDOW, KV_W), layer), _layer_seqs(nseq, (WINDOW, KV_W), layer)]
        args += list(cache)
    else:
        in_specs += [pl.BlockSpec((n, 2 * KV_W), lambda blk, c: (jnp.maximum(c - 1, 0), COL_KV // (2 * KV_W)))]
        args += [p]
    in_specs += [_SMEM]
    args += [prm["swa_sinks"]]
    return pl.pallas_call(
        functools.partial(_swa_kernel, nseq, nseq if sample else 1, rows, layer, not sample),
        grid=grid, in_specs=in_specs,
        out_specs=pl.BlockSpec((n, GROUP_W), lambda blk, c: (rb(blk, c), 0)),
        out_shape=jax.ShapeDtypeStruct((total, GROUP_W), BF16),
        compiler_params=_MIXER_PARAMS,
        name="swa_sample" if sample else "swa_prompt",
    )(*args)


def _lane_table(ssd_dt_bias, ssd_a_log, gdn_dt_bias, gdn_a_log):
    def put(v, lane):
        return jnp.pad(v.astype(F32), ((0, 0), (lane, CHUNK - lane - v.shape[1])))
    log_gamma = jnp.log1p(-jnp.exp2(-5.0 - jnp.arange(RET_HEADS, dtype=F32)))
    rows = [put(ssd_dt_bias, LANE_DT), put(ssd_a_log, LANE_DT), put(gdn_dt_bias, LANE_DECAY),
            put(gdn_a_log, LANE_DECAY), put(jnp.broadcast_to(log_gamma, (DEPTH, RET_HEADS)), 0)]
    rows += [jnp.zeros((DEPTH, CHUNK), F32)] * (SUBLANES - len(rows))
    return jnp.stack(rows, axis=1)


def _conv_state_pad(state):
    return jnp.pad(state, ((0, 0), (0, 0), (SUBLANES - (CONV_K - 1), 0), (0, 0)))


def kernel(x_prompt, x_sample, state_ssd, state_ssd_conv, cache_swa_k, cache_swa_v, state_ret, state_gdn,
           state_gdn_conv, meta_tokens, pre_norm, post_norm, w_in, w_out, ssd_conv_w, ssd_conv_b, ssd_dt_bias,
           ssd_a_log, ssd_d, ssd_norm, swa_sinks, ret_norm, gdn_conv_w, gdn_dt_bias, gdn_a_log, gdn_norm):
    xs = jnp.pad(x_sample, ((0, 0), (0, SUBLANES - DEC_SEQ), (0, 0))).reshape(N_SAMPLE, D_MODEL)
    mask_np = np.ones((PROMPT_CHUNKS, BATCH, CHUNK), np.float32)
    mask_np[0, :, :PROMPT_PAD] = 0.0
    mask_p = jnp.asarray(mask_np.reshape(N_PROMPT, 1))
    mask_s = jnp.asarray(np.tile(np.arange(SUBLANES) < DEC_SEQ, DEC_BATCH)[:, None], F32)

    w_in_p = _wprep(jnp.swapaxes(w_in, 1, 2))
    w_out_b = w_out.astype(BF16)
    pre_g = pre_norm.reshape(DEPTH, 1, D_MODEL)
    post_g = post_norm.reshape(DEPTH, 1, D_MODEL)
    prm = dict(
        lanes=_lane_table(ssd_dt_bias, ssd_a_log, gdn_dt_bias, gdn_a_log),
        ssd_conv_w=ssd_conv_w, ssd_conv_b=ssd_conv_b.reshape(DEPTH, 1, SSD_CONV_W),
        ssd_norm=ssd_norm.reshape(DEPTH, 1, GROUP_W), ssd_d=ssd_d, swa_sinks=swa_sinks,
        ret_norm=ret_norm.reshape(DEPTH, 1, GROUP_W), gdn_conv_w=gdn_conv_w,
        gdn_norm=gdn_norm.reshape(DEPTH, 1, GDN_DV))
    ssd_c0 = _conv_state_pad(state_ssd_conv)
    gdn_c0 = _conv_state_pad(state_gdn_conv)
    cache_k = cache_swa_k.reshape(DEPTH, DEC_BATCH, WINDOW, KV_W)
    cache_v = cache_swa_v.reshape(DEPTH, DEC_BATCH, WINDOW, KV_W)

    names = ("ssd", "ssd_conv", "swa_k", "swa_v", "ret", "gdn", "gdn_conv")
    p_out = {k: [] for k in names}
    s_out = {k: [] for k in names}
    tail = slice(CHUNK - (CONV_K - 1), CHUNK)
    xp, hp = _embed_prompt(x_prompt, meta_tokens, pre_g)
    hs = _prenorm(xs, pre_g, 0, N_SAMPLE)
    for l in range(DEPTH):
        pp = _inproj(hp, w_in_p, l, PROMPT_TM)
        ya, p_ssd = _ssd(pp, prm, l, None)
        yb = _swa(pp, prm, l, None)
        yc, p_ret = _ret(pp, prm, l, None)
        yd, p_gdn = _gdn(pp, prm, l, None)
        xp, hp = _outproj((ya, yb, yc, yd), w_out_b, xp, post_g, mask_p, pre_g, l, PROMPT_TM,
                          by_sequence=l == DEPTH - 1)

        ps = _inproj(hs, w_in_p, l, N_SAMPLE)
        ya, s_ssd = _ssd(ps, prm, l, (state_ssd, ssd_c0))
        yb = _swa(ps, prm, l, (cache_k, cache_v))
        yc, s_ret = _ret(ps, prm, l, (state_ret,))
        yd, s_gdn = _gdn(ps, prm, l, (state_gdn, gdn_c0))
        xs, hs = _outproj((ya, yb, yc, yd), w_out_b, xs, post_g, mask_s, pre_g, l, N_SAMPLE)

        last = pp[N_PROMPT - BATCH * CHUNK:].reshape(BATCH, CHUNK, IN_W_PAD)
        ps3 = ps.reshape(DEC_BATCH, SUBLANES, IN_W_PAD)
        p_out["ssd"].append(p_ssd)
        p_out["ssd_conv"].append(last[:, tail, COL_X:COL_X + SSD_CONV_W])
        p_out["swa_k"].append(last[:, :, COL_KV:COL_KV + KV_W])
        p_out["swa_v"].append(last[:, :, COL_KV + KV_W:COL_KV + 2 * KV_W])
        p_out["ret"].append(p_ret)
        p_out["gdn"].append(p_gdn)
        p_out["gdn_conv"].append(last[:, tail, COL_QD:COL_QD + GDN_CONV_W])
        s_out["ssd"].append(s_ssd)
        s_out["ssd_conv"].append(ps3[:, 1:DEC_SEQ, COL_X:COL_X + SSD_CONV_W])
        s_out["swa_k"].append(ps3[:, :DEC_SEQ, COL_KV:COL_KV + KV_W])
        s_out["swa_v"].append(ps3[:, :DEC_SEQ, COL_KV + KV_W:COL_KV + 2 * KV_W])
        s_out["ret"].append(s_ret)
        s_out["gdn"].append(s_gdn)
        s_out["gdn_conv"].append(ps3[:, 1:DEC_SEQ, COL_QD:COL_QD + GDN_CONV_W])

    p_st = {k: jnp.stack(v) for k, v in p_out.items()}
    s_st = {k: jnp.stack(v) for k, v in s_out.items()}
    kv_shape = (SWA_KV_HEADS, SWA_HEAD_DIM)
    y_prompt = xp.reshape(BATCH, SEQ, D_MODEL)
    y_sample = xs.reshape(DEC_BATCH, SUBLANES, D_MODEL)[:, :DEC_SEQ]
    return (
        y_prompt, y_sample,
        p_st["ssd"], p_st["ssd_conv"],
        p_st["swa_k"].reshape((DEPTH, BATCH, WINDOW) + kv_shape),
        p_st["swa_v"].reshape((DEPTH, BATCH, WINDOW) + kv_shape),
        p_st["ret"], p_st["gdn"], p_st["gdn_conv"],
        s_st["ssd"], s_st["ssd_conv"],
        jnp.concatenate([cache_swa_k[:, :, DEC_SEQ:], s_st["swa_k"].reshape((DEPTH, DEC_BATCH, DEC_SEQ) + kv_shape)], axis=2),
        jnp.concatenate([cache_swa_v[:, :, DEC_SEQ:], s_st["swa_v"].reshape((DEPTH, DEC_BATCH, DEC_SEQ) + kv_shape)], axis=2),
        s_st["ret"], s_st["gdn"], s_st["gdn_conv"],
    )
```

```python
import functools

import jax
import jax.numpy as jnp
import numpy as np
from jax import lax
from jax.experimental import pallas as pl
from jax.experimental.pallas import tpu as pltpu

F32 = jnp.float32
BF16 = jnp.bfloat16
HIGHEST = lax.Precision.HIGHEST

D_MODEL = 2048
BATCH = 4
SEQ = 2048
DEPTH = 4
DEC_BATCH = 32
DEC_SEQ = 4
N_META = 16
GROUP_W = 512
CONV_K = 4
NORM_EPS = 1e-6
WINDOW = 128

SSD_HEADS, SSD_HEAD_DIM, SSD_GROUPS, SSD_STATE = 8, 64, 2, 128
SWA_HEADS, SWA_KV_HEADS, SWA_HEAD_DIM = 8, 2, 64
RET_HEADS, RET_DK, RET_DV = 4, 64, 128
GDN_HEADS, GDN_DK, GDN_DV = 4, 128, 128
SSD_CONV_W = GROUP_W + 2 * SSD_GROUPS * SSD_STATE
GDN_CONV_W = 3 * GROUP_W
KV_W = SWA_KV_HEADS * SWA_HEAD_DIM

CHUNK = 128
SUBLANES = 8
PROMPT_PAD = CHUNK - N_META
PROMPT_ROWS = PROMPT_PAD + N_META + SEQ
PROMPT_CHUNKS = PROMPT_ROWS // CHUNK
N_PROMPT = BATCH * PROMPT_ROWS
N_SAMPLE = DEC_BATCH * SUBLANES
SAMPLE_SEQS = 8

IN_W = 6416
IN_W_PAD = 6528
COL_Z, COL_X, COL_BC, COL_QA, COL_GA, COL_VR, COL_GR = 0, 512, 1024, 1536, 2048, 2560, 3072
COL_QD, COL_KD, COL_VD, COL_GD, COL_QKR, COL_KV, COL_SMALL = 3584, 4096, 4608, 5120, 5632, 6144, 6400
LANE_DT, LANE_BETA, LANE_DECAY = 0, 8, 12
ROW_SSD_DTB, ROW_SSD_ALOG, ROW_GDN_DTB, ROW_GDN_ALOG, ROW_LOG_GAMMA = 0, 1, 2, 3, 4

W_SEGMENTS = ((0, 1536), (1544, 2056), (2312, 2824), (3336, 3848), (3848, 4360), (4360, 5896), (5896, 6408),
              (2824, 3336), (2056, 2312))
W_BLOCKS = IN_W_PAD // CHUNK
W_PER_STEP = 3
W_STEPS = W_BLOCKS // W_PER_STEP
assert W_STEPS * W_PER_STEP == W_BLOCKS

PROJ_TN = 2176
PROMPT_TM = 512
VMEM_LIMIT = 48 * 1024 * 1024


def _sigmoid(x):
    return 1.0 / (1.0 + jnp.exp(-x))


def _silu(x):
    return x * _sigmoid(x)


def _softplus(x):
    return jnp.maximum(x, 0.0) + jnp.log1p(jnp.exp(-jnp.abs(x)))


def _dot(a, b):
    return jnp.dot(a.astype(BF16), b.astype(BF16), preferred_element_type=F32)


def _dot_nt(a, b):
    return lax.dot_general(a.astype(BF16), b.astype(BF16), (((1,), (1,)), ((), ())),
                           preferred_element_type=F32)


def _dot_tn(a, b):
    return lax.dot_general(a.astype(BF16), b.astype(BF16), (((0,), (0,)), ((), ())),
                           preferred_element_type=F32)


def _dot_f32(a, b):
    return jnp.dot(a, b, precision=HIGHEST, preferred_element_type=F32)


def _pad_rows(a, rows=CHUNK):
    if a.shape[0] == rows:
        return a
    return jnp.concatenate([a, jnp.zeros((rows - a.shape[0], a.shape[1]), a.dtype)], axis=0)


def _col(a, lane, width):
    return jnp.broadcast_to(a[:, lane:lane + 1], (a.shape[0], width))


def _iotas(m):
    row = lax.broadcasted_iota(jnp.int32, (m, CHUNK), 0)
    col = lax.broadcasted_iota(jnp.int32, (m, CHUNK), 1)
    return row, col


def _valid_rows(m, chunk_idx, first_lo, hi):
    r = lax.broadcasted_iota(jnp.int32, (m, 1), 0)
    lo = jnp.where(chunk_idx == 0, first_lo, 0)
    return (r >= lo) & (r < hi)


def _pack_lanes(per_seq):
    stride = CHUNK // len(per_seq)
    out = per_seq[0]
    for b in range(1, len(per_seq)):
        out = out + pltpu.roll(per_seq[b], b * stride, 1)
    return out, stride


def _decay_terms(la):
    m = la.shape[0]
    row, col = _iotas(m)
    lower = (row >= col).astype(F32)
    cum = _dot_f32(lower, _pad_rows(la))
    last = cum[m - 1:m, :]
    return cum, _pad_rows(cum).T, jnp.exp(cum), jnp.exp(last - cum), jnp.exp(last)


def _decay_matrix(cum, cum_t, lane):
    m = cum.shape[0]
    row, col = _iotas(m)
    keep = row >= col
    seg = _col(cum, lane, CHUNK) - jnp.broadcast_to(cum_t[lane:lane + 1, :], (m, CHUNK))
    return jnp.where(keep, jnp.exp(jnp.where(keep, seg, 0.0)), 0.0)


def _shift_rows_matrix(rows):
    r = lax.broadcasted_iota(jnp.int32, (rows, rows), 0)
    c = lax.broadcasted_iota(jnp.int32, (rows, rows), 1)
    blocks = [jnp.where(r - c == CONV_K - 1 - j, 1.0, 0.0) for j in range(CONV_K - 1)]
    return jnp.concatenate(blocks, axis=0).astype(BF16)


def _causal_conv(ext_ref, cur, w_ref, shift_mat):
    rows = cur.shape[0]
    w_last = w_ref[CONV_K - 1:CONV_K, :]
    ext_ref[SUBLANES:2 * SUBLANES, :] = cur[0:SUBLANES]
    head = cur[0:SUBLANES] * w_last
    for j in range(CONV_K - 1):
        start = SUBLANES - (CONV_K - 1) + j
        head = head + ext_ref[start:start + SUBLANES, :] * w_ref[j:j + 1, :]
    ext_ref[0:SUBLANES, :] = cur[rows - SUBLANES:rows]
    if rows == SUBLANES:
        return head
    shifted = jnp.dot(shift_mat, cur.astype(BF16), preferred_element_type=F32)
    acc = cur * w_last
    for j in range(CONV_K - 1):
        acc = acc + shifted[j * rows:(j + 1) * rows] * w_ref[j:j + 1, :]
    return jnp.concatenate([head, acc[SUBLANES:]], axis=0)


def _rms(x, w):
    ms = jnp.mean(x * x, axis=-1, keepdims=True)
    return x * lax.rsqrt(ms + NORM_EPS) * w


def _prenorm_kernel(x_ref, g_ref, h_ref):
    h_ref[...] = _rms(x_ref[...], g_ref[...]).astype(BF16)


def _prenorm(x, g_all, layer, tm):
    n = x.shape[0]
    return pl.pallas_call(
        _prenorm_kernel,
        grid=(n // tm,),
        in_specs=[pl.BlockSpec((tm, D_MODEL), lambda i: (i, 0)),
                  pl.BlockSpec((None, 1, D_MODEL), lambda i: (layer, 0, 0))],
        out_specs=pl.BlockSpec((tm, D_MODEL), lambda i: (i, 0)),
        out_shape=jax.ShapeDtypeStruct((n, D_MODEL), BF16),
        compiler_params=pltpu.CompilerParams(dimension_semantics=("arbitrary",)),
        name="prenorm",
    )(x, g_all)


def _wprep_kernel(offs_ref, a0_ref, a1_ref, a2_ref, b_ref, o_ref):
    j = pl.program_id(1)
    o_ref[0:CHUNK, :] = a0_ref[...].astype(BF16)
    o_ref[CHUNK:2 * CHUNK, :] = a1_ref[...].astype(BF16)

    @pl.when(j < W_STEPS - 1)
    def _():
        o_ref[2 * CHUNK:3 * CHUNK, :] = a2_ref[...].astype(BF16)

    @pl.when(j == W_STEPS - 1)
    def _():
        small = jnp.concatenate([a2_ref[0:SUBLANES, :], b_ref[...],
                                 jnp.zeros((CHUNK - 2 * SUBLANES, D_MODEL), F32)], axis=0)
        o_ref[2 * CHUNK:3 * CHUNK, :] = small.astype(BF16)


def _wprep(w_t):
    offs = np.concatenate([np.arange(a, b, CHUNK) for a, b in W_SEGMENTS] + [[1536]]).astype(np.int32)
    assert offs.shape[0] == W_BLOCKS and not (offs % SUBLANES).any()
    offs = offs // SUBLANES
    def source_block(k):
        return pl.BlockSpec(
            (pl.Element(CHUNK), pl.Element(D_MODEL)),
            lambda l, j, offs: ((l * (IN_W // SUBLANES) + offs[W_PER_STEP * j + k]) * SUBLANES, 0))

    grid_spec = pltpu.PrefetchScalarGridSpec(
        num_scalar_prefetch=1,
        grid=(DEPTH, W_STEPS),
        in_specs=[source_block(k) for k in range(W_PER_STEP)]
        + [pl.BlockSpec((SUBLANES, D_MODEL), lambda l, j, offs: ((l * IN_W + 6408) // SUBLANES, 0))],
        out_specs=pl.BlockSpec((None, W_PER_STEP * CHUNK, D_MODEL), lambda l, j, offs: (l, j, 0)),
    )
    w_rows = w_t.reshape(DEPTH * IN_W, D_MODEL)
    return pl.pallas_call(
        _wprep_kernel, grid_spec=grid_spec,
        out_shape=jax.ShapeDtypeStruct((DEPTH, IN_W_PAD, D_MODEL), BF16),
        compiler_params=pltpu.CompilerParams(dimension_semantics=("arbitrary", "arbitrary")),
        name="wprep",
    )(jnp.asarray(offs), w_rows, w_rows, w_rows, w_rows)


def _inproj_kernel(h_ref, w_ref, o_ref):
    o_ref[...] = lax.dot_general(h_ref[...], w_ref[...], (((1,), (1,)), ((), ())),
                                 preferred_element_type=F32)


def _inproj(h, w_all, layer, tm):
    n = h.shape[0]
    return pl.pallas_call(
        _inproj_kernel,
        grid=(IN_W_PAD // PROJ_TN, n // tm),
        in_specs=[
            pl.BlockSpec((tm, D_MODEL), lambda j, i: (i, 0)),
            pl.BlockSpec((None, PROJ_TN, D_MODEL), lambda j, i: (layer, j, 0)),
        ],
        out_specs=pl.BlockSpec((tm, PROJ_TN), lambda j, i: (i, j)),
        out_shape=jax.ShapeDtypeStruct((n, IN_W_PAD), F32),
        compiler_params=pltpu.CompilerParams(
            dimension_semantics=("arbitrary", "arbitrary"), vmem_limit_bytes=VMEM_LIMIT),
        name="inproj",
    )(h, w_all)


def _embed_prompt_kernel(x_ref, meta_ref, g_ref, o_ref, h_ref):
    c = pl.program_id(0)

    @pl.when(c == 0)
    def _():
        head = jnp.concatenate([jnp.zeros((PROMPT_PAD, D_MODEL), F32), meta_ref[...]], axis=0)
        for b in range(BATCH):
            o_ref[b * CHUNK:(b + 1) * CHUNK, :] = head

    @pl.when(c > 0)
    def _():
        for b in range(BATCH):
            o_ref[b * CHUNK:(b + 1) * CHUNK, :] = x_ref[b]

    h_ref[...] = _rms(o_ref[...], g_ref[...]).astype(BF16)


def _embed_prompt(x_prompt, meta_tokens, g_all):
    x4 = x_prompt.reshape(BATCH, SEQ // CHUNK, CHUNK, D_MODEL)
    row = pl.BlockSpec((BATCH * CHUNK, D_MODEL), lambda c: (c, 0))
    return pl.pallas_call(
        _embed_prompt_kernel,
        grid=(PROMPT_CHUNKS,),
        in_specs=[pl.BlockSpec((BATCH, None, CHUNK, D_MODEL), lambda c: (0, jnp.maximum(c - 1, 0), 0, 0)),
                  pl.BlockSpec((N_META, D_MODEL), lambda c: (0, 0)),
                  pl.BlockSpec((None, 1, D_MODEL), lambda c: (0, 0, 0))],
        out_specs=[row, row],
        out_shape=[jax.ShapeDtypeStruct((N_PROMPT, D_MODEL), F32),
                   jax.ShapeDtypeStruct((N_PROMPT, D_MODEL), BF16)],
        compiler_params=pltpu.CompilerParams(dimension_semantics=("arbitrary",), vmem_limit_bytes=VMEM_LIMIT),
        name="embed_prompt",
    )(x4, meta_tokens, g_all)


def _outproj_kernel(with_next, by_sequence, *refs):
    if with_next:
        ya_ref, yb_ref, yc_ref, yd_ref, w_ref, x_ref, g_ref, m_ref, gn_ref, o_ref, h_ref = refs
    else:
        ya_ref, yb_ref, yc_ref, yd_ref, w_ref, x_ref, g_ref, m_ref, o_ref = refs
    acc = None
    for g, y_ref in enumerate((ya_ref, yb_ref, yc_ref, yd_ref)):
        part = jnp.dot(y_ref[...], w_ref[g * GROUP_W:(g + 1) * GROUP_W, :], preferred_element_type=F32)
        acc = part if acc is None else acc + part
    x_new = jnp.where(m_ref[...] > 0.0, x_ref[...] + _rms(acc, g_ref[...]), 0.0)
    if by_sequence:
        for b in range(BATCH):
            o_ref[b] = x_new[b * CHUNK:(b + 1) * CHUNK]
    else:
        o_ref[...] = x_new
    if with_next:
        h_ref[...] = _rms(x_new, gn_ref[...]).astype(BF16)


def _outproj(ys, w_all, x, post_all, rowmask, pre_all, layer, tm, by_sequence=False):
    n = x.shape[0]
    with_next = layer + 1 < DEPTH
    yspec = pl.BlockSpec((tm, GROUP_W), lambda i: (i, 0))
    row = pl.BlockSpec((tm, D_MODEL), lambda i: (i, 0))
    in_specs = [yspec, yspec, yspec, yspec,
                pl.BlockSpec((None, D_MODEL, D_MODEL), lambda i: (layer, 0, 0)),
                row,
                pl.BlockSpec((None, 1, D_MODEL), lambda i: (layer, 0, 0)),
                pl.BlockSpec((tm, 1), lambda i: (i, 0))]
    args = list(ys) + [w_all, x, post_all, rowmask]
    if by_sequence:
        assert tm == BATCH * CHUNK and not with_next
        out_specs = [pl.BlockSpec((BATCH, None, CHUNK, D_MODEL), lambda i: (0, jnp.maximum(i - 1, 0), 0, 0))]
        out_shape = [jax.ShapeDtypeStruct((BATCH, SEQ // CHUNK, CHUNK, D_MODEL), F32)]
    else:
        out_specs = [row]
        out_shape = [jax.ShapeDtypeStruct((n, D_MODEL), F32)]
    if with_next:
        in_specs.append(pl.BlockSpec((None, 1, D_MODEL), lambda i: (layer + 1, 0, 0)))
        args.append(pre_all)
        out_specs.append(row)
        out_shape.append(jax.ShapeDtypeStruct((n, D_MODEL), BF16))
    res = pl.pallas_call(
        functools.partial(_outproj_kernel, with_next, by_sequence),
        grid=(n // tm,), in_specs=in_specs, out_specs=out_specs, out_shape=out_shape,
        compiler_params=pltpu.CompilerParams(
            dimension_semantics=("arbitrary",), vmem_limit_bytes=VMEM_LIMIT),
        name="outproj",
    )(*args)
    return (res[0], res[1]) if with_next else (res[0], None)


def _init_state(c, has_init, s_scr, s0_ref, ext=None, c0_ref=None):
    @pl.when(c == 0)
    def _():
        if has_init:
            s_scr[...] = s0_ref[...]
            if ext is not None:
                ext[:, 0:SUBLANES, :] = c0_ref[...]
        else:
            s_scr[...] = jnp.zeros_like(s_scr)
            if ext is not None:
                ext[:, 0:SUBLANES, :] = jnp.zeros((ext.shape[0], SUBLANES, ext.shape[2]), F32)


def _run_staggered(groups, skew):
    pending, live, tick = list(groups), [], 0
    while pending or live:
        if pending and tick % skew == 0:
            live.append(pending.pop(0))
        for gen in list(live):
            if next(gen, "done") == "done":
                live.remove(gen)
        tick += 1


def _ssd_kernel(nseq, rows, layer, has_init, nchunks, first_lo, hi, *refs):
    if has_init:
        (z_ref, x_ref, bc_ref, sm_ref, s0_ref, c0_ref, cw_ref, cb_ref, lp_ref, nw_ref, d_ref,
         y_ref, so_ref, s_scr, ext) = refs
    else:
        (z_ref, x_ref, bc_ref, sm_ref, cw_ref, cb_ref, lp_ref, nw_ref, d_ref,
         y_ref, so_ref, s_scr, ext) = refs
        s0_ref = c0_ref = None
    c = pl.program_id(1)
    m = rows
    _init_state(c, has_init, s_scr, s0_ref, ext, c0_ref)

    valid = _valid_rows(m, c, first_lo, hi)
    lane = lax.broadcasted_iota(jnp.int32, (1, CHUNK), 1)
    head_lane = (lane >= LANE_DT) & (lane < LANE_DT + SSD_HEADS)
    dt_all = _softplus(sm_ref[...] + lp_ref[ROW_SSD_DTB:ROW_SSD_DTB + 1, :])
    la_all = -jnp.exp(lp_ref[ROW_SSD_ALOG:ROW_SSD_ALOG + 1, :]) * dt_all
    dts = [jnp.where(valid, dt_all[b * m:(b + 1) * m], 0.0) for b in range(nseq)]
    la, stride = _pack_lanes([jnp.where(valid & head_lane, la_all[b * m:(b + 1) * m], 0.0) for b in range(nseq)])
    cum, cum_t, e_cum, e_rem, e_last = _decay_terms(la)

    cur = jnp.concatenate([x_ref[...], bc_ref[...]], axis=1)
    shift_mat = _shift_rows_matrix(m) if m > SUBLANES else None
    conv = jnp.concatenate(
        [_causal_conv(ext.at[b], cur[b * m:(b + 1) * m], cw_ref, shift_mat) for b in range(nseq)], axis=0)
    act = _silu(conv + cb_ref[...])

    hpg = SSD_HEADS // SSD_GROUPS
    c_off = GROUP_W + SSD_GROUPS * SSD_STATE
    scores = {}
    for b in range(nseq):
        a_b = act[b * m:(b + 1) * m]
        for g in range(SSD_GROUPS):
            b_g = a_b[:, GROUP_W + g * SSD_STATE:GROUP_W + (g + 1) * SSD_STATE]
            c_g = a_b[:, c_off + g * SSD_STATE:c_off + (g + 1) * SSD_STATE]
            scores[b, g] = (_dot_nt(c_g, _pad_rows(b_g)), b_g, c_g)
    ys = []
    for b in range(nseq):
        a_b = act[b * m:(b + 1) * m]
        heads = []
        for h in range(SSD_HEADS):
            sc, b_g, c_g = scores[b, h // hpg]
            ln = b * stride + LANE_DT + h
            x_h = a_b[:, h * SSD_HEAD_DIM:(h + 1) * SSD_HEAD_DIM]
            v_h = x_h * _col(dts[b], LANE_DT + h, SSD_HEAD_DIM)
            s_h = s_scr[b, h]
            o = (_dot(sc * _decay_matrix(cum, cum_t, ln), _pad_rows(v_h))
                 + _dot(c_g, s_h) * _col(e_cum, ln, SSD_HEAD_DIM))
            s_scr[b, h] = s_h * e_last[:, ln:ln + 1] + _dot_tn(b_g, v_h * _col(e_rem, ln, SSD_HEAD_DIM))
            heads.append(o + d_ref[layer, h] * x_h)
        ys.append(jnp.concatenate(heads, axis=1))
    y = jnp.concatenate(ys, axis=0) * _silu(z_ref[...])
    gw = GROUP_W // SSD_GROUPS
    outs = [_rms(y[:, g * gw:(g + 1) * gw], nw_ref[:, g * gw:(g + 1) * gw]) for g in range(SSD_GROUPS)]
    y_ref[...] = jnp.concatenate(outs, axis=1).astype(y_ref.dtype)

    @pl.when(c == nchunks - 1)
    def _():
        so_ref[...] = s_scr[...]


def _ret_kernel(nseq, seq_group, rows, has_init, nchunks, first_lo, hi, *refs):
    if has_init:
        qk_ref, v_ref, g_ref, s0_ref, lp_ref, nw_ref, y_ref, so_ref, s_scr = refs
    else:
        qk_ref, v_ref, g_ref, lp_ref, nw_ref, y_ref, so_ref, s_scr = refs
        s0_ref = None
    c = pl.program_id(1)
    m = rows
    _init_state(c, has_init, s_scr, s0_ref)

    valid = _valid_rows(m, c, first_lo, hi)
    lane = lax.broadcasted_iota(jnp.int32, (1, CHUNK), 1)
    log_gamma = jnp.broadcast_to(lp_ref[ROW_LOG_GAMMA:ROW_LOG_GAMMA + 1, :], (m, CHUNK))
    la_one = jnp.where(valid & (lane < RET_HEADS), log_gamma, 0.0)
    la, stride = _pack_lanes([la_one] * nseq)
    cum, cum_t, e_cum, e_rem, e_last = _decay_terms(la)

    qk_all = qk_ref[...]
    v_all = v_ref[...]
    all_units = []
    for b in range(nseq):
        qk = qk_all[b * m:(b + 1) * m]
        v = jnp.where(valid, v_all[b * m:(b + 1) * m], 0.0)
        for h in range(RET_HEADS):
            q_h = qk[:, h * RET_DK:(h + 1) * RET_DK]
            k_h = qk[:, (RET_HEADS + h) * RET_DK:(RET_HEADS + h + 1) * RET_DK] * (RET_DK ** -0.5)
            s_h = s_scr[b, h]
            all_units.append(dict(b=b, h=h, ln=b * stride + h, k=k_h, v=v[:, h * RET_DV:(h + 1) * RET_DV],
                                  s=s_h, raw=_dot_nt(q_h, _pad_rows(k_h)), q_s=_dot(q_h, s_h)))
    ys = [None] * nseq

    def group(b0):
        units = all_units[b0 * RET_HEADS:(b0 + seq_group) * RET_HEADS]
        scs = [u["raw"] * _decay_matrix(cum, cum_t, u["ln"]) for u in units]
        yield
        os_ = [_dot(sc, _pad_rows(u["v"])) + u["q_s"] * _col(e_cum, u["ln"], RET_DV)
               for sc, u in zip(scs, units)]
        yield
        for u in units:
            ln = u["ln"]
            s_scr[u["b"], u["h"]] = (u["s"] * e_last[:, ln:ln + 1]
                                     + _dot_tn(u["k"] * _col(e_rem, ln, RET_DK), u["v"]))
        yield
        for i in range(seq_group):
            heads = [_rms(os_[i * RET_HEADS + h], nw_ref[:, h * RET_DV:(h + 1) * RET_DV])
                     for h in range(RET_HEADS)]
            rows_b = slice((b0 + i) * m, (b0 + i + 1) * m)
            ys[b0 + i] = jnp.concatenate(heads, axis=1) * _silu(g_ref[rows_b, :])

    _run_staggered([group(b0) for b0 in range(0, nseq, seq_group)], 1)
    y_ref[...] = jnp.concatenate(ys, axis=0).astype(y_ref.dtype)

    @pl.when(c == nchunks - 1)
    def _():
        so_ref[...] = s_scr[...]


def _gdn_kernel(nseq, seq_group, skew, rows, has_init, nchunks, first_lo, hi, *refs):
    if has_init:
        (q_ref, k_ref, v_ref, g_ref, sm_ref, s0_ref, c0_ref, cw_ref, lp_ref, nw_ref,
         y_ref, so_ref, s_scr, ext) = refs
    else:
        (q_ref, k_ref, v_ref, g_ref, sm_ref, cw_ref, lp_ref, nw_ref,
         y_ref, so_ref, s_scr, ext) = refs
        s0_ref = c0_ref = None
    c = pl.program_id(1)
    m = rows
    _init_state(c, has_init, s_scr, s0_ref, ext, c0_ref)

    shift_mat = _shift_rows_matrix(m) if m > SUBLANES else None
    valid = _valid_rows(m, c, first_lo, hi)
    lane = lax.broadcasted_iota(jnp.int32, (1, CHUNK), 1)
    head_lane = (lane >= LANE_DECAY) & (lane < LANE_DECAY + GDN_HEADS)
    sm = sm_ref[...]
    beta_all = _sigmoid(sm)
    la_all = (-jnp.exp(lp_ref[ROW_GDN_ALOG:ROW_GDN_ALOG + 1, :])
              * _softplus(sm + lp_ref[ROW_GDN_DTB:ROW_GDN_DTB + 1, :]))
    betas = [jnp.where(valid, beta_all[b * m:(b + 1) * m], 0.0) for b in range(nseq)]
    la, stride = _pack_lanes([jnp.where(valid & head_lane, la_all[b * m:(b + 1) * m], 0.0) for b in range(nseq)])
    cum, cum_t, e_cum, e_rem, e_last = _decay_terms(la)
    row, col = _iotas(m)
    levels = max(1, (min(m, hi) - 1).bit_length())

    ys = [None] * nseq

    def joining_mask(shift):
        same_big = lax.shift_right_logical(row, shift + 1) == lax.shift_right_logical(col, shift + 1)
        same_small = lax.shift_right_logical(row, shift) == lax.shift_right_logical(col, shift)
        return same_big & jnp.logical_not(same_small) & (row > col)

    def group(b0):
        units = []
        for b in range(b0, b0 + seq_group):
            rows_b = slice(b * m, (b + 1) * m)
            cur = jnp.concatenate([q_ref[rows_b, :], k_ref[rows_b, :], v_ref[rows_b, :]], axis=1)
            a_b = _silu(_causal_conv(ext.at[b], cur, cw_ref, shift_mat))
            yield
            for h in range(GDN_HEADS):
                ln = b * stride + LANE_DECAY + h
                q_h = a_b[:, h * GDN_DK:(h + 1) * GDN_DK]
                k_h = a_b[:, GROUP_W + h * GDN_DK:GROUP_W + (h + 1) * GDN_DK]
                v_h = a_b[:, 2 * GROUP_W + h * GDN_DV:2 * GROUP_W + (h + 1) * GDN_DV]
                q_h = q_h * lax.rsqrt(jnp.sum(q_h * q_h, axis=-1, keepdims=True) + 1e-6) * (GDN_DK ** -0.5)
                k_h = k_h * lax.rsqrt(jnp.sum(k_h * k_h, axis=-1, keepdims=True) + 1e-6)
                k_h = jnp.where(valid, k_h, 0.0)
                b_h = _col(betas[b], LANE_BETA + h, GDN_DK)
                k_b = k_h * b_h
                decay = _decay_matrix(cum, cum_t, ln)
                k_pad = _pad_rows(k_h)
                units.append(dict(
                    b=b, h=h, ln=ln, q=q_h, k=k_h, k_b=k_b, decay=decay,
                    a=jnp.where(row > col, _dot_nt(k_b, k_pad) * decay, 0.0),
                    v_b=_pad_rows(jnp.where(valid, v_h, 0.0) * b_h),
                    qk=_dot_nt(q_h, k_pad) * decay))
                yield
        eye = (row == col).astype(F32)
        mask = joining_mask(0)
        ts = [eye - jnp.where(mask, u["a"], 0.0) for u in units]
        for shift in range(1, levels):
            mask = joining_mask(shift)
            inner = [_dot(jnp.where(mask, u["a"], 0.0), _pad_rows(t)) for u, t in zip(units, ts)]
            yield
            ts = [t - _dot(t, _pad_rows(x)) for t, x in zip(ts, inner)]
            yield
        us = [_dot(t, u["v_b"]) for t, u in zip(ts, units)]
        ws = [_dot(t, _pad_rows(u["k_b"] * _col(e_cum, u["ln"], GDN_DK))) for t, u in zip(ts, units)]
        yield
        ss = [s_scr[u["b"], u["h"]] for u in units]
        v_news = [x - _dot(w, s_h) for x, w, s_h in zip(us, ws, ss)]
        yield
        os_ = [_dot(u["q"] * _col(e_cum, u["ln"], GDN_DK), s_h) + _dot(u["qk"], _pad_rows(v_new))
               for u, s_h, v_new in zip(units, ss, v_news)]
        yield
        for u, s_h, v_new in zip(units, ss, v_news):
            ln = u["ln"]
            s_scr[u["b"], u["h"]] = (s_h * e_last[:, ln:ln + 1]
                                     + _dot_tn(u["k"] * _col(e_rem, ln, GDN_DK), v_new))
        yield
        for i in range(seq_group):
            heads = [_rms(o, nw_ref[...]) for o in os_[i * GDN_HEADS:(i + 1) * GDN_HEADS]]
            rows_b = slice((b0 + i) * m, (b0 + i + 1) * m)
            ys[b0 + i] = jnp.concatenate(heads, axis=1) * _silu(g_ref[rows_b, :])

    _run_staggered([group(b0) for b0 in range(0, nseq, seq_group)], skew)
    y_ref[...] = jnp.concatenate(ys, axis=0).astype(y_ref.dtype)

    @pl.when(c == nchunks - 1)
    def _():
        so_ref[...] = s_scr[...]


def _swa_kernel(nseq, seq_group, rows, layer, is_prompt, *refs):
    if is_prompt:
        q_ref, g_ref, kvc_ref, kvp_ref, sink_ref, y_ref = refs
    else:
        q_ref, g_ref, kvc_ref, kp_ref, vp_ref, sink_ref, y_ref = refs
    n = pl.program_id(1)
    m = rows
    if is_prompt:
        lo_prev = jnp.where(n == 0, CHUNK, jnp.where(n == 1, PROMPT_PAD, 0))
        lo_cur = jnp.where(n == 0, PROMPT_PAD, 0)
    else:
        lo_prev, lo_cur = 0, 0
    qi = lax.broadcasted_iota(jnp.int32, (m, 2 * CHUNK), 0)
    kj = lax.broadcasted_iota(jnp.int32, (m, 2 * CHUNK), 1)
    dist = qi + WINDOW - kj
    key_ok = ((kj < CHUNK) & (kj >= lo_prev)) | (kj >= CHUNK + lo_cur)
    visible = (dist >= 0) & (dist <= WINDOW) & key_ok
    dist_f = dist.astype(F32)

    q_all = q_ref[...]
    kvc_all = kvc_ref[...]
    grp = SWA_HEADS // SWA_KV_HEADS
    ys = []
    for b0 in range(0, nseq, seq_group):
        units = []
        for b in range(b0, b0 + seq_group):
            q = q_all[b * m:(b + 1) * m]
            kvc = _pad_rows(kvc_all[b * m:(b + 1) * m])
            if is_prompt:
                k_prev = kvp_ref[b * CHUNK:(b + 1) * CHUNK, 0:KV_W]
                v_prev = kvp_ref[b * CHUNK:(b + 1) * CHUNK, KV_W:2 * KV_W]
            else:
                k_prev, v_prev = kp_ref[b], vp_ref[b]
            k_all = jnp.concatenate([k_prev, kvc[:, 0:KV_W]], axis=0)
            v_all = jnp.concatenate([v_prev, kvc[:, KV_W:2 * KV_W]], axis=0)
            for h in range(SWA_HEADS):
                kvh = h // grp
                units.append(dict(h=h, q=q[:, h * SWA_HEAD_DIM:(h + 1) * SWA_HEAD_DIM],
                                  k=k_all[:, kvh * SWA_HEAD_DIM:(kvh + 1) * SWA_HEAD_DIM],
                                  v=v_all[:, kvh * SWA_HEAD_DIM:(kvh + 1) * SWA_HEAD_DIM]))
        ss = [jnp.where(visible,
                        _dot_nt(u["q"], u["k"]) * (SWA_HEAD_DIM ** -0.5)
                        - 2.0 ** (-8.0 * (u["h"] + 1) / SWA_HEADS) * dist_f, -1e30) for u in units]
        mxs = [jnp.maximum(jnp.max(s, axis=-1, keepdims=True), sink_ref[layer, u["h"]])
               for s, u in zip(ss, units)]
        es = [jnp.exp(s - mx) for s, mx in zip(ss, mxs)]
        dens = [jnp.sum(e, axis=-1, keepdims=True) + jnp.exp(sink_ref[layer, u["h"]] - mx)
                for e, mx, u in zip(es, mxs, units)]
        os_ = [_dot(e, u["v"]) / den for e, u, den in zip(es, units, dens)]
        for i in range(seq_group):
            ys.append(jnp.concatenate(os_[i * SWA_HEADS:(i + 1) * SWA_HEADS], axis=1))
    y_ref[...] = (jnp.concatenate(ys, axis=0) * _silu(g_ref[...])).astype(y_ref.dtype)


def _geometry(sample):
    if sample:
        return (DEC_BATCH // SAMPLE_SEQS, 1), SAMPLE_SEQS, SUBLANES, N_SAMPLE, 0, DEC_SEQ
    return (1, PROMPT_CHUNKS), BATCH, CHUNK, N_PROMPT, PROMPT_PAD, CHUNK


def _rowblk(sample):
    return (lambda blk, c: blk) if sample else (lambda blk, c: c)


def _pcol(nrows, rowblk, width, col):
    assert col % width == 0
    return pl.BlockSpec((nrows, width), lambda blk, c: (rowblk(blk, c), col // width))


def _layer_param(shape, layer):
    return pl.BlockSpec((None,) + shape, lambda blk, c: (layer,) + (0,) * len(shape))


def _layer_seqs(nseq, shape, layer):
    return pl.BlockSpec((None, nseq) + shape, lambda blk, c: (layer, blk) + (0,) * len(shape))


def _seqs_out(nseq, shape):
    return pl.BlockSpec((nseq,) + shape, lambda blk, c: (blk,) + (0,) * len(shape))


_SMEM = pl.BlockSpec(memory_space=pltpu.SMEM)
_MIXER_PARAMS = pltpu.CompilerParams(dimension_semantics=("arbitrary", "arbitrary"),
                                     vmem_limit_bytes=VMEM_LIMIT)


def _ssd(p, prm, layer, init):
    sample = init is not None
    grid, nseq, rows, total, first_lo, hi = _geometry(sample)
    rb, n = _rowblk(sample), nseq * rows
    state = (SSD_HEADS, SSD_STATE, SSD_HEAD_DIM)
    in_specs = [_pcol(n, rb, GROUP_W, COL_Z), _pcol(n, rb, GROUP_W, COL_X),
                _pcol(n, rb, GROUP_W, COL_BC), _pcol(n, rb, CHUNK, COL_SMALL)]
    args = [p, p, p, p]
    if sample:
        in_specs += [_layer_seqs(nseq, state, layer), _layer_seqs(nseq, (SUBLANES, SSD_CONV_W), layer)]
        args += list(init)
    in_specs += [_layer_param((CONV_K, SSD_CONV_W), layer), _layer_param((1, SSD_CONV_W), layer),
                 _layer_param((SUBLANES, CHUNK), layer), _layer_param((1, GROUP_W), layer), _SMEM]
    args += [prm["ssd_conv_w"], prm["ssd_conv_b"], prm["lanes"], prm["ssd_norm"], prm["ssd_d"]]
    return pl.pallas_call(
        functools.partial(_ssd_kernel, nseq, rows, layer, sample, grid[1], first_lo, hi),
        grid=grid, in_specs=in_specs,
        out_specs=[pl.BlockSpec((n, GROUP_W), lambda blk, c: (rb(blk, c), 0)), _seqs_out(nseq, state)],
        out_shape=[jax.ShapeDtypeStruct((total, GROUP_W), BF16),
                   jax.ShapeDtypeStruct((grid[0] * nseq,) + state, F32)],
        scratch_shapes=[pltpu.VMEM((nseq,) + state, F32), pltpu.VMEM((nseq, 2 * SUBLANES, SSD_CONV_W), F32)],
        compiler_params=_MIXER_PARAMS,
        name="ssd_sample" if sample else "ssd_prompt",
    )(*args)


def _ret(p, prm, layer, init):
    sample = init is not None
    grid, nseq, rows, total, first_lo, hi = _geometry(sample)
    rb, n = _rowblk(sample), nseq * rows
    state = (RET_HEADS, RET_DK, RET_DV)
    in_specs = [_pcol(n, rb, GROUP_W, COL_QKR), _pcol(n, rb, GROUP_W, COL_VR), _pcol(n, rb, GROUP_W, COL_GR)]
    args = [p, p, p]
    if sample:
        in_specs += [_layer_seqs(nseq, state, layer)]
        args += list(init)
    in_specs += [_layer_param((SUBLANES, CHUNK), layer), _layer_param((1, GROUP_W), layer)]
    args += [prm["lanes"], prm["ret_norm"]]
    return pl.pallas_call(
        functools.partial(_ret_kernel, nseq, nseq if sample else 1, rows, sample, grid[1], first_lo, hi),
        grid=grid, in_specs=in_specs,
        out_specs=[pl.BlockSpec((n, GROUP_W), lambda blk, c: (rb(blk, c), 0)), _seqs_out(nseq, state)],
        out_shape=[jax.ShapeDtypeStruct((total, GROUP_W), BF16),
                   jax.ShapeDtypeStruct((grid[0] * nseq,) + state, F32)],
        scratch_shapes=[pltpu.VMEM((nseq,) + state, F32)],
        compiler_params=_MIXER_PARAMS,
        name="ret_sample" if sample else "ret_prompt",
    )(*args)


def _gdn(p, prm, layer, init):
    sample = init is not None
    grid, nseq, rows, total, first_lo, hi = _geometry(sample)
    rb, n = _rowblk(sample), nseq * rows
    state = (GDN_HEADS, GDN_DK, GDN_DV)
    in_specs = [_pcol(n, rb, GROUP_W, COL_QD), _pcol(n, rb, GROUP_W, COL_KD), _pcol(n, rb, GROUP_W, COL_VD),
                _pcol(n, rb, GROUP_W, COL_GD), _pcol(n, rb, CHUNK, COL_SMALL)]
    args = [p, p, p, p, p]
    if sample:
        in_specs += [_layer_seqs(nseq, state, layer), _layer_seqs(nseq, (SUBLANES, GDN_CONV_W), layer)]
        args += list(init)
    in_specs += [_layer_param((CONV_K, GDN_CONV_W), layer), _layer_param((SUBLANES, CHUNK), layer),
                 _layer_param((1, GDN_DV), layer)]
    args += [prm["gdn_conv_w"], prm["lanes"], prm["gdn_norm"]]
    return pl.pallas_call(
        functools.partial(_gdn_kernel, nseq, 2 if sample else 1, 1, rows, sample,
                          grid[1], first_lo, hi),
        grid=grid, in_specs=in_specs,
        out_specs=[pl.BlockSpec((n, GROUP_W), lambda blk, c: (rb(blk, c), 0)), _seqs_out(nseq, state)],
        out_shape=[jax.ShapeDtypeStruct((total, GROUP_W), BF16),
                   jax.ShapeDtypeStruct((grid[0] * nseq,) + state, F32)],
        scratch_shapes=[pltpu.VMEM((nseq,) + state, F32), pltpu.VMEM((nseq, 2 * SUBLANES, GDN_CONV_W), F32)],
        compiler_params=_MIXER_PARAMS,
        name="gdn_sample" if sample else "gdn_prompt",
    )(*args)


def _swa(p, prm, layer, cache):
    sample = cache is not None
    grid, nseq, rows, total, _, _ = _geometry(sample)
    rb, n = _rowblk(sample), nseq * rows
    in_specs = [_pcol(n, rb, GROUP_W, COL_QA), _pcol(n, rb, GROUP_W, COL_GA), _pcol(n, rb, 2 * KV_W, COL_KV)]
    args = [p, p, p]
    if sample:
        in_specs += [_layer_seqs(nseq, (WINDOW, KV_W), layer), _layer_seqs(nseq, (WINDOW, KV_W), layer)]
        args += list(cache)
    else:
        in_specs += [pl.BlockSpec((n, 2 * KV_W), lambda blk, c: (jnp.maximum(c - 1, 0), COL_KV // (2 * KV_W)))]
        args += [p]
    in_specs += [_SMEM]
    args += [prm["swa_sinks"]]
    return pl.pallas_call(
        functools.partial(_swa_kernel, nseq, nseq if sample else 1, rows, layer, not sample),
        grid=grid, in_specs=in_specs,
        out_specs=pl.BlockSpec((n, GROUP_W), lambda blk, c: (rb(blk, c), 0)),
        out_shape=jax.ShapeDtypeStruct((total, GROUP_W), BF16),
        compiler_params=_MIXER_PARAMS,
        name="swa_sample" if sample else "swa_prompt",
    )(*args)


def _lane_table(ssd_dt_bias, ssd_a_log, gdn_dt_bias, gdn_a_log):
    def put(v, lane):
        return jnp.pad(v.astype(F32), ((0, 0), (lane, CHUNK - lane - v.shape[1])))
    log_gamma = jnp.log1p(-jnp.exp2(-5.0 - jnp.arange(RET_HEADS, dtype=F32)))
    rows = [put(ssd_dt_bias, LANE_DT), put(ssd_a_log, LANE_DT), put(gdn_dt_bias, LANE_DECAY),
            put(gdn_a_log, LANE_DECAY), put(jnp.broadcast_to(log_gamma, (DEPTH, RET_HEADS)), 0)]
    rows += [jnp.zeros((DEPTH, CHUNK), F32)] * (SUBLANES - len(rows))
    return jnp.stack(rows, axis=1)


def _conv_state_pad(state):
    return jnp.pad(state, ((0, 0), (0, 0), (SUBLANES - (CONV_K - 1), 0), (0, 0)))


def kernel(x_prompt, x_sample, state_ssd, state_ssd_conv, cache_swa_k, cache_swa_v, state_ret, state_gdn,
           state_gdn_conv, meta_tokens, pre_norm, post_norm, w_in, w_out, ssd_conv_w, ssd_conv_b, ssd_dt_bias,
           ssd_a_log, ssd_d, ssd_norm, swa_sinks, ret_norm, gdn_conv_w, gdn_dt_bias, gdn_a_log, gdn_norm):
    xs = jnp.pad(x_sample, ((0, 0), (0, SUBLANES - DEC_SEQ), (0, 0))).reshape(N_SAMPLE, D_MODEL)
    mask_np = np.ones((PROMPT_CHUNKS, BATCH, CHUNK), np.float32)
    mask_np[0, :, :PROMPT_PAD] = 0.0
    mask_p = jnp.asarray(mask_np.reshape(N_PROMPT, 1))
    mask_s = jnp.asarray(np.tile(np.arange(SUBLANES) < DEC_SEQ, DEC_BATCH)[:, None], F32)

    w_in_p = _wprep(jnp.swapaxes(w_in, 1, 2))
    w_out_b = w_out.astype(BF16)
    pre_g = pre_norm.reshape(DEPTH, 1, D_MODEL)
    post_g = post_norm.reshape(DEPTH, 1, D_MODEL)
    prm = dict(
        lanes=_lane_table(ssd_dt_bias, ssd_a_log, gdn_dt_bias, gdn_a_log),
        ssd_conv_w=ssd_conv_w, ssd_conv_b=ssd_conv_b.reshape(DEPTH, 1, SSD_CONV_W),
        ssd_norm=ssd_norm.reshape(DEPTH, 1, GROUP_W), ssd_d=ssd_d, swa_sinks=swa_sinks,
        ret_norm=ret_norm.reshape(DEPTH, 1, GROUP_W), gdn_conv_w=gdn_conv_w,
        gdn_norm=gdn_norm.reshape(DEPTH, 1, GDN_DV))
    ssd_c0 = _conv_state_pad(state_ssd_conv)
    gdn_c0 = _conv_state_pad(state_gdn_conv)
    cache_k = cache_swa_k.reshape(DEPTH, DEC_BATCH, WINDOW, KV_W)
    cache_v = cache_swa_v.reshape(DEPTH, DEC_BATCH, WINDOW, KV_W)

    names = ("ssd", "ssd_conv", "swa_k", "swa_v", "ret", "gdn", "gdn_conv")
    p_out = {k: [] for k in names}
    s_out = {k: [] for k in names}
    tail = slice(CHUNK - (CONV_K - 1), CHUNK)
    xp, hp = _embed_prompt(x_prompt, meta_tokens, pre_g)
    hs = _prenorm(xs, pre_g, 0, N_SAMPLE)
    for l in range(DEPTH):
        pp = _inproj(hp, w_in_p, l, PROMPT_TM)
        ya, p_ssd = _ssd(pp, prm, l, None)
        yb = _swa(pp, prm, l, None)
        yc, p_ret = _ret(pp, prm, l, None)
        yd, p_gdn = _gdn(pp, prm, l, None)
        xp, hp = _outproj((ya, yb, yc, yd), w_out_b, xp, post_g, mask_p, pre_g, l, PROMPT_TM,
                          by_sequence=l == DEPTH - 1)

        ps = _inproj(hs, w_in_p, l, N_SAMPLE)
        ya, s_ssd = _ssd(ps, prm, l, (state_ssd, ssd_c0))
        yb = _swa(ps, prm, l, (cache_k, cache_v))
        yc, s_ret = _ret(ps, prm, l, (state_ret,))
        yd, s_gdn = _gdn(ps, prm, l, (state_gdn, gdn_c0))
        xs, hs = _outproj((ya, yb, yc, yd), w_out_b, xs, post_g, mask_s, pre_g, l, N_SAMPLE)

        pp4 = pp.reshape(PROMPT_CHUNKS, BATCH, CHUNK, IN_W_PAD)

        def last_chunk(r0, c0, c1):
            return lax.slice(pp4, (PROMPT_CHUNKS - 1, 0, r0, c0), (PROMPT_CHUNKS, BATCH, CHUNK, c1))[0]

        ps3 = ps.reshape(DEC_BATCH, SUBLANES, IN_W_PAD)
        p_out["ssd"].append(p_ssd)
        p_out["ssd_conv"].append(last_chunk(tail.start, COL_X, COL_X + SSD_CONV_W))
        p_out["swa_k"].append(last_chunk(0, COL_KV, COL_KV + KV_W))
        p_out["swa_v"].append(last_chunk(0, COL_KV + KV_W, COL_KV + 2 * KV_W))
        p_out["ret"].append(p_ret)
        p_out["gdn"].append(p_gdn)
        p_out["gdn_conv"].append(last_chunk(tail.start, COL_QD, COL_QD + GDN_CONV_W))
        s_out["ssd"].append(s_ssd)
        s_out["ssd_conv"].append(ps3[:, 1:DEC_SEQ, COL_X:COL_X + SSD_CONV_W])
        s_out["swa_k"].append(ps3[:, :DEC_SEQ, COL_KV:COL_KV + KV_W])
        s_out["swa_v"].append(ps3[:, :DEC_SEQ, COL_KV + KV_W:COL_KV + 2 * KV_W])
        s_out["ret"].append(s_ret)
        s_out["gdn"].append(s_gdn)
        s_out["gdn_conv"].append(ps3[:, 1:DEC_SEQ, COL_QD:COL_QD + GDN_CONV_W])

    p_st = {k: jnp.stack(v) for k, v in p_out.items()}
    s_st = {k: jnp.stack(v) for k, v in s_out.items()}
    kv_shape = (SWA_KV_HEADS, SWA_HEAD_DIM)
    y_prompt = xp.reshape(BATCH, SEQ, D_MODEL)
    y_sample = xs.reshape(DEC_BATCH, SUBLANES, D_MODEL)[:, :DEC_SEQ]
    return (
        y_prompt, y_sample,
        p_st["ssd"], p_st["ssd_conv"],
        p_st["swa_k"].reshape((DEPTH, BATCH, WINDOW) + kv_shape),
        p_st["swa_v"].reshape((DEPTH, BATCH, WINDOW) + kv_shape),
        p_st["ret"], p_st["gdn"], p_st["gdn_conv"],
        s_st["ssd"], s_st["ssd_conv"],
        jnp.concatenate([cache_swa_k[:, :, DEC_SEQ:], s_st["swa_k"].reshape((DEPTH, DEC_BATCH, DEC_SEQ) + kv_shape)], axis=2),
        jnp.concatenate([cache_swa_v[:, :, DEC_SEQ:], s_st["swa_v"].reshape((DEPTH, DEC_BATCH, DEC_SEQ) + kv_shape)], axis=2),
        s_st["ret"], s_st["gdn"], s_st["gdn_conv"],
    )
```

```python
import functools

import jax
import jax.numpy as jnp
import numpy as np
from jax import lax
from jax.experimental import pallas as pl
from jax.experimental.pallas import tpu as pltpu

F32 = jnp.float32
BF16 = jnp.bfloat16
HIGHEST = lax.Precision.HIGHEST

D_MODEL = 2048
BATCH = 4
SEQ = 2048
DEPTH = 4
DEC_BATCH = 32
DEC_SEQ = 4
N_META = 16
GROUP_W = 512
CONV_K = 4
NORM_EPS = 1e-6
WINDOW = 128

SSD_HEADS, SSD_HEAD_DIM, SSD_GROUPS, SSD_STATE = 8, 64, 2, 128
SWA_HEADS, SWA_KV_HEADS, SWA_HEAD_DIM = 8, 2, 64
RET_HEADS, RET_DK, RET_DV = 4, 64, 128
GDN_HEADS, GDN_DK, GDN_DV = 4, 128, 128
SSD_CONV_W = GROUP_W + 2 * SSD_GROUPS * SSD_STATE
GDN_CONV_W = 3 * GROUP_W
KV_W = SWA_KV_HEADS * SWA_HEAD_DIM

CHUNK = 128
SUBLANES = 8
PROMPT_PAD = CHUNK - N_META
PROMPT_ROWS = PROMPT_PAD + N_META + SEQ
PROMPT_CHUNKS = PROMPT_ROWS // CHUNK
N_PROMPT = BATCH * PROMPT_ROWS
N_SAMPLE = DEC_BATCH * SUBLANES
SAMPLE_SEQS = 8

IN_W = 6416
IN_W_PAD = 6528
COL_Z, COL_X, COL_BC, COL_QA, COL_GA, COL_VR, COL_GR = 0, 512, 1024, 1536, 2048, 2560, 3072
COL_QD, COL_KD, COL_VD, COL_GD, COL_QKR, COL_KV, COL_SMALL = 3584, 4096, 4608, 5120, 5632, 6144, 6400
LANE_DT, LANE_BETA, LANE_DECAY = 0, 8, 12
ROW_SSD_DTB, ROW_SSD_ALOG, ROW_GDN_DTB, ROW_GDN_ALOG, ROW_LOG_GAMMA = 0, 1, 2, 3, 4

W_SEGMENTS = ((0, 1536), (1544, 2056), (2312, 2824), (3336, 3848), (3848, 4360), (4360, 5896), (5896, 6408),
              (2824, 3336), (2056, 2312))
W_BLOCKS = IN_W_PAD // CHUNK
W_PER_STEP = 3
W_STEPS = W_BLOCKS // W_PER_STEP
assert W_STEPS * W_PER_STEP == W_BLOCKS

PROJ_TN = 2176
PROMPT_TM = 512
VMEM_LIMIT = 48 * 1024 * 1024


def _sigmoid(x):
    return 1.0 / (1.0 + jnp.exp(-x))


def _silu(x):
    return x * _sigmoid(x)


def _softplus(x):
    return jnp.maximum(x, 0.0) + jnp.log1p(jnp.exp(-jnp.abs(x)))


def _dot(a, b):
    return jnp.dot(a.astype(BF16), b.astype(BF16), preferred_element_type=F32)


def _dot_nt(a, b):
    return lax.dot_general(a.astype(BF16), b.astype(BF16), (((1,), (1,)), ((), ())),
                           preferred_element_type=F32)


def _dot_tn(a, b):
    return lax.dot_general(a.astype(BF16), b.astype(BF16), (((0,), (0,)), ((), ())),
                           preferred_element_type=F32)


def _dot_f32(a, b):
    return jnp.dot(a, b, precision=HIGHEST, preferred_element_type=F32)


def _pad_rows(a, rows=CHUNK):
    if a.shape[0] == rows:
        return a
    return jnp.concatenate([a, jnp.zeros((rows - a.shape[0], a.shape[1]), a.dtype)], axis=0)


def _col(a, lane, width):
    return jnp.broadcast_to(a[:, lane:lane + 1], (a.shape[0], width))


def _iotas(m):
    row = lax.broadcasted_iota(jnp.int32, (m, CHUNK), 0)
    col = lax.broadcasted_iota(jnp.int32, (m, CHUNK), 1)
    return row, col


def _valid_rows(m, chunk_idx, first_lo, hi):
    r = lax.broadcasted_iota(jnp.int32, (m, 1), 0)
    lo = jnp.where(chunk_idx == 0, first_lo, 0)
    return (r >= lo) & (r < hi)


def _pack_lanes(per_seq):
    stride = CHUNK // len(per_seq)
    out = per_seq[0]
    for b in range(1, len(per_seq)):
        out = out + pltpu.roll(per_seq[b], b * stride, 1)
    return out, stride


def _decay_terms(la):
    m = la.shape[0]
    row, col = _iotas(m)
    lower = (row >= col).astype(F32)
    cum = _dot_f32(lower, _pad_rows(la))
    last = cum[m - 1:m, :]
    return cum, _pad_rows(cum).T, jnp.exp(cum), jnp.exp(last - cum), jnp.exp(last)


def _decay_matrix(cum, cum_t, lane):
    m = cum.shape[0]
    row, col = _iotas(m)
    keep = row >= col
    seg = _col(cum, lane, CHUNK) - jnp.broadcast_to(cum_t[lane:lane + 1, :], (m, CHUNK))
    return jnp.where(keep, jnp.exp(jnp.where(keep, seg, 0.0)), 0.0)


def _shift_rows_matrix(rows):
    r = lax.broadcasted_iota(jnp.int32, (rows, rows), 0)
    c = lax.broadcasted_iota(jnp.int32, (rows, rows), 1)
    blocks = [jnp.where(r - c == CONV_K - 1 - j, 1.0, 0.0) for j in range(CONV_K - 1)]
    return jnp.concatenate(blocks, axis=0).astype(BF16)


def _causal_conv(ext_ref, cur, w_ref, shift_mat):
    rows = cur.shape[0]
    w_last = w_ref[CONV_K - 1:CONV_K, :]
    ext_ref[SUBLANES:2 * SUBLANES, :] = cur[0:SUBLANES]
    head = cur[0:SUBLANES] * w_last
    for j in range(CONV_K - 1):
        start = SUBLANES - (CONV_K - 1) + j
        head = head + ext_ref[start:start + SUBLANES, :] * w_ref[j:j + 1, :]
    ext_ref[0:SUBLANES, :] = cur[rows - SUBLANES:rows]
    if rows == SUBLANES:
        return head
    shifted = jnp.dot(shift_mat, cur.astype(BF16), preferred_element_type=F32)
    acc = cur * w_last
    for j in range(CONV_K - 1):
        acc = acc + shifted[j * rows:(j + 1) * rows] * w_ref[j:j + 1, :]
    return jnp.concatenate([head, acc[SUBLANES:]], axis=0)


def _rms(x, w):
    ms = jnp.mean(x * x, axis=-1, keepdims=True)
    return x * lax.rsqrt(ms + NORM_EPS) * w


def _prenorm_kernel(x_ref, g_ref, h_ref):
    h_ref[...] = _rms(x_ref[...], g_ref[...]).astype(BF16)


def _prenorm(x, g_all, layer, tm):
    n = x.shape[0]
    return pl.pallas_call(
        _prenorm_kernel,
        grid=(n // tm,),
        in_specs=[pl.BlockSpec((tm, D_MODEL), lambda i: (i, 0)),
                  pl.BlockSpec((None, 1, D_MODEL), lambda i: (layer, 0, 0))],
        out_specs=pl.BlockSpec((tm, D_MODEL), lambda i: (i, 0)),
        out_shape=jax.ShapeDtypeStruct((n, D_MODEL), BF16),
        compiler_params=pltpu.CompilerParams(dimension_semantics=("arbitrary",)),
        name="prenorm",
    )(x, g_all)


def _wprep_kernel(offs_ref, a0_ref, a1_ref, a2_ref, b_ref, o_ref):
    j = pl.program_id(1)
    o_ref[0:CHUNK, :] = a0_ref[...].astype(BF16)
    o_ref[CHUNK:2 * CHUNK, :] = a1_ref[...].astype(BF16)

    @pl.when(j < W_STEPS - 1)
    def _():
        o_ref[2 * CHUNK:3 * CHUNK, :] = a2_ref[...].astype(BF16)

    @pl.when(j == W_STEPS - 1)
    def _():
        small = jnp.concatenate([a2_ref[0:SUBLANES, :], b_ref[...],
                                 jnp.zeros((CHUNK - 2 * SUBLANES, D_MODEL), F32)], axis=0)
        o_ref[2 * CHUNK:3 * CHUNK, :] = small.astype(BF16)


def _wprep(w_t):
    offs = np.concatenate([np.arange(a, b, CHUNK) for a, b in W_SEGMENTS] + [[1536]]).astype(np.int32)
    assert offs.shape[0] == W_BLOCKS and not (offs % SUBLANES).any()
    offs = offs // SUBLANES
    def source_block(k):
        return pl.BlockSpec(
            (pl.Element(CHUNK), pl.Element(D_MODEL)),
            lambda l, j, offs: ((l * (IN_W // SUBLANES) + offs[W_PER_STEP * j + k]) * SUBLANES, 0))

    grid_spec = pltpu.PrefetchScalarGridSpec(
        num_scalar_prefetch=1,
        grid=(DEPTH, W_STEPS),
        in_specs=[source_block(k) for k in range(W_PER_STEP)]
        + [pl.BlockSpec((SUBLANES, D_MODEL), lambda l, j, offs: ((l * IN_W + 6408) // SUBLANES, 0))],
        out_specs=pl.BlockSpec((None, W_PER_STEP * CHUNK, D_MODEL), lambda l, j, offs: (l, j, 0)),
    )
    w_rows = w_t.reshape(DEPTH * IN_W, D_MODEL)
    return pl.pallas_call(
        _wprep_kernel, grid_spec=grid_spec,
        out_shape=jax.ShapeDtypeStruct((DEPTH, IN_W_PAD, D_MODEL), BF16),
        compiler_params=pltpu.CompilerParams(dimension_semantics=("arbitrary", "arbitrary")),
        name="wprep",
    )(jnp.asarray(offs), w_rows, w_rows, w_rows, w_rows)


def _inproj_kernel(h_ref, w_ref, o_ref):
    o_ref[...] = lax.dot_general(h_ref[...], w_ref[...], (((1,), (1,)), ((), ())),
                                 preferred_element_type=F32)


def _inproj(h, w_all, layer, tm):
    n = h.shape[0]
    return pl.pallas_call(
        _inproj_kernel,
        grid=(IN_W_PAD // PROJ_TN, n // tm),
        in_specs=[
            pl.BlockSpec((tm, D_MODEL), lambda j, i: (i, 0)),
            pl.BlockSpec((None, PROJ_TN, D_MODEL), lambda j, i: (layer, j, 0)),
        ],
        out_specs=pl.BlockSpec((tm, PROJ_TN), lambda j, i: (i, j)),
        out_shape=jax.ShapeDtypeStruct((n, IN_W_PAD), F32),
        compiler_params=pltpu.CompilerParams(
            dimension_semantics=("arbitrary", "arbitrary"), vmem_limit_bytes=VMEM_LIMIT),
        name="inproj",
    )(h, w_all)


def _embed_prompt_kernel(x_ref, meta_ref, g_ref, o_ref, h_ref):
    c = pl.program_id(0)

    @pl.when(c == 0)
    def _():
        head = jnp.concatenate([jnp.zeros((PROMPT_PAD, D_MODEL), F32), meta_ref[...]], axis=0)
        for b in range(BATCH):
            o_ref[b * CHUNK:(b + 1) * CHUNK, :] = head

    @pl.when(c > 0)
    def _():
        for b in range(BATCH):
            o_ref[b * CHUNK:(b + 1) * CHUNK, :] = x_ref[b]

    h_ref[...] = _rms(o_ref[...], g_ref[...]).astype(BF16)


def _embed_prompt(x_prompt, meta_tokens, g_all):
    x4 = x_prompt.reshape(BATCH, SEQ // CHUNK, CHUNK, D_MODEL)
    row = pl.BlockSpec((BATCH * CHUNK, D_MODEL), lambda c: (c, 0))
    return pl.pallas_call(
        _embed_prompt_kernel,
        grid=(PROMPT_CHUNKS,),
        in_specs=[pl.BlockSpec((BATCH, None, CHUNK, D_MODEL), lambda c: (0, jnp.maximum(c - 1, 0), 0, 0)),
                  pl.BlockSpec((N_META, D_MODEL), lambda c: (0, 0)),
                  pl.BlockSpec((None, 1, D_MODEL), lambda c: (0, 0, 0))],
        out_specs=[row, row],
        out_shape=[jax.ShapeDtypeStruct((N_PROMPT, D_MODEL), F32),
                   jax.ShapeDtypeStruct((N_PROMPT, D_MODEL), BF16)],
        compiler_params=pltpu.CompilerParams(dimension_semantics=("arbitrary",), vmem_limit_bytes=VMEM_LIMIT),
        name="embed_prompt",
    )(x4, meta_tokens, g_all)


def _outproj_kernel(with_next, by_sequence, *refs):
    if with_next:
        ya_ref, yb_ref, yc_ref, yd_ref, w_ref, x_ref, g_ref, m_ref, gn_ref, o_ref, h_ref = refs
    else:
        ya_ref, yb_ref, yc_ref, yd_ref, w_ref, x_ref, g_ref, m_ref, o_ref = refs
    acc = None
    for g, y_ref in enumerate((ya_ref, yb_ref, yc_ref, yd_ref)):
        part = jnp.dot(y_ref[...], w_ref[g * GROUP_W:(g + 1) * GROUP_W, :], preferred_element_type=F32)
        acc = part if acc is None else acc + part
    x_new = jnp.where(m_ref[...] > 0.0, x_ref[...] + _rms(acc, g_ref[...]), 0.0)
    if by_sequence:
        for b in range(BATCH):
            o_ref[b] = x_new[b * CHUNK:(b + 1) * CHUNK]
    else:
        o_ref[...] = x_new
    if with_next:
        h_ref[...] = _rms(x_new, gn_ref[...]).astype(BF16)


def _outproj(ys, w_all, x, post_all, rowmask, pre_all, layer, tm, by_sequence=False):
    n = x.shape[0]
    with_next = layer + 1 < DEPTH
    yspec = pl.BlockSpec((tm, GROUP_W), lambda i: (i, 0))
    row = pl.BlockSpec((tm, D_MODEL), lambda i: (i, 0))
    in_specs = [yspec, yspec, yspec, yspec,
                pl.BlockSpec((None, D_MODEL, D_MODEL), lambda i: (layer, 0, 0)),
                row,
                pl.BlockSpec((None, 1, D_MODEL), lambda i: (layer, 0, 0)),
                pl.BlockSpec((tm, 1), lambda i: (i, 0))]
    args = list(ys) + [w_all, x, post_all, rowmask]
    if by_sequence:
        assert tm == BATCH * CHUNK and not with_next
        out_specs = [pl.BlockSpec((BATCH, None, CHUNK, D_MODEL), lambda i: (0, jnp.maximum(i - 1, 0), 0, 0))]
        out_shape = [jax.ShapeDtypeStruct((BATCH, SEQ // CHUNK, CHUNK, D_MODEL), F32)]
    else:
        out_specs = [row]
        out_shape = [jax.ShapeDtypeStruct((n, D_MODEL), F32)]
    if with_next:
        in_specs.append(pl.BlockSpec((None, 1, D_MODEL), lambda i: (layer + 1, 0, 0)))
        args.append(pre_all)
        out_specs.append(row)
        out_shape.append(jax.ShapeDtypeStruct((n, D_MODEL), BF16))
    res = pl.pallas_call(
        functools.partial(_outproj_kernel, with_next, by_sequence),
        grid=(n // tm,), in_specs=in_specs, out_specs=out_specs, out_shape=out_shape,
        compiler_params=pltpu.CompilerParams(
            dimension_semantics=("arbitrary",), vmem_limit_bytes=VMEM_LIMIT),
        name="outproj",
    )(*args)
    return (res[0], res[1]) if with_next else (res[0], None)


def _init_state(c, has_init, s_scr, s0_ref, ext=None, c0_ref=None):
    @pl.when(c == 0)
    def _():
        if has_init:
            s_scr[...] = s0_ref[...]
            if ext is not None:
                ext[:, 0:SUBLANES, :] = c0_ref[...]
        else:
            s_scr[...] = jnp.zeros_like(s_scr)
            if ext is not None:
                ext[:, 0:SUBLANES, :] = jnp.zeros((ext.shape[0], SUBLANES, ext.shape[2]), F32)


def _run_staggered(groups, skew):
    pending, live, tick = list(groups), [], 0
    while pending or live:
        if pending and tick % skew == 0:
            live.append(pending.pop(0))
        for gen in list(live):
            if next(gen, "done") == "done":
                live.remove(gen)
        tick += 1


def _ssd_kernel(nseq, rows, layer, has_init, state_t, nchunks, first_lo, hi, *refs):
    if has_init:
        (z_ref, x_ref, bc_ref, sm_ref, s0_ref, c0_ref, _, cw_ref, cb_ref, lp_ref, nw_ref, d_ref,
         y_ref, so_ref, s_scr, ext) = refs
    else:
        (z_ref, x_ref, bc_ref, sm_ref, cw_ref, cb_ref, lp_ref, nw_ref, d_ref,
         y_ref, so_ref, s_scr, ext) = refs
        s0_ref = c0_ref = None
    c = pl.program_id(1)
    m = rows
    _init_state(c, has_init, s_scr, s0_ref, ext, c0_ref)

    valid = _valid_rows(m, c, first_lo, hi)
    lane = lax.broadcasted_iota(jnp.int32, (1, CHUNK), 1)
    head_lane = (lane >= LANE_DT) & (lane < LANE_DT + SSD_HEADS)
    dt_all = _softplus(sm_ref[...] + lp_ref[ROW_SSD_DTB:ROW_SSD_DTB + 1, :])
    la_all = -jnp.exp(lp_ref[ROW_SSD_ALOG:ROW_SSD_ALOG + 1, :]) * dt_all
    dts = [jnp.where(valid, dt_all[b * m:(b + 1) * m], 0.0) for b in range(nseq)]
    la, stride = _pack_lanes([jnp.where(valid & head_lane, la_all[b * m:(b + 1) * m], 0.0) for b in range(nseq)])
    cum, cum_t, e_cum, e_rem, e_last = _decay_terms(la)

    shift_mat = _shift_rows_matrix(m) if m > SUBLANES else None
    hpg = SSD_HEADS // SSD_GROUPS
    c_off = GROUP_W + SSD_GROUPS * SSD_STATE
    gw = GROUP_W // SSD_GROUPS
    ys = [None] * nseq

    def sequence(b):
        rows_b = slice(b * m, (b + 1) * m)
        cur = jnp.concatenate([x_ref[rows_b, :], bc_ref[rows_b, :]], axis=1)
        a_b = _silu(_causal_conv(ext.at[b], cur, cw_ref, shift_mat) + cb_ref[...])
        yield
        scores = []
        for g in range(SSD_GROUPS):
            b_g = a_b[:, GROUP_W + g * SSD_STATE:GROUP_W + (g + 1) * SSD_STATE]
            c_g = a_b[:, c_off + g * SSD_STATE:c_off + (g + 1) * SSD_STATE]
            scores.append((_dot_nt(c_g, _pad_rows(b_g)), b_g, c_g))
        yield
        heads = []
        for h in range(SSD_HEADS):
            sc, b_g, c_g = scores[h // hpg]
            ln = b * stride + LANE_DT + h
            x_h = a_b[:, h * SSD_HEAD_DIM:(h + 1) * SSD_HEAD_DIM]
            v_h = x_h * _col(dts[b], LANE_DT + h, SSD_HEAD_DIM)
            s_h = s_scr[b, h]
            v_w = v_h * _col(e_rem, ln, SSD_HEAD_DIM)
            if state_t:
                from_state = _dot_nt(c_g, s_h)
                s_scr[b, h] = s_h * e_last[:, ln:ln + 1] + _dot_tn(v_w, b_g)
            else:
                from_state = _dot(c_g, s_h)
                s_scr[b, h] = s_h * e_last[:, ln:ln + 1] + _dot_tn(b_g, v_w)
            o = (_dot(sc * _decay_matrix(cum, cum_t, ln), _pad_rows(v_h))
                 + from_state * _col(e_cum, ln, SSD_HEAD_DIM))
            heads.append(o + d_ref[layer, h] * x_h)
            yield
        y = jnp.concatenate(heads, axis=1) * _silu(z_ref[rows_b, :])
        outs = [_rms(y[:, g * gw:(g + 1) * gw], nw_ref[:, g * gw:(g + 1) * gw]) for g in range(SSD_GROUPS)]
        ys[b] = jnp.concatenate(outs, axis=1)

    _run_staggered([sequence(b) for b in range(nseq)], 1)
    y_ref[...] = jnp.concatenate(ys, axis=0).astype(y_ref.dtype)

    @pl.when(c == nchunks - 1)
    def _():
        so_ref[...] = s_scr[...]


def _ret_kernel(nseq, seq_group, rows, has_init, nchunks, first_lo, hi, *refs):
    if has_init:
        qk_ref, v_ref, g_ref, s0_ref, _, lp_ref, nw_ref, y_ref, so_ref, s_scr = refs
    else:
        qk_ref, v_ref, g_ref, lp_ref, nw_ref, y_ref, so_ref, s_scr = refs
        s0_ref = None
    c = pl.program_id(1)
    m = rows
    _init_state(c, has_init, s_scr, s0_ref)

    valid = _valid_rows(m, c, first_lo, hi)
    lane = lax.broadcasted_iota(jnp.int32, (1, CHUNK), 1)
    log_gamma = jnp.broadcast_to(lp_ref[ROW_LOG_GAMMA:ROW_LOG_GAMMA + 1, :], (m, CHUNK))
    la_one = jnp.where(valid & (lane < RET_HEADS), log_gamma, 0.0)
    la, stride = _pack_lanes([la_one] * nseq)
    cum, cum_t, e_cum, e_rem, e_last = _decay_terms(la)

    qk_all = qk_ref[...]
    v_all = v_ref[...]
    all_units = []
    for b in range(nseq):
        qk = qk_all[b * m:(b + 1) * m]
        v = jnp.where(valid, v_all[b * m:(b + 1) * m], 0.0)
        for h in range(RET_HEADS):
            q_h = qk[:, h * RET_DK:(h + 1) * RET_DK]
            k_h = qk[:, (RET_HEADS + h) * RET_DK:(RET_HEADS + h + 1) * RET_DK] * (RET_DK ** -0.5)
            s_h = s_scr[b, h]
            all_units.append(dict(b=b, h=h, ln=b * stride + h, k=k_h, v=v[:, h * RET_DV:(h + 1) * RET_DV],
                                  s=s_h, raw=_dot_nt(q_h, _pad_rows(k_h)), q_s=_dot(q_h, s_h)))
    ys = [None] * nseq

    def group(b0):
        units = all_units[b0 * RET_HEADS:(b0 + seq_group) * RET_HEADS]
        scs = [u["raw"] * _decay_matrix(cum, cum_t, u["ln"]) for u in units]
        yield
        os_ = [_dot(sc, _pad_rows(u["v"])) + u["q_s"] * _col(e_cum, u["ln"], RET_DV)
               for sc, u in zip(scs, units)]
        yield
        for u in units:
            ln = u["ln"]
            s_scr[u["b"], u["h"]] = (u["s"] * e_last[:, ln:ln + 1]
                                     + _dot_tn(u["k"] * _col(e_rem, ln, RET_DK), u["v"]))
        yield
        for i in range(seq_group):
            heads = [_rms(os_[i * RET_HEADS + h], nw_ref[:, h * RET_DV:(h + 1) * RET_DV])
                     for h in range(RET_HEADS)]
            rows_b = slice((b0 + i) * m, (b0 + i + 1) * m)
            ys[b0 + i] = jnp.concatenate(heads, axis=1) * _silu(g_ref[rows_b, :])

    _run_staggered([group(b0) for b0 in range(0, nseq, seq_group)], 1)
    y_ref[...] = jnp.concatenate(ys, axis=0).astype(y_ref.dtype)

    @pl.when(c == nchunks - 1)
    def _():
        so_ref[...] = s_scr[...]


def _gdn_kernel(nseq, seq_group, skew, rows, has_init, nchunks, first_lo, hi, *refs):
    if has_init:
        (q_ref, k_ref, v_ref, g_ref, sm_ref, s0_ref, c0_ref, _, cw_ref, lp_ref, nw_ref,
         y_ref, so_ref, s_scr, ext) = refs
    else:
        (q_ref, k_ref, v_ref, g_ref, sm_ref, cw_ref, lp_ref, nw_ref,
         y_ref, so_ref, s_scr, ext) = refs
        s0_ref = c0_ref = None
    c = pl.program_id(1)
    m = rows
    _init_state(c, has_init, s_scr, s0_ref, ext, c0_ref)

    shift_mat = _shift_rows_matrix(m) if m > SUBLANES else None
    valid = _valid_rows(m, c, first_lo, hi)
    lane = lax.broadcasted_iota(jnp.int32, (1, CHUNK), 1)
    head_lane = (lane >= LANE_DECAY) & (lane < LANE_DECAY + GDN_HEADS)
    sm = sm_ref[...]
    beta_all = _sigmoid(sm)
    la_all = (-jnp.exp(lp_ref[ROW_GDN_ALOG:ROW_GDN_ALOG + 1, :])
              * _softplus(sm + lp_ref[ROW_GDN_DTB:ROW_GDN_DTB + 1, :]))
    betas = [jnp.where(valid, beta_all[b * m:(b + 1) * m], 0.0) for b in range(nseq)]
    la, stride = _pack_lanes([jnp.where(valid & head_lane, la_all[b * m:(b + 1) * m], 0.0) for b in range(nseq)])
    cum, cum_t, e_cum, e_rem, e_last = _decay_terms(la)
    row, col = _iotas(m)
    levels = max(1, (min(m, hi) - 1).bit_length())

    ys = [None] * nseq

    def joining_mask(shift):
        same_big = lax.shift_right_logical(row, shift + 1) == lax.shift_right_logical(col, shift + 1)
        same_small = lax.shift_right_logical(row, shift) == lax.shift_right_logical(col, shift)
        return same_big & jnp.logical_not(same_small) & (row > col)

    def group(b0):
        units = []
        for b in range(b0, b0 + seq_group):
            rows_b = slice(b * m, (b + 1) * m)
            cur = jnp.concatenate([q_ref[rows_b, :], k_ref[rows_b, :], v_ref[rows_b, :]], axis=1)
            a_b = _silu(_causal_conv(ext.at[b], cur, cw_ref, shift_mat))
            yield
            for h in range(GDN_HEADS):
                ln = b * stride + LANE_DECAY + h
                q_h = a_b[:, h * GDN_DK:(h + 1) * GDN_DK]
                k_h = a_b[:, GROUP_W + h * GDN_DK:GROUP_W + (h + 1) * GDN_DK]
                v_h = a_b[:, 2 * GROUP_W + h * GDN_DV:2 * GROUP_W + (h + 1) * GDN_DV]
                q_h = q_h * lax.rsqrt(jnp.sum(q_h * q_h, axis=-1, keepdims=True) + 1e-6) * (GDN_DK ** -0.5)
                k_h = k_h * lax.rsqrt(jnp.sum(k_h * k_h, axis=-1, keepdims=True) + 1e-6)
                k_h = jnp.where(valid, k_h, 0.0)
                b_h = _col(betas[b], LANE_BETA + h, GDN_DK)
                k_b = k_h * b_h
                decay = _decay_matrix(cum, cum_t, ln)
                k_pad = _pad_rows(k_h)
                units.append(dict(
                    b=b, h=h, ln=ln, q=q_h, k=k_h, k_b=k_b, decay=decay,
                    a=jnp.where(row > col, _dot_nt(k_b, k_pad) * decay, 0.0),
                    v_b=_pad_rows(jnp.where(valid, v_h, 0.0) * b_h),
                    qk=_dot_nt(q_h, k_pad) * decay))
                yield
        eye = (row == col).astype(F32)
        mask = joining_mask(0)
        ts = [eye - jnp.where(mask, u["a"], 0.0) for u in units]
        for shift in range(1, levels):
            mask = joining_mask(shift)
            inner = [_dot(jnp.where(mask, u["a"], 0.0), _pad_rows(t)) for u, t in zip(units, ts)]
            yield
            ts = [t - _dot(t, _pad_rows(x)) for t, x in zip(ts, inner)]
            yield
        us = [_dot(t, u["v_b"]) for t, u in zip(ts, units)]
        ws = [_dot(t, _pad_rows(u["k_b"] * _col(e_cum, u["ln"], GDN_DK))) for t, u in zip(ts, units)]
        yield
        ss = [s_scr[u["b"], u["h"]] for u in units]
        v_news = [x - _dot(w, s_h) for x, w, s_h in zip(us, ws, ss)]
        yield
        os_ = [_dot(u["q"] * _col(e_cum, u["ln"], GDN_DK), s_h) + _dot(u["qk"], _pad_rows(v_new))
               for u, s_h, v_new in zip(units, ss, v_news)]
        yield
        for u, s_h, v_new in zip(units, ss, v_news):
            ln = u["ln"]
            s_scr[u["b"], u["h"]] = (s_h * e_last[:, ln:ln + 1]
                                     + _dot_tn(u["k"] * _col(e_rem, ln, GDN_DK), v_new))
        yield
        for i in range(seq_group):
            heads = [_rms(o, nw_ref[...]) for o in os_[i * GDN_HEADS:(i + 1) * GDN_HEADS]]
            rows_b = slice((b0 + i) * m, (b0 + i + 1) * m)
            ys[b0 + i] = jnp.concatenate(heads, axis=1) * _silu(g_ref[rows_b, :])

    _run_staggered([group(b0) for b0 in range(0, nseq, seq_group)], skew)
    y_ref[...] = jnp.concatenate(ys, axis=0).astype(y_ref.dtype)

    @pl.when(c == nchunks - 1)
    def _():
        so_ref[...] = s_scr[...]


def _swa_kernel(nseq, seq_group, rows, layer, is_prompt, *refs):
    if is_prompt:
        q_ref, g_ref, kvc_ref, kvp_ref, sink_ref, y_ref = refs
    else:
        q_ref, g_ref, kvc_ref, kp_ref, vp_ref, sink_ref, y_ref = refs
    n = pl.program_id(1)
    m = rows
    if is_prompt:
        lo_prev = jnp.where(n == 0, CHUNK, jnp.where(n == 1, PROMPT_PAD, 0))
        lo_cur = jnp.where(n == 0, PROMPT_PAD, 0)
    else:
        lo_prev, lo_cur = 0, 0
    qi = lax.broadcasted_iota(jnp.int32, (m, 2 * CHUNK), 0)
    kj = lax.broadcasted_iota(jnp.int32, (m, 2 * CHUNK), 1)
    dist = qi + WINDOW - kj
    key_ok = ((kj < CHUNK) & (kj >= lo_prev)) | (kj >= CHUNK + lo_cur)
    visible = (dist >= 0) & (dist <= WINDOW) & key_ok
    dist_f = dist.astype(F32)

    q_all = q_ref[...]
    kvc_all = kvc_ref[...]
    grp = SWA_HEADS // SWA_KV_HEADS
    ys = []
    for b0 in range(0, nseq, seq_group):
        units = []
        for b in range(b0, b0 + seq_group):
            q = q_all[b * m:(b + 1) * m]
            kvc = _pad_rows(kvc_all[b * m:(b + 1) * m])
            if is_prompt:
                k_prev = kvp_ref[b * CHUNK:(b + 1) * CHUNK, 0:KV_W]
                v_prev = kvp_ref[b * CHUNK:(b + 1) * CHUNK, KV_W:2 * KV_W]
            else:
                k_prev, v_prev = kp_ref[b], vp_ref[b]
            k_all = jnp.concatenate([k_prev, kvc[:, 0:KV_W]], axis=0)
            v_all = jnp.concatenate([v_prev, kvc[:, KV_W:2 * KV_W]], axis=0)
            for h in range(SWA_HEADS):
                kvh = h // grp
                units.append(dict(h=h, q=q[:, h * SWA_HEAD_DIM:(h + 1) * SWA_HEAD_DIM],
                                  k=k_all[:, kvh * SWA_HEAD_DIM:(kvh + 1) * SWA_HEAD_DIM],
                                  v=v_all[:, kvh * SWA_HEAD_DIM:(kvh + 1) * SWA_HEAD_DIM]))
        ss = [jnp.where(visible,
                        _dot_nt(u["q"], u["k"]) * (SWA_HEAD_DIM ** -0.5)
                        - 2.0 ** (-8.0 * (u["h"] + 1) / SWA_HEADS) * dist_f, -1e30) for u in units]
        mxs = [jnp.maximum(jnp.max(s, axis=-1, keepdims=True), sink_ref[layer, u["h"]])
               for s, u in zip(ss, units)]
        es = [jnp.exp(s - mx) for s, mx in zip(ss, mxs)]
        dens = [jnp.sum(e, axis=-1, keepdims=True) + jnp.exp(sink_ref[layer, u["h"]] - mx)
                for e, mx, u in zip(es, mxs, units)]
        os_ = [_dot(e, u["v"]) / den for e, u, den in zip(es, units, dens)]
        for i in range(seq_group):
            ys.append(jnp.concatenate(os_[i * SWA_HEADS:(i + 1) * SWA_HEADS], axis=1))
    y_ref[...] = (jnp.concatenate(ys, axis=0) * _silu(g_ref[...])).astype(y_ref.dtype)


def _geometry(sample):
    if sample:
        return (DEC_BATCH // SAMPLE_SEQS, 1), SAMPLE_SEQS, SUBLANES, N_SAMPLE, 0, DEC_SEQ
    return (1, PROMPT_CHUNKS), BATCH, CHUNK, N_PROMPT, PROMPT_PAD, CHUNK


def _rowblk(sample):
    return (lambda blk, c: blk) if sample else (lambda blk, c: c)


def _pcol(nrows, rowblk, width, col):
    assert col % width == 0
    return pl.BlockSpec((nrows, width), lambda blk, c: (rowblk(blk, c), col // width))


def _layer_param(shape, layer):
    return pl.BlockSpec((None,) + shape, lambda blk, c: (layer,) + (0,) * len(shape))


def _layer_seqs(nseq, shape, layer):
    return pl.BlockSpec((None, nseq) + shape, lambda blk, c: (layer, blk) + (0,) * len(shape))


def _seqs_out(nseq, shape):
    return pl.BlockSpec((nseq,) + shape, lambda blk, c: (blk,) + (0,) * len(shape))


_SMEM = pl.BlockSpec(memory_space=pltpu.SMEM)
_MIXER_PARAMS = pltpu.CompilerParams(dimension_semantics=("arbitrary", "arbitrary"),
                                     vmem_limit_bytes=VMEM_LIMIT)


def _state_output(sample, nseq, state, layer, acc, in_specs, args):
    if not sample:
        return _seqs_out(nseq, state), jax.ShapeDtypeStruct((nseq,) + state, F32), {}
    in_specs.append(pl.BlockSpec(memory_space=pl.ANY))
    args.append(acc)
    spec = pl.BlockSpec((None, nseq) + state, lambda blk, c: (layer, blk) + (0,) * len(state))
    return spec, jax.ShapeDtypeStruct(acc.shape, F32), {len(args) - 1: 1}


def _ssd(p, prm, layer, init, acc=None):
    sample = init is not None
    grid, nseq, rows, total, first_lo, hi = _geometry(sample)
    rb, n = _rowblk(sample), nseq * rows
    state = (SSD_HEADS, SSD_HEAD_DIM, SSD_STATE) if sample else (SSD_HEADS, SSD_STATE, SSD_HEAD_DIM)
    in_specs = [_pcol(n, rb, GROUP_W, COL_Z), _pcol(n, rb, GROUP_W, COL_X),
                _pcol(n, rb, GROUP_W, COL_BC), _pcol(n, rb, CHUNK, COL_SMALL)]
    args = [p, p, p, p]
    if sample:
        in_specs += [_layer_seqs(nseq, state, layer), _layer_seqs(nseq, (SUBLANES, SSD_CONV_W), layer)]
        args += list(init)
    state_spec, state_shape, aliases = _state_output(sample, nseq, state, layer, acc, in_specs, args)
    in_specs += [_layer_param((CONV_K, SSD_CONV_W), layer), _layer_param((1, SSD_CONV_W), layer),
                 _layer_param((SUBLANES, CHUNK), layer), _layer_param((1, GROUP_W), layer), _SMEM]
    args += [prm["ssd_conv_w"], prm["ssd_conv_b"], prm["lanes"], prm["ssd_norm"], prm["ssd_d"]]
    return pl.pallas_call(
        functools.partial(_ssd_kernel, nseq, rows, layer, sample, sample, grid[1], first_lo, hi),
        grid=grid, in_specs=in_specs,
        out_specs=[pl.BlockSpec((n, GROUP_W), lambda blk, c: (rb(blk, c), 0)), state_spec],
        out_shape=[jax.ShapeDtypeStruct((total, GROUP_W), BF16), state_shape],
        input_output_aliases=aliases,
        scratch_shapes=[pltpu.VMEM((nseq,) + state, F32), pltpu.VMEM((nseq, 2 * SUBLANES, SSD_CONV_W), F32)],
        compiler_params=_MIXER_PARAMS,
        name="ssd_sample" if sample else "ssd_prompt",
    )(*args)


def _ret(p, prm, layer, init, acc=None):
    sample = init is not None
    grid, nseq, rows, total, first_lo, hi = _geometry(sample)
    rb, n = _rowblk(sample), nseq * rows
    state = (RET_HEADS, RET_DK, RET_DV)
    in_specs = [_pcol(n, rb, GROUP_W, COL_QKR), _pcol(n, rb, GROUP_W, COL_VR), _pcol(n, rb, GROUP_W, COL_GR)]
    args = [p, p, p]
    if sample:
        in_specs += [_layer_seqs(nseq, state, layer)]
        args += list(init)
    state_spec, state_shape, aliases = _state_output(sample, nseq, state, layer, acc, in_specs, args)
    in_specs += [_layer_param((SUBLANES, CHUNK), layer), _layer_param((1, GROUP_W), layer)]
    args += [prm["lanes"], prm["ret_norm"]]
    return pl.pallas_call(
        functools.partial(_ret_kernel, nseq, nseq if sample else 1, rows, sample, grid[1], first_lo, hi),
        grid=grid, in_specs=in_specs,
        out_specs=[pl.BlockSpec((n, GROUP_W), lambda blk, c: (rb(blk, c), 0)), state_spec],
        out_shape=[jax.ShapeDtypeStruct((total, GROUP_W), BF16), state_shape],
        input_output_aliases=aliases,
        scratch_shapes=[pltpu.VMEM((nseq,) + state, F32)],
        compiler_params=_MIXER_PARAMS,
        name="ret_sample" if sample else "ret_prompt",
    )(*args)


def _gdn(p, prm, layer, init, acc=None):
    sample = init is not None
    grid, nseq, rows, total, first_lo, hi = _geometry(sample)
    rb, n = _rowblk(sample), nseq * rows
    state = (GDN_HEADS, GDN_DK, GDN_DV)
    in_specs = [_pcol(n, rb, GROUP_W, COL_QD), _pcol(n, rb, GROUP_W, COL_KD), _pcol(n, rb, GROUP_W, COL_VD),
                _pcol(n, rb, GROUP_W, COL_GD), _pcol(n, rb, CHUNK, COL_SMALL)]
    args = [p, p, p, p, p]
    if sample:
        in_specs += [_layer_seqs(nseq, state, layer), _layer_seqs(nseq, (SUBLANES, GDN_CONV_W), layer)]
        args += list(init)
    state_spec, state_shape, aliases = _state_output(sample, nseq, state, layer, acc, in_specs, args)
    in_specs += [_layer_param((CONV_K, GDN_CONV_W), layer), _layer_param((SUBLANES, CHUNK), layer),
                 _layer_param((1, GDN_DV), layer)]
    args += [prm["gdn_conv_w"], prm["lanes"], prm["gdn_norm"]]
    return pl.pallas_call(
        functools.partial(_gdn_kernel, nseq, 2 if sample else 1, 1, rows, sample,
                          grid[1], first_lo, hi),
        grid=grid, in_specs=in_specs,
        out_specs=[pl.BlockSpec((n, GROUP_W), lambda blk, c: (rb(blk, c), 0)), state_spec],
        out_shape=[jax.ShapeDtypeStruct((total, GROUP_W), BF16), state_shape],
        input_output_aliases=aliases,
        scratch_shapes=[pltpu.VMEM((nseq,) + state, F32), pltpu.VMEM((nseq, 2 * SUBLANES, GDN_CONV_W), F32)],
        compiler_params=_MIXER_PARAMS,
        name="gdn_sample" if sample else "gdn_prompt",
    )(*args)


def _swa(p, prm, layer, cache):
    sample = cache is not None
    grid, nseq, rows, total, _, _ = _geometry(sample)
    rb, n = _rowblk(sample), nseq * rows
    in_specs = [_pcol(n, rb, GROUP_W, COL_QA), _pcol(n, rb, GROUP_W, COL_GA), _pcol(n, rb, 2 * KV_W, COL_KV)]
    args = [p, p, p]
    if sample:
        in_specs += [_layer_seqs(nseq, (WINDOW, KV_W), layer), _layer_seqs(nseq, (WINDOW, KV_W), layer)]
        args += list(cache)
    else:
        in_specs += [pl.BlockSpec((n, 2 * KV_W), lambda blk, c: (jnp.maximum(c - 1, 0), COL_KV // (2 * KV_W)))]
        args += [p]
    in_specs += [_SMEM]
    args += [prm["swa_sinks"]]
    return pl.pallas_call(
        functools.partial(_swa_kernel, nseq, nseq if sample else 1, rows, layer, not sample),
        grid=grid, in_specs=in_specs,
        out_specs=pl.BlockSpec((n, GROUP_W), lambda blk, c: (rb(blk, c), 0)),
        out_shape=jax.ShapeDtypeStruct((total, GROUP_W), BF16),
        compiler_params=_MIXER_PARAMS,
        name="swa_sample" if sample else "swa_prompt",
    )(*args)


def _lane_table(ssd_dt_bias, ssd_a_log, gdn_dt_bias, gdn_a_log):
    def put(v, lane):
        return jnp.pad(v.astype(F32), ((0, 0), (lane, CHUNK - lane - v.shape[1])))
    log_gamma = jnp.log1p(-jnp.exp2(-5.0 - jnp.arange(RET_HEADS, dtype=F32)))
    rows = [put(ssd_dt_bias, LANE_DT), put(ssd_a_log, LANE_DT), put(gdn_dt_bias, LANE_DECAY),
            put(gdn_a_log, LANE_DECAY), put(jnp.broadcast_to(log_gamma, (DEPTH, RET_HEADS)), 0)]
    rows += [jnp.zeros((DEPTH, CHUNK), F32)] * (SUBLANES - len(rows))
    return jnp.stack(rows, axis=1)


def _conv_state_pad(state):
    return jnp.pad(state, ((0, 0), (0, 0), (SUBLANES - (CONV_K - 1), 0), (0, 0)))


def kernel(x_prompt, x_sample, state_ssd, state_ssd_conv, cache_swa_k, cache_swa_v, state_ret, state_gdn,
           state_gdn_conv, meta_tokens, pre_norm, post_norm, w_in, w_out, ssd_conv_w, ssd_conv_b, ssd_dt_bias,
           ssd_a_log, ssd_d, ssd_norm, swa_sinks, ret_norm, gdn_conv_w, gdn_dt_bias, gdn_a_log, gdn_norm):
    xs = jnp.pad(x_sample, ((0, 0), (0, SUBLANES - DEC_SEQ), (0, 0))).reshape(N_SAMPLE, D_MODEL)
    mask_np = np.ones((PROMPT_CHUNKS, BATCH, CHUNK), np.float32)
    mask_np[0, :, :PROMPT_PAD] = 0.0
    mask_p = jnp.asarray(mask_np.reshape(N_PROMPT, 1))
    mask_s = jnp.asarray(np.tile(np.arange(SUBLANES) < DEC_SEQ, DEC_BATCH)[:, None], F32)

    w_in_p = _wprep(jnp.swapaxes(w_in, 1, 2))
    w_out_b = w_out.astype(BF16)
    pre_g = pre_norm.reshape(DEPTH, 1, D_MODEL)
    post_g = post_norm.reshape(DEPTH, 1, D_MODEL)
    prm = dict(
        lanes=_lane_table(ssd_dt_bias, ssd_a_log, gdn_dt_bias, gdn_a_log),
        ssd_conv_w=ssd_conv_w, ssd_conv_b=ssd_conv_b.reshape(DEPTH, 1, SSD_CONV_W),
        ssd_norm=ssd_norm.reshape(DEPTH, 1, GROUP_W), ssd_d=ssd_d, swa_sinks=swa_sinks,
        ret_norm=ret_norm.reshape(DEPTH, 1, GROUP_W), gdn_conv_w=gdn_conv_w,
        gdn_norm=gdn_norm.reshape(DEPTH, 1, GDN_DV))
    ssd_c0 = _conv_state_pad(state_ssd_conv)
    gdn_c0 = _conv_state_pad(state_gdn_conv)
    cache_k = cache_swa_k.reshape(DEPTH, DEC_BATCH, WINDOW, KV_W)
    cache_v = cache_swa_v.reshape(DEPTH, DEC_BATCH, WINDOW, KV_W)

    names = ("ssd", "ssd_conv", "swa_k", "swa_v", "ret", "gdn", "gdn_conv")
    p_out = {k: [] for k in names}
    s_out = {k: [] for k in ("ssd_conv", "swa_k", "swa_v", "gdn_conv")}
    tail = slice(CHUNK - (CONV_K - 1), CHUNK)
    state_ssd_t = jnp.swapaxes(state_ssd, 3, 4)
    s_ssd_t = jnp.zeros(state_ssd_t.shape, F32)
    s_ret = jnp.zeros(state_ret.shape, F32)
    s_gdn = jnp.zeros(state_gdn.shape, F32)
    xp, hp = _embed_prompt(x_prompt, meta_tokens, pre_g)
    hs = _prenorm(xs, pre_g, 0, N_SAMPLE)
    for l in range(DEPTH):
        pp = _inproj(hp, w_in_p, l, PROMPT_TM)
        ya, p_ssd = _ssd(pp, prm, l, None)
        yb = _swa(pp, prm, l, None)
        yc, p_ret = _ret(pp, prm, l, None)
        yd, p_gdn = _gdn(pp, prm, l, None)
        xp, hp = _outproj((ya, yb, yc, yd), w_out_b, xp, post_g, mask_p, pre_g, l, PROMPT_TM,
                          by_sequence=l == DEPTH - 1)

        ps = _inproj(hs, w_in_p, l, N_SAMPLE)
        ya, s_ssd_t = _ssd(ps, prm, l, (state_ssd_t, ssd_c0), s_ssd_t)
        yb = _swa(ps, prm, l, (cache_k, cache_v))
        yc, s_ret = _ret(ps, prm, l, (state_ret,), s_ret)
        yd, s_gdn = _gdn(ps, prm, l, (state_gdn, gdn_c0), s_gdn)
        xs, hs = _outproj((ya, yb, yc, yd), w_out_b, xs, post_g, mask_s, pre_g, l, N_SAMPLE)

        pp4 = pp.reshape(PROMPT_CHUNKS, BATCH, CHUNK, IN_W_PAD)

        def last_chunk(r0, c0, c1):
            return lax.slice(pp4, (PROMPT_CHUNKS - 1, 0, r0, c0), (PROMPT_CHUNKS, BATCH, CHUNK, c1))[0]

        ps3 = ps.reshape(DEC_BATCH, SUBLANES, IN_W_PAD)
        p_out["ssd"].append(p_ssd)
        p_out["ssd_conv"].append(last_chunk(tail.start, COL_X, COL_X + SSD_CONV_W))
        p_out["swa_k"].append(last_chunk(0, COL_KV, COL_KV + KV_W))
        p_out["swa_v"].append(last_chunk(0, COL_KV + KV_W, COL_KV + 2 * KV_W))
        p_out["ret"].append(p_ret)
        p_out["gdn"].append(p_gdn)
        p_out["gdn_conv"].append(last_chunk(tail.start, COL_QD, COL_QD + GDN_CONV_W))
        s_out["ssd_conv"].append(ps3[:, 1:DEC_SEQ, COL_X:COL_X + SSD_CONV_W])
        s_out["swa_k"].append(ps3[:, :DEC_SEQ, COL_KV:COL_KV + KV_W])
        s_out["swa_v"].append(ps3[:, :DEC_SEQ, COL_KV + KV_W:COL_KV + 2 * KV_W])
        s_out["gdn_conv"].append(ps3[:, 1:DEC_SEQ, COL_QD:COL_QD + GDN_CONV_W])

    p_st = {k: jnp.stack(v) for k, v in p_out.items()}
    s_st = {k: jnp.stack(v) for k, v in s_out.items()}
    kv_shape = (SWA_KV_HEADS, SWA_HEAD_DIM)
    y_prompt = xp.reshape(BATCH, SEQ, D_MODEL)
    y_sample = xs.reshape(DEC_BATCH, SUBLANES, D_MODEL)[:, :DEC_SEQ]
    return (
        y_prompt, y_sample,
        p_st["ssd"], p_st["ssd_conv"],
        p_st["swa_k"].reshape((DEPTH, BATCH, WINDOW) + kv_shape),
        p_st["swa_v"].reshape((DEPTH, BATCH, WINDOW) + kv_shape),
        p_st["ret"], p_st["gdn"], p_st["gdn_conv"],
        jnp.swapaxes(s_ssd_t, 3, 4), s_st["ssd_conv"],
        jnp.concatenate([cache_swa_k[:, :, DEC_SEQ:], s_st["swa_k"].reshape((DEPTH, DEC_BATCH, DEC_SEQ) + kv_shape)], axis=2),
        jnp.concatenate([cache_swa_v[:, :, DEC_SEQ:], s_st["swa_v"].reshape((DEPTH, DEC_BATCH, DEC_SEQ) + kv_shape)], axis=2),
        s_ret, s_gdn, s_st["gdn_conv"],
    )
```

```python
import functools

import jax
import jax.numpy as jnp
import numpy as np
from jax import lax
from jax.experimental import pallas as pl
from jax.experimental.pallas import tpu as pltpu

F32 = jnp.float32
BF16 = jnp.bfloat16
HIGHEST = lax.Precision.HIGHEST

D_MODEL = 2048
BATCH = 4
SEQ = 2048
DEPTH = 4
DEC_BATCH = 32
DEC_SEQ = 4
N_META = 16
GROUP_W = 512
CONV_K = 4
NORM_EPS = 1e-6
WINDOW = 128

SSD_HEADS, SSD_HEAD_DIM, SSD_GROUPS, SSD_STATE = 8, 64, 2, 128
SWA_HEADS, SWA_KV_HEADS, SWA_HEAD_DIM = 8, 2, 64
RET_HEADS, RET_DK, RET_DV = 4, 64, 128
GDN_HEADS, GDN_DK, GDN_DV = 4, 128, 128
SSD_CONV_W = GROUP_W + 2 * SSD_GROUPS * SSD_STATE
GDN_CONV_W = 3 * GROUP_W
KV_W = SWA_KV_HEADS * SWA_HEAD_DIM

CHUNK = 128
SUBLANES = 8
PROMPT_PAD = CHUNK - N_META
PROMPT_ROWS = PROMPT_PAD + N_META + SEQ
PROMPT_CHUNKS = PROMPT_ROWS // CHUNK
N_PROMPT = BATCH * PROMPT_ROWS
N_SAMPLE = DEC_BATCH * SUBLANES
SAMPLE_SEQS = 8

IN_W = 6416
IN_W_PAD = 6528
COL_Z, COL_X, COL_BC, COL_QA, COL_GA, COL_VR, COL_GR = 0, 512, 1024, 1536, 2048, 2560, 3072
COL_QD, COL_KD, COL_VD, COL_GD, COL_QKR, COL_KV, COL_SMALL = 3584, 4096, 4608, 5120, 5632, 6144, 6400
LANE_DT, LANE_BETA, LANE_DECAY = 0, 8, 12
ROW_SSD_DTB, ROW_SSD_ALOG, ROW_GDN_DTB, ROW_GDN_ALOG, ROW_LOG_GAMMA = 0, 1, 2, 3, 4

W_SEGMENTS = ((0, 1536), (1544, 2056), (2312, 2824), (3336, 3848), (3848, 4360), (4360, 5896), (5896, 6408),
              (2824, 3336), (2056, 2312))
W_BLOCKS = IN_W_PAD // CHUNK
W_PER_STEP = 3
W_STEPS = W_BLOCKS // W_PER_STEP
assert W_STEPS * W_PER_STEP == W_BLOCKS

PROJ_TN = 2176
PROMPT_TM = 512
VMEM_LIMIT = 48 * 1024 * 1024
MIXER_VMEM_LIMIT = 56 * 1024 * 1024


def _sigmoid(x):
    return 1.0 / (1.0 + jnp.exp(-x))


def _silu(x):
    return x * _sigmoid(x)


def _softplus(x):
    return jnp.maximum(x, 0.0) + jnp.log1p(jnp.exp(-jnp.abs(x)))


def _dot(a, b):
    return jnp.dot(a.astype(BF16), b.astype(BF16), preferred_element_type=F32)


def _dot_nt(a, b):
    return lax.dot_general(a.astype(BF16), b.astype(BF16), (((1,), (1,)), ((), ())),
                           preferred_element_type=F32)


def _dot_tn(a, b):
    return lax.dot_general(a.astype(BF16), b.astype(BF16), (((0,), (0,)), ((), ())),
                           preferred_element_type=F32)


def _dot_f32(a, b):
    return jnp.dot(a, b, precision=HIGHEST, preferred_element_type=F32)


def _pad_rows(a, rows=CHUNK):
    if a.shape[0] == rows:
        return a
    return jnp.concatenate([a, jnp.zeros((rows - a.shape[0], a.shape[1]), a.dtype)], axis=0)


def _col(a, lane, width):
    return jnp.broadcast_to(a[:, lane:lane + 1], (a.shape[0], width))


def _iotas(m):
    row = lax.broadcasted_iota(jnp.int32, (m, CHUNK), 0)
    col = lax.broadcasted_iota(jnp.int32, (m, CHUNK), 1)
    return row, col


def _valid_rows(m, chunk_idx, first_lo, hi):
    r = lax.broadcasted_iota(jnp.int32, (m, 1), 0)
    lo = jnp.where(chunk_idx == 0, first_lo, 0)
    return (r >= lo) & (r < hi)


def _pack_lanes(per_seq):
    stride = CHUNK // len(per_seq)
    out = per_seq[0]
    for b in range(1, len(per_seq)):
        out = out + pltpu.roll(per_seq[b], b * stride, 1)
    return out, stride


def _decay_terms(la):
    m = la.shape[0]
    row, col = _iotas(m)
    lower = (row >= col).astype(F32)
    cum = _dot_f32(lower, _pad_rows(la))
    last = cum[m - 1:m, :]
    return cum, _pad_rows(cum).T, jnp.exp(cum), jnp.exp(last - cum), jnp.exp(last)


def _decay_matrix(cum, cum_t, lane):
    m = cum.shape[0]
    row, col = _iotas(m)
    keep = row >= col
    seg = _col(cum, lane, CHUNK) - jnp.broadcast_to(cum_t[lane:lane + 1, :], (m, CHUNK))
    return jnp.where(keep, jnp.exp(jnp.where(keep, seg, 0.0)), 0.0)


def _shift_rows_matrix(rows):
    r = lax.broadcasted_iota(jnp.int32, (rows, rows), 0)
    c = lax.broadcasted_iota(jnp.int32, (rows, rows), 1)
    blocks = [jnp.where(r - c == CONV_K - 1 - j, 1.0, 0.0) for j in range(CONV_K - 1)]
    return jnp.concatenate(blocks, axis=0).astype(BF16)


def _causal_conv(ext_ref, cur, w_ref, shift_mat):
    rows = cur.shape[0]
    w_last = w_ref[CONV_K - 1:CONV_K, :]
    ext_ref[SUBLANES:2 * SUBLANES, :] = cur[0:SUBLANES]
    head = cur[0:SUBLANES] * w_last
    for j in range(CONV_K - 1):
        start = SUBLANES - (CONV_K - 1) + j
        head = head + ext_ref[start:start + SUBLANES, :] * w_ref[j:j + 1, :]
    ext_ref[0:SUBLANES, :] = cur[rows - SUBLANES:rows]
    if rows == SUBLANES:
        return head
    shifted = jnp.dot(shift_mat, cur.astype(BF16), preferred_element_type=F32)
    acc = cur * w_last
    for j in range(CONV_K - 1):
        acc = acc + shifted[j * rows:(j + 1) * rows] * w_ref[j:j + 1, :]
    return jnp.concatenate([head, acc[SUBLANES:]], axis=0)


def _rms(x, w):
    ms = jnp.mean(x * x, axis=-1, keepdims=True)
    return x * lax.rsqrt(ms + NORM_EPS) * w


def _prenorm_kernel(x_ref, g_ref, h_ref):
    h_ref[...] = _rms(x_ref[...], g_ref[...]).astype(BF16)


def _prenorm(x, g_all, layer, tm):
    n = x.shape[0]
    return pl.pallas_call(
        _prenorm_kernel,
        grid=(n // tm,),
        in_specs=[pl.BlockSpec((tm, D_MODEL), lambda i: (i, 0)),
                  pl.BlockSpec((None, 1, D_MODEL), lambda i: (layer, 0, 0))],
        out_specs=pl.BlockSpec((tm, D_MODEL), lambda i: (i, 0)),
        out_shape=jax.ShapeDtypeStruct((n, D_MODEL), BF16),
        compiler_params=pltpu.CompilerParams(dimension_semantics=("arbitrary",)),
        name="prenorm",
    )(x, g_all)


def _wprep_kernel(offs_ref, a0_ref, a1_ref, a2_ref, b_ref, o_ref):
    j = pl.program_id(1)
    o_ref[0:CHUNK, :] = a0_ref[...].astype(BF16)
    o_ref[CHUNK:2 * CHUNK, :] = a1_ref[...].astype(BF16)

    @pl.when(j < W_STEPS - 1)
    def _():
        o_ref[2 * CHUNK:3 * CHUNK, :] = a2_ref[...].astype(BF16)

    @pl.when(j == W_STEPS - 1)
    def _():
        small = jnp.concatenate([a2_ref[0:SUBLANES, :], b_ref[...],
                                 jnp.zeros((CHUNK - 2 * SUBLANES, D_MODEL), F32)], axis=0)
        o_ref[2 * CHUNK:3 * CHUNK, :] = small.astype(BF16)


def _wprep(w_t):
    offs = np.concatenate([np.arange(a, b, CHUNK) for a, b in W_SEGMENTS] + [[1536]]).astype(np.int32)
    assert offs.shape[0] == W_BLOCKS and not (offs % SUBLANES).any()
    offs = offs // SUBLANES
    def source_block(k):
        return pl.BlockSpec(
            (pl.Element(CHUNK), pl.Element(D_MODEL)),
            lambda l, j, offs: ((l * (IN_W // SUBLANES) + offs[W_PER_STEP * j + k]) * SUBLANES, 0))

    grid_spec = pltpu.PrefetchScalarGridSpec(
        num_scalar_prefetch=1,
        grid=(DEPTH, W_STEPS),
        in_specs=[source_block(k) for k in range(W_PER_STEP)]
        + [pl.BlockSpec((SUBLANES, D_MODEL), lambda l, j, offs: ((l * IN_W + 6408) // SUBLANES, 0))],
        out_specs=pl.BlockSpec((None, W_PER_STEP * CHUNK, D_MODEL), lambda l, j, offs: (l, j, 0)),
    )
    w_rows = w_t.reshape(DEPTH * IN_W, D_MODEL)
    return pl.pallas_call(
        _wprep_kernel, grid_spec=grid_spec,
        out_shape=jax.ShapeDtypeStruct((DEPTH, IN_W_PAD, D_MODEL), BF16),
        compiler_params=pltpu.CompilerParams(dimension_semantics=("arbitrary", "arbitrary")),
        name="wprep",
    )(jnp.asarray(offs), w_rows, w_rows, w_rows, w_rows)


def _inproj_kernel(h_ref, w_ref, o_ref):
    o_ref[...] = lax.dot_general(h_ref[...], w_ref[...], (((1,), (1,)), ((), ())),
                                 preferred_element_type=F32)


def _inproj(h, w_all, layer, tm):
    n = h.shape[0]
    return pl.pallas_call(
        _inproj_kernel,
        grid=(IN_W_PAD // PROJ_TN, n // tm),
        in_specs=[
            pl.BlockSpec((tm, D_MODEL), lambda j, i: (i, 0)),
            pl.BlockSpec((None, PROJ_TN, D_MODEL), lambda j, i: (layer, j, 0)),
        ],
        out_specs=pl.BlockSpec((tm, PROJ_TN), lambda j, i: (i, j)),
        out_shape=jax.ShapeDtypeStruct((n, IN_W_PAD), F32),
        compiler_params=pltpu.CompilerParams(
            dimension_semantics=("arbitrary", "arbitrary"), vmem_limit_bytes=VMEM_LIMIT),
        name="inproj",
    )(h, w_all)


def _embed_prompt_kernel(x_ref, meta_ref, g_ref, o_ref, h_ref):
    c = pl.program_id(0)

    @pl.when(c == 0)
    def _():
        head = jnp.concatenate([jnp.zeros((PROMPT_PAD, D_MODEL), F32), meta_ref[...]], axis=0)
        for b in range(BATCH):
            o_ref[b * CHUNK:(b + 1) * CHUNK, :] = head

    @pl.when(c > 0)
    def _():
        for b in range(BATCH):
            o_ref[b * CHUNK:(b + 1) * CHUNK, :] = x_ref[b]

    h_ref[...] = _rms(o_ref[...], g_ref[...]).astype(BF16)


def _embed_prompt(x_prompt, meta_tokens, g_all):
    x4 = x_prompt.reshape(BATCH, SEQ // CHUNK, CHUNK, D_MODEL)
    row = pl.BlockSpec((BATCH * CHUNK, D_MODEL), lambda c: (c, 0))
    return pl.pallas_call(
        _embed_prompt_kernel,
        grid=(PROMPT_CHUNKS,),
        in_specs=[pl.BlockSpec((BATCH, None, CHUNK, D_MODEL), lambda c: (0, jnp.maximum(c - 1, 0), 0, 0)),
                  pl.BlockSpec((N_META, D_MODEL), lambda c: (0, 0)),
                  pl.BlockSpec((None, 1, D_MODEL), lambda c: (0, 0, 0))],
        out_specs=[row, row],
        out_shape=[jax.ShapeDtypeStruct((N_PROMPT, D_MODEL), F32),
                   jax.ShapeDtypeStruct((N_PROMPT, D_MODEL), BF16)],
        compiler_params=pltpu.CompilerParams(dimension_semantics=("arbitrary",), vmem_limit_bytes=VMEM_LIMIT),
        name="embed_prompt",
    )(x4, meta_tokens, g_all)


def _outproj_kernel(with_next, by_sequence, *refs):
    if with_next:
        ya_ref, yb_ref, yc_ref, yd_ref, w_ref, x_ref, g_ref, m_ref, gn_ref, o_ref, h_ref = refs
    else:
        ya_ref, yb_ref, yc_ref, yd_ref, w_ref, x_ref, g_ref, m_ref, o_ref = refs
    acc = None
    for g, y_ref in enumerate((ya_ref, yb_ref, yc_ref, yd_ref)):
        part = jnp.dot(y_ref[...], w_ref[g * GROUP_W:(g + 1) * GROUP_W, :], preferred_element_type=F32)
        acc = part if acc is None else acc + part
    x_new = jnp.where(m_ref[...] > 0.0, x_ref[...] + _rms(acc, g_ref[...]), 0.0)
    if by_sequence:
        for b in range(BATCH):
            o_ref[b] = x_new[b * CHUNK:(b + 1) * CHUNK]
    else:
        o_ref[...] = x_new
    if with_next:
        h_ref[...] = _rms(x_new, gn_ref[...]).astype(BF16)


def _outproj(ys, w_all, x, post_all, rowmask, pre_all, layer, tm, by_sequence=False):
    n = x.shape[0]
    with_next = layer + 1 < DEPTH
    yspec = pl.BlockSpec((tm, GROUP_W), lambda i: (i, 0))
    row = pl.BlockSpec((tm, D_MODEL), lambda i: (i, 0))
    in_specs = [yspec, yspec, yspec, yspec,
                pl.BlockSpec((None, D_MODEL, D_MODEL), lambda i: (layer, 0, 0)),
                row,
                pl.BlockSpec((None, 1, D_MODEL), lambda i: (layer, 0, 0)),
                pl.BlockSpec((tm, 1), lambda i: (i, 0))]
    args = list(ys) + [w_all, x, post_all, rowmask]
    if by_sequence:
        assert tm == BATCH * CHUNK and not with_next
        out_specs = [pl.BlockSpec((BATCH, None, CHUNK, D_MODEL), lambda i: (0, jnp.maximum(i - 1, 0), 0, 0))]
        out_shape = [jax.ShapeDtypeStruct((BATCH, SEQ // CHUNK, CHUNK, D_MODEL), F32)]
    else:
        out_specs = [row]
        out_shape = [jax.ShapeDtypeStruct((n, D_MODEL), F32)]
    if with_next:
        in_specs.append(pl.BlockSpec((None, 1, D_MODEL), lambda i: (layer + 1, 0, 0)))
        args.append(pre_all)
        out_specs.append(row)
        out_shape.append(jax.ShapeDtypeStruct((n, D_MODEL), BF16))
    res = pl.pallas_call(
        functools.partial(_outproj_kernel, with_next, by_sequence),
        grid=(n // tm,), in_specs=in_specs, out_specs=out_specs, out_shape=out_shape,
        compiler_params=pltpu.CompilerParams(
            dimension_semantics=("arbitrary",), vmem_limit_bytes=VMEM_LIMIT),
        name="outproj",
    )(*args)
    return (res[0], res[1]) if with_next else (res[0], None)


def _init_state(c, has_init, s_scr, s0_ref, ext=None, c0_ref=None):
    @pl.when(c == 0)
    def _():
        if has_init:
            s_scr[...] = s0_ref[...]
            if ext is not None:
                ext[:, 0:SUBLANES, :] = c0_ref[...]
        else:
            s_scr[...] = jnp.zeros_like(s_scr)
            if ext is not None:
                ext[:, 0:SUBLANES, :] = jnp.zeros((ext.shape[0], SUBLANES, ext.shape[2]), F32)


def _run_staggered(groups, skew):
    pending, live, tick = list(groups), [], 0
    while pending or live:
        if pending and tick % skew == 0:
            live.append(pending.pop(0))
        for gen in list(live):
            if next(gen, "done") == "done":
                live.remove(gen)
        tick += 1


def _ssd_stages(nseq, rows, layer, has_init, state_t, nchunks, first_lo, hi, *refs):
    if has_init:
        (z_ref, x_ref, bc_ref, sm_ref, s0_ref, c0_ref, _, cw_ref, cb_ref, lp_ref, nw_ref, d_ref,
         y_ref, so_ref, s_scr, ext) = refs
    else:
        (z_ref, x_ref, bc_ref, sm_ref, cw_ref, cb_ref, lp_ref, nw_ref, d_ref,
         y_ref, so_ref, s_scr, ext) = refs
        s0_ref = c0_ref = None
    c = pl.program_id(1)
    m = rows
    _init_state(c, has_init, s_scr, s0_ref, ext, c0_ref)

    valid = _valid_rows(m, c, first_lo, hi)
    lane = lax.broadcasted_iota(jnp.int32, (1, CHUNK), 1)
    head_lane = (lane >= LANE_DT) & (lane < LANE_DT + SSD_HEADS)
    dt_all = _softplus(sm_ref[...] + lp_ref[ROW_SSD_DTB:ROW_SSD_DTB + 1, :])
    la_all = -jnp.exp(lp_ref[ROW_SSD_ALOG:ROW_SSD_ALOG + 1, :]) * dt_all
    dts = [jnp.where(valid, dt_all[b * m:(b + 1) * m], 0.0) for b in range(nseq)]
    la, stride = _pack_lanes([jnp.where(valid & head_lane, la_all[b * m:(b + 1) * m], 0.0) for b in range(nseq)])
    cum, cum_t, e_cum, e_rem, e_last = _decay_terms(la)

    shift_mat = _shift_rows_matrix(m) if m > SUBLANES else None
    hpg = SSD_HEADS // SSD_GROUPS
    c_off = GROUP_W + SSD_GROUPS * SSD_STATE
    gw = GROUP_W // SSD_GROUPS
    ys = [None] * nseq

    def sequence(b):
        rows_b = slice(b * m, (b + 1) * m)
        cur = jnp.concatenate([x_ref[rows_b, :], bc_ref[rows_b, :]], axis=1)
        a_b = _silu(_causal_conv(ext.at[b], cur, cw_ref, shift_mat) + cb_ref[...])
        yield
        scores = []
        for g in range(SSD_GROUPS):
            b_g = a_b[:, GROUP_W + g * SSD_STATE:GROUP_W + (g + 1) * SSD_STATE]
            c_g = a_b[:, c_off + g * SSD_STATE:c_off + (g + 1) * SSD_STATE]
            scores.append((_dot_nt(c_g, _pad_rows(b_g)), b_g, c_g))
        yield
        heads = []
        for h in range(SSD_HEADS):
            sc, b_g, c_g = scores[h // hpg]
            ln = b * stride + LANE_DT + h
            x_h = a_b[:, h * SSD_HEAD_DIM:(h + 1) * SSD_HEAD_DIM]
            v_h = x_h * _col(dts[b], LANE_DT + h, SSD_HEAD_DIM)
            s_h = s_scr[b, h]
            v_w = v_h * _col(e_rem, ln, SSD_HEAD_DIM)
            if state_t:
                from_state = _dot_nt(c_g, s_h)
                s_scr[b, h] = s_h * e_last[:, ln:ln + 1] + _dot_tn(v_w, b_g)
            else:
                from_state = _dot(c_g, s_h)
                s_scr[b, h] = s_h * e_last[:, ln:ln + 1] + _dot_tn(b_g, v_w)
            o = (_dot(sc * _decay_matrix(cum, cum_t, ln), _pad_rows(v_h))
                 + from_state * _col(e_cum, ln, SSD_HEAD_DIM))
            heads.append(o + d_ref[layer, h] * x_h)
            yield
        y = jnp.concatenate(heads, axis=1) * _silu(z_ref[rows_b, :])
        outs = [_rms(y[:, g * gw:(g + 1) * gw], nw_ref[:, g * gw:(g + 1) * gw]) for g in range(SSD_GROUPS)]
        ys[b] = jnp.concatenate(outs, axis=1)

    def finish():
        y_ref[...] = jnp.concatenate(ys, axis=0).astype(y_ref.dtype)

        @pl.when(c == nchunks - 1)
        def _():
            so_ref[...] = s_scr[...]

    return [sequence(b) for b in range(nseq)], finish


def _ret_stages(nseq, seq_group, rows, has_init, nchunks, first_lo, hi, *refs):
    if has_init:
        qk_ref, v_ref, g_ref, s0_ref, _, lp_ref, nw_ref, y_ref, so_ref, s_scr = refs
    else:
        qk_ref, v_ref, g_ref, lp_ref, nw_ref, y_ref, so_ref, s_scr = refs
        s0_ref = None
    c = pl.program_id(1)
    m = rows
    _init_state(c, has_init, s_scr, s0_ref)

    valid = _valid_rows(m, c, first_lo, hi)
    lane = lax.broadcasted_iota(jnp.int32, (1, CHUNK), 1)
    log_gamma = jnp.broadcast_to(lp_ref[ROW_LOG_GAMMA:ROW_LOG_GAMMA + 1, :], (m, CHUNK))
    la_one = jnp.where(valid & (lane < RET_HEADS), log_gamma, 0.0)
    la, stride = _pack_lanes([la_one] * nseq)
    cum, cum_t, e_cum, e_rem, e_last = _decay_terms(la)

    qk_all = qk_ref[...]
    v_all = v_ref[...]
    all_units = []
    for b in range(nseq):
        qk = qk_all[b * m:(b + 1) * m]
        v = jnp.where(valid, v_all[b * m:(b + 1) * m], 0.0)
        for h in range(RET_HEADS):
            q_h = qk[:, h * RET_DK:(h + 1) * RET_DK]
            k_h = qk[:, (RET_HEADS + h) * RET_DK:(RET_HEADS + h + 1) * RET_DK] * (RET_DK ** -0.5)
            s_h = s_scr[b, h]
            all_units.append(dict(b=b, h=h, ln=b * stride + h, k=k_h, v=v[:, h * RET_DV:(h + 1) * RET_DV],
                                  s=s_h, raw=_dot_nt(q_h, _pad_rows(k_h)), q_s=_dot(q_h, s_h)))
    ys = [None] * nseq

    def group(b0):
        units = all_units[b0 * RET_HEADS:(b0 + seq_group) * RET_HEADS]
        scs = [u["raw"] * _decay_matrix(cum, cum_t, u["ln"]) for u in units]
        yield
        os_ = [_dot(sc, _pad_rows(u["v"])) + u["q_s"] * _col(e_cum, u["ln"], RET_DV)
               for sc, u in zip(scs, units)]
        yield
        for u in units:
            ln = u["ln"]
            s_scr[u["b"], u["h"]] = (u["s"] * e_last[:, ln:ln + 1]
                                     + _dot_tn(u["k"] * _col(e_rem, ln, RET_DK), u["v"]))
        yield
        for i in range(seq_group):
            heads = [_rms(os_[i * RET_HEADS + h], nw_ref[:, h * RET_DV:(h + 1) * RET_DV])
                     for h in range(RET_HEADS)]
            rows_b = slice((b0 + i) * m, (b0 + i + 1) * m)
            ys[b0 + i] = jnp.concatenate(heads, axis=1) * _silu(g_ref[rows_b, :])

    def finish():
        y_ref[...] = jnp.concatenate(ys, axis=0).astype(y_ref.dtype)

        @pl.when(c == nchunks - 1)
        def _():
            so_ref[...] = s_scr[...]

    return [group(b0) for b0 in range(0, nseq, seq_group)], finish


def _gdn_stages(nseq, seq_group, rows, has_init, nchunks, first_lo, hi, *refs):
    if has_init:
        (q_ref, k_ref, v_ref, g_ref, sm_ref, s0_ref, c0_ref, _, cw_ref, lp_ref, nw_ref,
         y_ref, so_ref, s_scr, ext) = refs
    else:
        (q_ref, k_ref, v_ref, g_ref, sm_ref, cw_ref, lp_ref, nw_ref,
         y_ref, so_ref, s_scr, ext) = refs
        s0_ref = c0_ref = None
    c = pl.program_id(1)
    m = rows
    _init_state(c, has_init, s_scr, s0_ref, ext, c0_ref)

    shift_mat = _shift_rows_matrix(m) if m > SUBLANES else None
    valid = _valid_rows(m, c, first_lo, hi)
    lane = lax.broadcasted_iota(jnp.int32, (1, CHUNK), 1)
    head_lane = (lane >= LANE_DECAY) & (lane < LANE_DECAY + GDN_HEADS)
    sm = sm_ref[...]
    beta_all = _sigmoid(sm)
    la_all = (-jnp.exp(lp_ref[ROW_GDN_ALOG:ROW_GDN_ALOG + 1, :])
              * _softplus(sm + lp_ref[ROW_GDN_DTB:ROW_GDN_DTB + 1, :]))
    betas = [jnp.where(valid, beta_all[b * m:(b + 1) * m], 0.0) for b in range(nseq)]
    la, stride = _pack_lanes([jnp.where(valid & head_lane, la_all[b * m:(b + 1) * m], 0.0) for b in range(nseq)])
    cum, cum_t, e_cum, e_rem, e_last = _decay_terms(la)
    row, col = _iotas(m)
    levels = max(1, (min(m, hi) - 1).bit_length())

    ys = [None] * nseq

    def joining_mask(shift):
        same_big = lax.shift_right_logical(row, shift + 1) == lax.shift_right_logical(col, shift + 1)
        same_small = lax.shift_right_logical(row, shift) == lax.shift_right_logical(col, shift)
        return same_big & jnp.logical_not(same_small) & (row > col)

    def group(b0):
        units = []
        for b in range(b0, b0 + seq_group):
            rows_b = slice(b * m, (b + 1) * m)
            cur = jnp.concatenate([q_ref[rows_b, :], k_ref[rows_b, :], v_ref[rows_b, :]], axis=1)
            a_b = _silu(_causal_conv(ext.at[b], cur, cw_ref, shift_mat))
            yield
            for h in range(GDN_HEADS):
                ln = b * stride + LANE_DECAY + h
                q_h = a_b[:, h * GDN_DK:(h + 1) * GDN_DK]
                k_h = a_b[:, GROUP_W + h * GDN_DK:GROUP_W + (h + 1) * GDN_DK]
                v_h = a_b[:, 2 * GROUP_W + h * GDN_DV:2 * GROUP_W + (h + 1) * GDN_DV]
                q_h = q_h * lax.rsqrt(jnp.sum(q_h * q_h, axis=-1, keepdims=True) + 1e-6) * (GDN_DK ** -0.5)
                k_h = k_h * lax.rsqrt(jnp.sum(k_h * k_h, axis=-1, keepdims=True) + 1e-6)
                k_h = jnp.where(valid, k_h, 0.0)
                b_h = _col(betas[b], LANE_BETA + h, GDN_DK)
                k_b = k_h * b_h
                decay = _decay_matrix(cum, cum_t, ln)
                k_pad = _pad_rows(k_h)
                units.append(dict(
                    b=b, h=h, ln=ln, q=q_h, k=k_h, k_b=k_b, decay=decay,
                    a=jnp.where(row > col, _dot_nt(k_b, k_pad) * decay, 0.0),
                    v_b=_pad_rows(jnp.where(valid, v_h, 0.0) * b_h),
                    qk=_dot_nt(q_h, k_pad) * decay))
                yield
        eye = (row == col).astype(F32)
        mask = joining_mask(0)
        ts = [eye - jnp.where(mask, u["a"], 0.0) for u in units]
        for shift in range(1, levels):
            mask = joining_mask(shift)
            inner = [_dot(jnp.where(mask, u["a"], 0.0), _pad_rows(t)) for u, t in zip(units, ts)]
            yield
            ts = [t - _dot(t, _pad_rows(x)) for t, x in zip(ts, inner)]
            yield
        us = [_dot(t, u["v_b"]) for t, u in zip(ts, units)]
        ws = [_dot(t, _pad_rows(u["k_b"] * _col(e_cum, u["ln"], GDN_DK))) for t, u in zip(ts, units)]
        yield
        ss = [s_scr[u["b"], u["h"]] for u in units]
        v_news = [x - _dot(w, s_h) for x, w, s_h in zip(us, ws, ss)]
        yield
        os_ = [_dot(u["q"] * _col(e_cum, u["ln"], GDN_DK), s_h) + _dot(u["qk"], _pad_rows(v_new))
               for u, s_h, v_new in zip(units, ss, v_news)]
        yield
        for u, s_h, v_new in zip(units, ss, v_news):
            ln = u["ln"]
            s_scr[u["b"], u["h"]] = (s_h * e_last[:, ln:ln + 1]
                                     + _dot_tn(u["k"] * _col(e_rem, ln, GDN_DK), v_new))
        yield
        for i in range(seq_group):
            heads = [_rms(o, nw_ref[...]) for o in os_[i * GDN_HEADS:(i + 1) * GDN_HEADS]]
            rows_b = slice((b0 + i) * m, (b0 + i + 1) * m)
            ys[b0 + i] = jnp.concatenate(heads, axis=1) * _silu(g_ref[rows_b, :])

    def finish():
        y_ref[...] = jnp.concatenate(ys, axis=0).astype(y_ref.dtype)

        @pl.when(c == nchunks - 1)
        def _():
            so_ref[...] = s_scr[...]

    return [group(b0) for b0 in range(0, nseq, seq_group)], finish


def _swa_stages(nseq, seq_group, rows, layer, is_prompt, *refs):
    if is_prompt:
        q_ref, g_ref, kvc_ref, kvp_ref, sink_ref, y_ref = refs
    else:
        q_ref, g_ref, kvc_ref, kp_ref, vp_ref, sink_ref, y_ref = refs
    n = pl.program_id(1)
    m = rows
    if is_prompt:
        lo_prev = jnp.where(n == 0, CHUNK, jnp.where(n == 1, PROMPT_PAD, 0))
        lo_cur = jnp.where(n == 0, PROMPT_PAD, 0)
    else:
        lo_prev, lo_cur = 0, 0
    qi = lax.broadcasted_iota(jnp.int32, (m, 2 * CHUNK), 0)
    kj = lax.broadcasted_iota(jnp.int32, (m, 2 * CHUNK), 1)
    dist = qi + WINDOW - kj
    key_ok = ((kj < CHUNK) & (kj >= lo_prev)) | (kj >= CHUNK + lo_cur)
    visible = (dist >= 0) & (dist <= WINDOW) & key_ok
    dist_f = dist.astype(F32)

    grp = SWA_HEADS // SWA_KV_HEADS
    ys = [None] * nseq

    def group(b0):
        units = []
        for b in range(b0, b0 + seq_group):
            q = q_ref[b * m:(b + 1) * m, :]
            kvc = _pad_rows(kvc_ref[b * m:(b + 1) * m, :])
            if is_prompt:
                k_prev = kvp_ref[b * CHUNK:(b + 1) * CHUNK, 0:KV_W]
                v_prev = kvp_ref[b * CHUNK:(b + 1) * CHUNK, KV_W:2 * KV_W]
            else:
                k_prev, v_prev = kp_ref[b], vp_ref[b]
            k_all = jnp.concatenate([k_prev, kvc[:, 0:KV_W]], axis=0)
            v_all = jnp.concatenate([v_prev, kvc[:, KV_W:2 * KV_W]], axis=0)
            for h in range(SWA_HEADS):
                kvh = h // grp
                units.append(dict(h=h, q=q[:, h * SWA_HEAD_DIM:(h + 1) * SWA_HEAD_DIM],
                                  k=k_all[:, kvh * SWA_HEAD_DIM:(kvh + 1) * SWA_HEAD_DIM],
                                  v=v_all[:, kvh * SWA_HEAD_DIM:(kvh + 1) * SWA_HEAD_DIM]))
        ss = [jnp.where(visible,
                        _dot_nt(u["q"], u["k"]) * (SWA_HEAD_DIM ** -0.5)
                        - 2.0 ** (-8.0 * (u["h"] + 1) / SWA_HEADS) * dist_f, -1e30) for u in units]
        yield
        mxs = [jnp.maximum(jnp.max(s, axis=-1, keepdims=True), sink_ref[layer, u["h"]])
               for s, u in zip(ss, units)]
        es = [jnp.exp(s - mx) for s, mx in zip(ss, mxs)]
        yield
        dens = [jnp.sum(e, axis=-1, keepdims=True) + jnp.exp(sink_ref[layer, u["h"]] - mx)
                for e, mx, u in zip(es, mxs, units)]
        os_ = [_dot(e, u["v"]) / den for e, u, den in zip(es, units, dens)]
        yield
        for i in range(seq_group):
            rows_b = slice((b0 + i) * m, (b0 + i + 1) * m)
            ys[b0 + i] = (jnp.concatenate(os_[i * SWA_HEADS:(i + 1) * SWA_HEADS], axis=1)
                          * _silu(g_ref[rows_b, :]))

    def finish():
        y_ref[...] = jnp.concatenate(ys, axis=0).astype(y_ref.dtype)

    return [group(b0) for b0 in range(0, nseq, seq_group)], finish


def _mixer_kernel(parts, *refs):
    starts = [0, sum(p[1] for p in parts), sum(p[1] + p[2] for p in parts)]
    gens, finishes = [], []
    for stages, n_in, n_out, n_scr in parts:
        mine = []
        for k, cnt in enumerate((n_in, n_out, n_scr)):
            mine += refs[starts[k]:starts[k] + cnt]
            starts[k] += cnt
        g, fin = stages(*mine)
        gens.append(g)
        finishes.append(fin)
    order = []
    for i in range(max(len(g) for g in gens)):
        order += [g[i] for g in gens if i < len(g)]
    _run_staggered(order, 1)
    for fin in finishes:
        fin()


def _geometry(sample):
    if sample:
        return (DEC_BATCH // SAMPLE_SEQS, 1), SAMPLE_SEQS, SUBLANES, N_SAMPLE, 0, DEC_SEQ
    return (1, PROMPT_CHUNKS), BATCH, CHUNK, N_PROMPT, PROMPT_PAD, CHUNK


def _rowblk(sample):
    return (lambda blk, c: blk) if sample else (lambda blk, c: c)


def _pcol(nrows, rowblk, width, col):
    assert col % width == 0
    return pl.BlockSpec((nrows, width), lambda blk, c: (rowblk(blk, c), col // width))


def _layer_param(shape, layer):
    return pl.BlockSpec((None,) + shape, lambda blk, c: (layer,) + (0,) * len(shape))


def _layer_seqs(nseq, shape, layer):
    return pl.BlockSpec((None, nseq) + shape, lambda blk, c: (layer, blk) + (0,) * len(shape))


def _seqs_out(nseq, shape):
    return pl.BlockSpec((nseq,) + shape, lambda blk, c: (blk,) + (0,) * len(shape))


_SMEM = pl.BlockSpec(memory_space=pltpu.SMEM)
_MIXER_PARAMS = pltpu.CompilerParams(dimension_semantics=("arbitrary", "arbitrary"),
                                     vmem_limit_bytes=MIXER_VMEM_LIMIT)


def _state_output(sample, nseq, state, layer, acc, in_specs, args):
    if not sample:
        return _seqs_out(nseq, state), jax.ShapeDtypeStruct((nseq,) + state, F32), {}
    in_specs.append(pl.BlockSpec(memory_space=pl.ANY))
    args.append(acc)
    spec = pl.BlockSpec((None, nseq) + state, lambda blk, c: (layer, blk) + (0,) * len(state))
    return spec, jax.ShapeDtypeStruct(acc.shape, F32), {len(args) - 1: 1}


def _ssd(p, prm, layer, init, acc=None):
    sample = init is not None
    grid, nseq, rows, total, first_lo, hi = _geometry(sample)
    rb, n = _rowblk(sample), nseq * rows
    state = (SSD_HEADS, SSD_HEAD_DIM, SSD_STATE) if sample else (SSD_HEADS, SSD_STATE, SSD_HEAD_DIM)
    in_specs = [_pcol(n, rb, GROUP_W, COL_Z), _pcol(n, rb, GROUP_W, COL_X),
                _pcol(n, rb, GROUP_W, COL_BC), _pcol(n, rb, CHUNK, COL_SMALL)]
    args = [p, p, p, p]
    if sample:
        in_specs += [_layer_seqs(nseq, state, layer), _layer_seqs(nseq, (SUBLANES, SSD_CONV_W), layer)]
        args += list(init)
    state_spec, state_shape, aliases = _state_output(sample, nseq, state, layer, acc, in_specs, args)
    in_specs += [_layer_param((CONV_K, SSD_CONV_W), layer), _layer_param((1, SSD_CONV_W), layer),
                 _layer_param((SUBLANES, CHUNK), layer), _layer_param((1, GROUP_W), layer), _SMEM]
    args += [prm["ssd_conv_w"], prm["ssd_conv_b"], prm["lanes"], prm["ssd_norm"], prm["ssd_d"]]
    return dict(
        name="ssd", grid=grid,
        stages=functools.partial(_ssd_stages, nseq, rows, layer, sample, sample, grid[1], first_lo, hi),
        in_specs=in_specs, args=args,
        out_specs=[pl.BlockSpec((n, GROUP_W), lambda blk, c: (rb(blk, c), 0)), state_spec],
        out_shape=[jax.ShapeDtypeStruct((total, GROUP_W), BF16), state_shape],
        aliases=aliases,
        scratch=[pltpu.VMEM((nseq,) + state, F32), pltpu.VMEM((nseq, 2 * SUBLANES, SSD_CONV_W), F32)])


def _ret(p, prm, layer, init, acc=None):
    sample = init is not None
    grid, nseq, rows, total, first_lo, hi = _geometry(sample)
    rb, n = _rowblk(sample), nseq * rows
    state = (RET_HEADS, RET_DK, RET_DV)
    in_specs = [_pcol(n, rb, GROUP_W, COL_QKR), _pcol(n, rb, GROUP_W, COL_VR), _pcol(n, rb, GROUP_W, COL_GR)]
    args = [p, p, p]
    if sample:
        in_specs += [_layer_seqs(nseq, state, layer)]
        args += list(init)
    state_spec, state_shape, aliases = _state_output(sample, nseq, state, layer, acc, in_specs, args)
    in_specs += [_layer_param((SUBLANES, CHUNK), layer), _layer_param((1, GROUP_W), layer)]
    args += [prm["lanes"], prm["ret_norm"]]
    return dict(
        name="ret", grid=grid,
        stages=functools.partial(_ret_stages, nseq, nseq if sample else 1, rows, sample, grid[1], first_lo, hi),
        in_specs=in_specs, args=args,
        out_specs=[pl.BlockSpec((n, GROUP_W), lambda blk, c: (rb(blk, c), 0)), state_spec],
        out_shape=[jax.ShapeDtypeStruct((total, GROUP_W), BF16), state_shape],
        aliases=aliases,
        scratch=[pltpu.VMEM((nseq,) + state, F32)])


def _gdn(p, prm, layer, init, acc=None):
    sample = init is not None
    grid, nseq, rows, total, first_lo, hi = _geometry(sample)
    rb, n = _rowblk(sample), nseq * rows
    state = (GDN_HEADS, GDN_DK, GDN_DV)
    in_specs = [_pcol(n, rb, GROUP_W, COL_QD), _pcol(n, rb, GROUP_W, COL_KD), _pcol(n, rb, GROUP_W, COL_VD),
                _pcol(n, rb, GROUP_W, COL_GD), _pcol(n, rb, CHUNK, COL_SMALL)]
    args = [p, p, p, p, p]
    if sample:
        in_specs += [_layer_seqs(nseq, state, layer), _layer_seqs(nseq, (SUBLANES, GDN_CONV_W), layer)]
        args += list(init)
    state_spec, state_shape, aliases = _state_output(sample, nseq, state, layer, acc, in_specs, args)
    in_specs += [_layer_param((CONV_K, GDN_CONV_W), layer), _layer_param((SUBLANES, CHUNK), layer),
                 _layer_param((1, GDN_DV), layer)]
    args += [prm["gdn_conv_w"], prm["lanes"], prm["gdn_norm"]]
    return dict(
        name="gdn", grid=grid,
        stages=functools.partial(_gdn_stages, nseq, 2 if sample else 1, rows, sample, grid[1], first_lo, hi),
        in_specs=in_specs, args=args,
        out_specs=[pl.BlockSpec((n, GROUP_W), lambda blk, c: (rb(blk, c), 0)), state_spec],
        out_shape=[jax.ShapeDtypeStruct((total, GROUP_W), BF16), state_shape],
        aliases=aliases,
        scratch=[pltpu.VMEM((nseq,) + state, F32), pltpu.VMEM((nseq, 2 * SUBLANES, GDN_CONV_W), F32)])


def _swa(p, prm, layer, cache):
    sample = cache is not None
    grid, nseq, rows, total, _, _ = _geometry(sample)
    rb, n = _rowblk(sample), nseq * rows
    in_specs = [_pcol(n, rb, GROUP_W, COL_QA), _pcol(n, rb, GROUP_W, COL_GA), _pcol(n, rb, 2 * KV_W, COL_KV)]
    args = [p, p, p]
    if sample:
        in_specs += [_layer_seqs(nseq, (WINDOW, KV_W), layer), _layer_seqs(nseq, (WINDOW, KV_W), layer)]
        args += list(cache)
    else:
        in_specs += [pl.BlockSpec((n, 2 * KV_W), lambda blk, c: (jnp.maximum(c - 1, 0), COL_KV // (2 * KV_W)))]
        args += [p]
    in_specs += [_SMEM]
    args += [prm["swa_sinks"]]
    return dict(
        name="swa", grid=grid,
        stages=functools.partial(_swa_stages, nseq, nseq if sample else 1, rows, layer, not sample),
        in_specs=in_specs, args=args,
        out_specs=[pl.BlockSpec((n, GROUP_W), lambda blk, c: (rb(blk, c), 0))],
        out_shape=[jax.ShapeDtypeStruct((total, GROUP_W), BF16)],
        aliases={}, scratch=[])


def _run_mixers(parts, suffix):
    kernel_parts = tuple((p["stages"], len(p["in_specs"]), len(p["out_specs"]), len(p["scratch"]))
                         for p in parts)
    aliases, n_in, n_out = {}, 0, 0
    for p in parts:
        aliases.update({n_in + k: n_out + v for k, v in p["aliases"].items()})
        n_in += len(p["in_specs"])
        n_out += len(p["out_specs"])
    res = pl.pallas_call(
        functools.partial(_mixer_kernel, kernel_parts),
        grid=parts[0]["grid"],
        in_specs=[s for p in parts for s in p["in_specs"]],
        out_specs=[s for p in parts for s in p["out_specs"]],
        out_shape=[s for p in parts for s in p["out_shape"]],
        scratch_shapes=[s for p in parts for s in p["scratch"]],
        input_output_aliases=aliases,
        compiler_params=_MIXER_PARAMS,
        name="_".join(p["name"] for p in parts) + suffix,
    )(*[a for p in parts for a in p["args"]])
    outs, k = [], 0
    for p in parts:
        outs.append(list(res[k:k + len(p["out_specs"])]))
        k += len(p["out_specs"])
    return outs


def _lane_table(ssd_dt_bias, ssd_a_log, gdn_dt_bias, gdn_a_log):
    def put(v, lane):
        return jnp.pad(v.astype(F32), ((0, 0), (lane, CHUNK - lane - v.shape[1])))
    log_gamma = jnp.log1p(-jnp.exp2(-5.0 - jnp.arange(RET_HEADS, dtype=F32)))
    rows = [put(ssd_dt_bias, LANE_DT), put(ssd_a_log, LANE_DT), put(gdn_dt_bias, LANE_DECAY),
            put(gdn_a_log, LANE_DECAY), put(jnp.broadcast_to(log_gamma, (DEPTH, RET_HEADS)), 0)]
    rows += [jnp.zeros((DEPTH, CHUNK), F32)] * (SUBLANES - len(rows))
    return jnp.stack(rows, axis=1)


def _conv_state_pad(state):
    return jnp.pad(state, ((0, 0), (0, 0), (SUBLANES - (CONV_K - 1), 0), (0, 0)))


def kernel(x_prompt, x_sample, state_ssd, state_ssd_conv, cache_swa_k, cache_swa_v, state_ret, state_gdn,
           state_gdn_conv, meta_tokens, pre_norm, post_norm, w_in, w_out, ssd_conv_w, ssd_conv_b, ssd_dt_bias,
           ssd_a_log, ssd_d, ssd_norm, swa_sinks, ret_norm, gdn_conv_w, gdn_dt_bias, gdn_a_log, gdn_norm):
    xs = jnp.pad(x_sample, ((0, 0), (0, SUBLANES - DEC_SEQ), (0, 0))).reshape(N_SAMPLE, D_MODEL)
    mask_np = np.ones((PROMPT_CHUNKS, BATCH, CHUNK), np.float32)
    mask_np[0, :, :PROMPT_PAD] = 0.0
    mask_p = jnp.asarray(mask_np.reshape(N_PROMPT, 1))
    mask_s = jnp.asarray(np.tile(np.arange(SUBLANES) < DEC_SEQ, DEC_BATCH)[:, None], F32)

    w_in_p = _wprep(jnp.swapaxes(w_in, 1, 2))
    w_out_b = w_out.astype(BF16)
    pre_g = pre_norm.reshape(DEPTH, 1, D_MODEL)
    post_g = post_norm.reshape(DEPTH, 1, D_MODEL)
    prm = dict(
        lanes=_lane_table(ssd_dt_bias, ssd_a_log, gdn_dt_bias, gdn_a_log),
        ssd_conv_w=ssd_conv_w, ssd_conv_b=ssd_conv_b.reshape(DEPTH, 1, SSD_CONV_W),
        ssd_norm=ssd_norm.reshape(DEPTH, 1, GROUP_W), ssd_d=ssd_d, swa_sinks=swa_sinks,
        ret_norm=ret_norm.reshape(DEPTH, 1, GROUP_W), gdn_conv_w=gdn_conv_w,
        gdn_norm=gdn_norm.reshape(DEPTH, 1, GDN_DV))
    ssd_c0 = _conv_state_pad(state_ssd_conv)
    gdn_c0 = _conv_state_pad(state_gdn_conv)
    cache_k = cache_swa_k.reshape(DEPTH, DEC_BATCH, WINDOW, KV_W)
    cache_v = cache_swa_v.reshape(DEPTH, DEC_BATCH, WINDOW, KV_W)

    names = ("ssd", "ssd_conv", "swa_k", "swa_v", "ret", "gdn", "gdn_conv")
    p_out = {k: [] for k in names}
    s_out = {k: [] for k in ("ssd_conv", "swa_k", "swa_v", "gdn_conv")}
    tail = slice(CHUNK - (CONV_K - 1), CHUNK)
    state_ssd_t = jnp.swapaxes(state_ssd, 3, 4)
    s_ssd_t = jnp.zeros(state_ssd_t.shape, F32)
    s_ret = jnp.zeros(state_ret.shape, F32)
    s_gdn = jnp.zeros(state_gdn.shape, F32)
    xp, hp = _embed_prompt(x_prompt, meta_tokens, pre_g)
    hs = _prenorm(xs, pre_g, 0, N_SAMPLE)
    for l in range(DEPTH):
        pp = _inproj(hp, w_in_p, l, PROMPT_TM)
        (ya, p_ssd), (yd, p_gdn), (yb,), (yc, p_ret) = _run_mixers(
            [_ssd(pp, prm, l, None), _gdn(pp, prm, l, None), _swa(pp, prm, l, None), _ret(pp, prm, l, None)],
            "_prompt")
        xp, hp = _outproj((ya, yb, yc, yd), w_out_b, xp, post_g, mask_p, pre_g, l, PROMPT_TM,
                          by_sequence=l == DEPTH - 1)

        ps = _inproj(hs, w_in_p, l, N_SAMPLE)
        (ya, s_ssd_t), (yd, s_gdn), (yb,), (yc, s_ret) = _run_mixers(
            [_ssd(ps, prm, l, (state_ssd_t, ssd_c0), s_ssd_t), _gdn(ps, prm, l, (state_gdn, gdn_c0), s_gdn),
             _swa(ps, prm, l, (cache_k, cache_v)), _ret(ps, prm, l, (state_ret,), s_ret)],
            "_sample")
        xs, hs = _outproj((ya, yb, yc, yd), w_out_b, xs, post_g, mask_s, pre_g, l, N_SAMPLE)

        pp4 = pp.reshape(PROMPT_CHUNKS, BATCH, CHUNK, IN_W_PAD)

        def last_chunk(r0, c0, c1):
            return lax.slice(pp4, (PROMPT_CHUNKS - 1, 0, r0, c0), (PROMPT_CHUNKS, BATCH, CHUNK, c1))[0]

        ps3 = ps.reshape(DEC_BATCH, SUBLANES, IN_W_PAD)
        p_out["ssd"].append(p_ssd)
        p_out["ssd_conv"].append(last_chunk(tail.start, COL_X, COL_X + SSD_CONV_W))
        p_out["swa_k"].append(last_chunk(0, COL_KV, COL_KV + KV_W))
        p_out["swa_v"].append(last_chunk(0, COL_KV + KV_W, COL_KV + 2 * KV_W))
        p_out["ret"].append(p_ret)
        p_out["gdn"].append(p_gdn)
        p_out["gdn_conv"].append(last_chunk(tail.start, COL_QD, COL_QD + GDN_CONV_W))
        s_out["ssd_conv"].append(ps3[:, 1:DEC_SEQ, COL_X:COL_X + SSD_CONV_W])
        s_out["swa_k"].append(ps3[:, :DEC_SEQ, COL_KV:COL_KV + KV_W])
        s_out["swa_v"].append(ps3[:, :DEC_SEQ, COL_KV + KV_W:COL_KV + 2 * KV_W])
        s_out["gdn_conv"].append(ps3[:, 1:DEC_SEQ, COL_QD:COL_QD + GDN_CONV_W])

    p_st = {k: jnp.stack(v) for k, v in p_out.items()}
    s_st = {k: jnp.stack(v) for k, v in s_out.items()}
    kv_shape = (SWA_KV_HEADS, SWA_HEAD_DIM)
    y_prompt = xp.reshape(BATCH, SEQ, D_MODEL)
    y_sample = xs.reshape(DEC_BATCH, SUBLANES, D_MODEL)[:, :DEC_SEQ]
    return (
        y_prompt, y_sample,
        p_st["ssd"], p_st["ssd_conv"],
        p_st["swa_k"].reshape((DEPTH, BATCH, WINDOW) + kv_shape),
        p_st["swa_v"].reshape((DEPTH, BATCH, WINDOW) + kv_shape),
        p_st["ret"], p_st["gdn"], p_st["gdn_conv"],
        jnp.swapaxes(s_ssd_t, 3, 4), s_st["ssd_conv"],
        jnp.concatenate([cache_swa_k[:, :, DEC_SEQ:], s_st["swa_k"].reshape((DEPTH, DEC_BATCH, DEC_SEQ) + kv_shape)], axis=2),
        jnp.concatenate([cache_swa_v[:, :, DEC_SEQ:], s_st["swa_v"].reshape((DEPTH, DEC_BATCH, DEC_SEQ) + kv_shape)], axis=2),
        s_ret, s_gdn, s_st["gdn_conv"],
    )
```

```python
import functools

import jax
import jax.numpy as jnp
import numpy as np
from jax import lax
from jax.experimental import pallas as pl
from jax.experimental.pallas import tpu as pltpu

F32 = jnp.float32
BF16 = jnp.bfloat16
HIGHEST = lax.Precision.HIGHEST

D_MODEL = 2048
BATCH = 4
SEQ = 2048
DEPTH = 4
DEC_BATCH = 32
DEC_SEQ = 4
N_META = 16
GROUP_W = 512
CONV_K = 4
NORM_EPS = 1e-6
WINDOW = 128

SSD_HEADS, SSD_HEAD_DIM, SSD_GROUPS, SSD_STATE = 8, 64, 2, 128
SWA_HEADS, SWA_KV_HEADS, SWA_HEAD_DIM = 8, 2, 64
RET_HEADS, RET_DK, RET_DV = 4, 64, 128
GDN_HEADS, GDN_DK, GDN_DV = 4, 128, 128
SSD_CONV_W = GROUP_W + 2 * SSD_GROUPS * SSD_STATE
GDN_CONV_W = 3 * GROUP_W
KV_W = SWA_KV_HEADS * SWA_HEAD_DIM

CHUNK = 128
SUBLANES = 8
PROMPT_PAD = CHUNK - N_META
PROMPT_ROWS = PROMPT_PAD + N_META + SEQ
PROMPT_CHUNKS = PROMPT_ROWS // CHUNK
N_PROMPT = BATCH * PROMPT_ROWS
N_SAMPLE = DEC_BATCH * SUBLANES
SAMPLE_SEQS = 8

IN_W = 6416
IN_W_PAD = 6528
COL_Z, COL_X, COL_BC, COL_QA, COL_GA, COL_VR, COL_GR = 0, 512, 1024, 1536, 2048, 2560, 3072
COL_QD, COL_KD, COL_VD, COL_GD, COL_QKR, COL_KV, COL_SMALL = 3584, 4096, 4608, 5120, 5632, 6144, 6400
LANE_DT, LANE_BETA, LANE_DECAY = 0, 8, 12
ROW_SSD_DTB, ROW_SSD_ALOG, ROW_GDN_DTB, ROW_GDN_ALOG, ROW_LOG_GAMMA = 0, 1, 2, 3, 4

W_SEGMENTS = ((0, 1536), (1544, 2056), (2312, 2824), (3336, 3848), (3848, 4360), (4360, 5896), (5896, 6408),
              (2824, 3336), (2056, 2312))
W_BLOCKS = IN_W_PAD // CHUNK
W_PER_STEP = 3
W_STEPS = W_BLOCKS // W_PER_STEP
assert W_STEPS * W_PER_STEP == W_BLOCKS

PROJ_TN = 2176
PROMPT_TM = 512
INPROJ_TM = 1088
VMEM_LIMIT = 48 * 1024 * 1024
LARGE_VMEM_LIMIT = 56 * 1024 * 1024


def _sigmoid(x):
    return 1.0 / (1.0 + jnp.exp(-x))


def _silu(x):
    return x * _sigmoid(x)


def _softplus(x):
    return jnp.maximum(x, 0.0) + jnp.log1p(jnp.exp(-jnp.abs(x)))


def _dot(a, b):
    return jnp.dot(a.astype(BF16), b.astype(BF16), preferred_element_type=F32)


def _dot_nt(a, b):
    return lax.dot_general(a.astype(BF16), b.astype(BF16), (((1,), (1,)), ((), ())),
                           preferred_element_type=F32)


def _dot_tn(a, b):
    return lax.dot_general(a.astype(BF16), b.astype(BF16), (((0,), (0,)), ((), ())),
                           preferred_element_type=F32)


def _dot_f32(a, b):
    return jnp.dot(a, b, precision=HIGHEST, preferred_element_type=F32)


def _pad_rows(a, rows=CHUNK):
    if a.shape[0] == rows:
        return a
    return jnp.concatenate([a, jnp.zeros((rows - a.shape[0], a.shape[1]), a.dtype)], axis=0)


def _col(a, lane, width):
    return jnp.broadcast_to(a[:, lane:lane + 1], (a.shape[0], width))


def _iotas(m):
    row = lax.broadcasted_iota(jnp.int32, (m, CHUNK), 0)
    col = lax.broadcasted_iota(jnp.int32, (m, CHUNK), 1)
    return row, col


def _valid_rows(m, chunk_idx, first_lo, hi):
    r = lax.broadcasted_iota(jnp.int32, (m, 1), 0)
    lo = jnp.where(chunk_idx == 0, first_lo, 0)
    return (r >= lo) & (r < hi)


def _pack_lanes(per_seq):
    stride = CHUNK // len(per_seq)
    out = per_seq[0]
    for b in range(1, len(per_seq)):
        out = out + pltpu.roll(per_seq[b], b * stride, 1)
    return out, stride


def _decay_terms(la):
    m = la.shape[0]
    row, col = _iotas(m)
    lower = (row >= col).astype(F32)
    cum = _dot_f32(lower, _pad_rows(la))
    last = cum[m - 1:m, :]
    return cum, _pad_rows(cum).T, jnp.exp(cum), jnp.exp(last - cum), jnp.exp(last)


def _decay_matrix(cum, cum_t, lane):
    m = cum.shape[0]
    row, col = _iotas(m)
    keep = row >= col
    seg = _col(cum, lane, CHUNK) - jnp.broadcast_to(cum_t[lane:lane + 1, :], (m, CHUNK))
    return jnp.where(keep, jnp.exp(jnp.where(keep, seg, 0.0)), 0.0)


def _shift_rows_matrix(rows):
    r = lax.broadcasted_iota(jnp.int32, (rows, rows), 0)
    c = lax.broadcasted_iota(jnp.int32, (rows, rows), 1)
    blocks = [jnp.where(r - c == CONV_K - 1 - j, 1.0, 0.0) for j in range(CONV_K - 1)]
    return jnp.concatenate(blocks, axis=0).astype(BF16)


def _causal_conv(ext_ref, cur, w_ref, shift_mat):
    rows = cur.shape[0]
    w_last = w_ref[CONV_K - 1:CONV_K, :]
    ext_ref[SUBLANES:2 * SUBLANES, :] = cur[0:SUBLANES]
    head = cur[0:SUBLANES] * w_last
    for j in range(CONV_K - 1):
        start = SUBLANES - (CONV_K - 1) + j
        head = head + ext_ref[start:start + SUBLANES, :] * w_ref[j:j + 1, :]
    ext_ref[0:SUBLANES, :] = cur[rows - SUBLANES:rows]
    if rows == SUBLANES:
        return head
    shifted = jnp.dot(shift_mat, cur.astype(BF16), preferred_element_type=F32)
    acc = cur * w_last
    for j in range(CONV_K - 1):
        acc = acc + shifted[j * rows:(j + 1) * rows] * w_ref[j:j + 1, :]
    return jnp.concatenate([head, acc[SUBLANES:]], axis=0)


def _rms(x, w):
    ms = jnp.mean(x * x, axis=-1, keepdims=True)
    return x * lax.rsqrt(ms + NORM_EPS) * w


def _prenorm_kernel(x_ref, g_ref, h_ref):
    h_ref[...] = _rms(x_ref[...], g_ref[...]).astype(BF16)


def _prenorm(x, g_all, layer, tm):
    n = x.shape[0]
    return pl.pallas_call(
        _prenorm_kernel,
        grid=(n // tm,),
        in_specs=[pl.BlockSpec((tm, D_MODEL), lambda i: (i, 0)),
                  pl.BlockSpec((None, 1, D_MODEL), lambda i: (layer, 0, 0))],
        out_specs=pl.BlockSpec((tm, D_MODEL), lambda i: (i, 0)),
        out_shape=jax.ShapeDtypeStruct((n, D_MODEL), BF16),
        compiler_params=pltpu.CompilerParams(dimension_semantics=("arbitrary",)),
        name="prenorm",
    )(x, g_all)


def _wprep_kernel(offs_ref, a0_ref, a1_ref, a2_ref, b_ref, o_ref):
    j = pl.program_id(1)
    o_ref[0:CHUNK, :] = a0_ref[...].astype(BF16)
    o_ref[CHUNK:2 * CHUNK, :] = a1_ref[...].astype(BF16)

    @pl.when(j < W_STEPS - 1)
    def _():
        o_ref[2 * CHUNK:3 * CHUNK, :] = a2_ref[...].astype(BF16)

    @pl.when(j == W_STEPS - 1)
    def _():
        small = jnp.concatenate([a2_ref[0:SUBLANES, :], b_ref[...],
                                 jnp.zeros((CHUNK - 2 * SUBLANES, D_MODEL), F32)], axis=0)
        o_ref[2 * CHUNK:3 * CHUNK, :] = small.astype(BF16)


def _wprep(w_t):
    offs = np.concatenate([np.arange(a, b, CHUNK) for a, b in W_SEGMENTS] + [[1536]]).astype(np.int32)
    assert offs.shape[0] == W_BLOCKS and not (offs % SUBLANES).any()
    offs = offs // SUBLANES
    def source_block(k):
        return pl.BlockSpec(
            (pl.Element(CHUNK), pl.Element(D_MODEL)),
            lambda l, j, offs: ((l * (IN_W // SUBLANES) + offs[W_PER_STEP * j + k]) * SUBLANES, 0))

    grid_spec = pltpu.PrefetchScalarGridSpec(
        num_scalar_prefetch=1,
        grid=(DEPTH, W_STEPS),
        in_specs=[source_block(k) for k in range(W_PER_STEP)]
        + [pl.BlockSpec((SUBLANES, D_MODEL), lambda l, j, offs: ((l * IN_W + 6408) // SUBLANES, 0))],
        out_specs=pl.BlockSpec((None, W_PER_STEP * CHUNK, D_MODEL), lambda l, j, offs: (l, j, 0)),
    )
    w_rows = w_t.reshape(DEPTH * IN_W, D_MODEL)
    return pl.pallas_call(
        _wprep_kernel, grid_spec=grid_spec,
        out_shape=jax.ShapeDtypeStruct((DEPTH, IN_W_PAD, D_MODEL), BF16),
        compiler_params=pltpu.CompilerParams(dimension_semantics=("arbitrary", "arbitrary")),
        name="wprep",
    )(jnp.asarray(offs), w_rows, w_rows, w_rows, w_rows)


def _inproj_kernel(h_ref, w_ref, o_ref):
    o_ref[...] = lax.dot_general(h_ref[...], w_ref[...], (((1,), (1,)), ((), ())),
                                 preferred_element_type=F32)


def _inproj(h, w_all, layer, tm):
    n = h.shape[0]
    return pl.pallas_call(
        _inproj_kernel,
        grid=(IN_W_PAD // PROJ_TN, n // tm),
        in_specs=[
            pl.BlockSpec((tm, D_MODEL), lambda j, i: (i, 0)),
            pl.BlockSpec((None, PROJ_TN, D_MODEL), lambda j, i: (layer, j, 0)),
        ],
        out_specs=pl.BlockSpec((tm, PROJ_TN), lambda j, i: (i, j)),
        out_shape=jax.ShapeDtypeStruct((n, IN_W_PAD), F32),
        compiler_params=pltpu.CompilerParams(
            dimension_semantics=("arbitrary", "arbitrary"), vmem_limit_bytes=LARGE_VMEM_LIMIT),
        name="inproj",
    )(h, w_all)


def _embed_prompt_kernel(x_ref, meta_ref, g_ref, o_ref, h_ref):
    c = pl.program_id(0)

    @pl.when(c == 0)
    def _():
        head = jnp.concatenate([jnp.zeros((PROMPT_PAD, D_MODEL), F32), meta_ref[...]], axis=0)
        for b in range(BATCH):
            o_ref[b * CHUNK:(b + 1) * CHUNK, :] = head

    @pl.when(c > 0)
    def _():
        for b in range(BATCH):
            o_ref[b * CHUNK:(b + 1) * CHUNK, :] = x_ref[b]

    h_ref[...] = _rms(o_ref[...], g_ref[...]).astype(BF16)


def _embed_prompt(x_prompt, meta_tokens, g_all):
    x4 = x_prompt.reshape(BATCH, SEQ // CHUNK, CHUNK, D_MODEL)
    row = pl.BlockSpec((BATCH * CHUNK, D_MODEL), lambda c: (c, 0))
    return pl.pallas_call(
        _embed_prompt_kernel,
        grid=(PROMPT_CHUNKS,),
        in_specs=[pl.BlockSpec((BATCH, None, CHUNK, D_MODEL), lambda c: (0, jnp.maximum(c - 1, 0), 0, 0)),
                  pl.BlockSpec((N_META, D_MODEL), lambda c: (0, 0)),
                  pl.BlockSpec((None, 1, D_MODEL), lambda c: (0, 0, 0))],
        out_specs=[row, row],
        out_shape=[jax.ShapeDtypeStruct((N_PROMPT, D_MODEL), F32),
                   jax.ShapeDtypeStruct((N_PROMPT, D_MODEL), BF16)],
        compiler_params=pltpu.CompilerParams(dimension_semantics=("arbitrary",), vmem_limit_bytes=VMEM_LIMIT),
        name="embed_prompt",
    )(x4, meta_tokens, g_all)


def _outproj_kernel(with_next, by_sequence, *refs):
    if with_next:
        ya_ref, yb_ref, yc_ref, yd_ref, w_ref, x_ref, g_ref, m_ref, gn_ref, o_ref, h_ref = refs
    else:
        ya_ref, yb_ref, yc_ref, yd_ref, w_ref, x_ref, g_ref, m_ref, o_ref = refs
    acc = None
    for g, y_ref in enumerate((ya_ref, yb_ref, yc_ref, yd_ref)):
        part = jnp.dot(y_ref[...], w_ref[g * GROUP_W:(g + 1) * GROUP_W, :], preferred_element_type=F32)
        acc = part if acc is None else acc + part
    x_new = jnp.where(m_ref[...] > 0.0, x_ref[...] + _rms(acc, g_ref[...]), 0.0)
    if by_sequence:
        for b in range(BATCH):
            o_ref[b] = x_new[b * CHUNK:(b + 1) * CHUNK]
    else:
        o_ref[...] = x_new
    if with_next:
        h_ref[...] = _rms(x_new, gn_ref[...]).astype(BF16)


def _outproj(ys, w_all, x, post_all, rowmask, pre_all, layer, tm, by_sequence=False):
    n = x.shape[0]
    with_next = layer + 1 < DEPTH
    yspec = pl.BlockSpec((tm, GROUP_W), lambda i: (i, 0))
    row = pl.BlockSpec((tm, D_MODEL), lambda i: (i, 0))
    in_specs = [yspec, yspec, yspec, yspec,
                pl.BlockSpec((None, D_MODEL, D_MODEL), lambda i: (layer, 0, 0)),
                row,
                pl.BlockSpec((None, 1, D_MODEL), lambda i: (layer, 0, 0)),
                pl.BlockSpec((tm, 1), lambda i: (i, 0))]
    args = list(ys) + [w_all, x, post_all, rowmask]
    if by_sequence:
        assert tm == BATCH * CHUNK and not with_next
        out_specs = [pl.BlockSpec((BATCH, None, CHUNK, D_MODEL), lambda i: (0, jnp.maximum(i - 1, 0), 0, 0))]
        out_shape = [jax.ShapeDtypeStruct((BATCH, SEQ // CHUNK, CHUNK, D_MODEL), F32)]
    else:
        out_specs = [row]
        out_shape = [jax.ShapeDtypeStruct((n, D_MODEL), F32)]
    if with_next:
        in_specs.append(pl.BlockSpec((None, 1, D_MODEL), lambda i: (layer + 1, 0, 0)))
        args.append(pre_all)
        out_specs.append(row)
        out_shape.append(jax.ShapeDtypeStruct((n, D_MODEL), BF16))
    res = pl.pallas_call(
        functools.partial(_outproj_kernel, with_next, by_sequence),
        grid=(n // tm,), in_specs=in_specs, out_specs=out_specs, out_shape=out_shape,
        compiler_params=pltpu.CompilerParams(
            dimension_semantics=("arbitrary",), vmem_limit_bytes=VMEM_LIMIT),
        name="outproj",
    )(*args)
    return (res[0], res[1]) if with_next else (res[0], None)


def _init_state(c, has_init, s_scr, s0_ref, ext=None, c0_ref=None):
    @pl.when(c == 0)
    def _():
        if has_init:
            s_scr[...] = s0_ref[...]
            if ext is not None:
                ext[:, 0:SUBLANES, :] = c0_ref[...]
        else:
            s_scr[...] = jnp.zeros_like(s_scr)
            if ext is not None:
                ext[:, 0:SUBLANES, :] = jnp.zeros((ext.shape[0], SUBLANES, ext.shape[2]), F32)


def _run_staggered(groups, skew):
    pending, live, tick = list(groups), [], 0
    while pending or live:
        if pending and tick % skew == 0:
            live.append(pending.pop(0))
        for gen in list(live):
            if next(gen, "done") == "done":
                live.remove(gen)
        tick += 1


def _ssd_stages(nseq, rows, layer, has_init, state_t, nchunks, first_lo, hi, *refs):
    if has_init:
        (z_ref, x_ref, bc_ref, sm_ref, s0_ref, c0_ref, _, cw_ref, cb_ref, lp_ref, nw_ref, d_ref,
         y_ref, so_ref, s_scr, ext) = refs
    else:
        (z_ref, x_ref, bc_ref, sm_ref, cw_ref, cb_ref, lp_ref, nw_ref, d_ref,
         y_ref, so_ref, s_scr, ext) = refs
        s0_ref = c0_ref = None
    c = pl.program_id(1)
    m = rows
    _init_state(c, has_init, s_scr, s0_ref, ext, c0_ref)

    valid = _valid_rows(m, c, first_lo, hi)
    lane = lax.broadcasted_iota(jnp.int32, (1, CHUNK), 1)
    head_lane = (lane >= LANE_DT) & (lane < LANE_DT + SSD_HEADS)
    dt_all = _softplus(sm_ref[...] + lp_ref[ROW_SSD_DTB:ROW_SSD_DTB + 1, :])
    la_all = -jnp.exp(lp_ref[ROW_SSD_ALOG:ROW_SSD_ALOG + 1, :]) * dt_all
    dts = [jnp.where(valid, dt_all[b * m:(b + 1) * m], 0.0) for b in range(nseq)]
    la, stride = _pack_lanes([jnp.where(valid & head_lane, la_all[b * m:(b + 1) * m], 0.0) for b in range(nseq)])
    cum, cum_t, e_cum, e_rem, e_last = _decay_terms(la)

    shift_mat = _shift_rows_matrix(m) if m > SUBLANES else None
    hpg = SSD_HEADS // SSD_GROUPS
    c_off = GROUP_W + SSD_GROUPS * SSD_STATE
    gw = GROUP_W // SSD_GROUPS
    ys = [None] * nseq

    def sequence(b):
        rows_b = slice(b * m, (b + 1) * m)
        cur = jnp.concatenate([x_ref[rows_b, :], bc_ref[rows_b, :]], axis=1)
        a_b = _silu(_causal_conv(ext.at[b], cur, cw_ref, shift_mat) + cb_ref[...])
        yield
        scores = []
        for g in range(SSD_GROUPS):
            b_g = a_b[:, GROUP_W + g * SSD_STATE:GROUP_W + (g + 1) * SSD_STATE]
            c_g = a_b[:, c_off + g * SSD_STATE:c_off + (g + 1) * SSD_STATE]
            scores.append((_dot_nt(c_g, _pad_rows(b_g)), b_g, c_g))
        yield
        heads = []
        for h in range(SSD_HEADS):
            sc, b_g, c_g = scores[h // hpg]
            ln = b * stride + LANE_DT + h
            x_h = a_b[:, h * SSD_HEAD_DIM:(h + 1) * SSD_HEAD_DIM]
            v_h = x_h * _col(dts[b], LANE_DT + h, SSD_HEAD_DIM)
            s_h = s_scr[b, h]
            v_w = v_h * _col(e_rem, ln, SSD_HEAD_DIM)
            if state_t:
                from_state = _dot_nt(c_g, s_h)
                s_scr[b, h] = s_h * e_last[:, ln:ln + 1] + _dot_tn(v_w, b_g)
            else:
                from_state = _dot(c_g, s_h)
                s_scr[b, h] = s_h * e_last[:, ln:ln + 1] + _dot_tn(b_g, v_w)
            o = (_dot(sc * _decay_matrix(cum, cum_t, ln), _pad_rows(v_h))
                 + from_state * _col(e_cum, ln, SSD_HEAD_DIM))
            heads.append(o + d_ref[layer, h] * x_h)
            yield
        y = jnp.concatenate(heads, axis=1) * _silu(z_ref[rows_b, :])
        outs = [_rms(y[:, g * gw:(g + 1) * gw], nw_ref[:, g * gw:(g + 1) * gw]) for g in range(SSD_GROUPS)]
        ys[b] = jnp.concatenate(outs, axis=1)

    def finish():
        y_ref[...] = jnp.concatenate(ys, axis=0).astype(y_ref.dtype)

        @pl.when(c == nchunks - 1)
        def _():
            so_ref[...] = s_scr[...]

    return [sequence(b) for b in range(nseq)], finish


def _ret_stages(nseq, seq_group, rows, has_init, nchunks, first_lo, hi, *refs):
    if has_init:
        qk_ref, v_ref, g_ref, s0_ref, _, lp_ref, nw_ref, y_ref, so_ref, s_scr = refs
    else:
        qk_ref, v_ref, g_ref, lp_ref, nw_ref, y_ref, so_ref, s_scr = refs
        s0_ref = None
    c = pl.program_id(1)
    m = rows
    _init_state(c, has_init, s_scr, s0_ref)

    valid = _valid_rows(m, c, first_lo, hi)
    lane = lax.broadcasted_iota(jnp.int32, (1, CHUNK), 1)
    log_gamma = jnp.broadcast_to(lp_ref[ROW_LOG_GAMMA:ROW_LOG_GAMMA + 1, :], (m, CHUNK))
    la_one = jnp.where(valid & (lane < RET_HEADS), log_gamma, 0.0)
    la, stride = _pack_lanes([la_one] * nseq)
    cum, cum_t, e_cum, e_rem, e_last = _decay_terms(la)

    qk_all = qk_ref[...]
    v_all = v_ref[...]
    all_units = []
    for b in range(nseq):
        qk = qk_all[b * m:(b + 1) * m]
        v = jnp.where(valid, v_all[b * m:(b + 1) * m], 0.0)
        for h in range(RET_HEADS):
            q_h = qk[:, h * RET_DK:(h + 1) * RET_DK]
            k_h = qk[:, (RET_HEADS + h) * RET_DK:(RET_HEADS + h + 1) * RET_DK] * (RET_DK ** -0.5)
            s_h = s_scr[b, h]
            all_units.append(dict(b=b, h=h, ln=b * stride + h, k=k_h, v=v[:, h * RET_DV:(h + 1) * RET_DV],
                                  s=s_h, raw=_dot_nt(q_h, _pad_rows(k_h)), q_s=_dot(q_h, s_h)))
    ys = [None] * nseq

    def group(b0):
        units = all_units[b0 * RET_HEADS:(b0 + seq_group) * RET_HEADS]
        scs = [u["raw"] * _decay_matrix(cum, cum_t, u["ln"]) for u in units]
        yield
        os_ = [_dot(sc, _pad_rows(u["v"])) + u["q_s"] * _col(e_cum, u["ln"], RET_DV)
               for sc, u in zip(scs, units)]
        yield
        for u in units:
            ln = u["ln"]
            s_scr[u["b"], u["h"]] = (u["s"] * e_last[:, ln:ln + 1]
                                     + _dot_tn(u["k"] * _col(e_rem, ln, RET_DK), u["v"]))
        yield
        for i in range(seq_group):
            heads = [_rms(os_[i * RET_HEADS + h], nw_ref[:, h * RET_DV:(h + 1) * RET_DV])
                     for h in range(RET_HEADS)]
            rows_b = slice((b0 + i) * m, (b0 + i + 1) * m)
            ys[b0 + i] = jnp.concatenate(heads, axis=1) * _silu(g_ref[rows_b, :])

    def finish():
        y_ref[...] = jnp.concatenate(ys, axis=0).astype(y_ref.dtype)

        @pl.when(c == nchunks - 1)
        def _():
            so_ref[...] = s_scr[...]

    return [group(b0) for b0 in range(0, nseq, seq_group)], finish


def _gdn_stages(nseq, seq_group, rows, has_init, nchunks, first_lo, hi, *refs):
    if has_init:
        (q_ref, k_ref, v_ref, g_ref, sm_ref, s0_ref, c0_ref, _, cw_ref, lp_ref, nw_ref,
         y_ref, so_ref, s_scr, ext) = refs
    else:
        (q_ref, k_ref, v_ref, g_ref, sm_ref, cw_ref, lp_ref, nw_ref,
         y_ref, so_ref, s_scr, ext) = refs
        s0_ref = c0_ref = None
    c = pl.program_id(1)
    m = rows
    _init_state(c, has_init, s_scr, s0_ref, ext, c0_ref)

    shift_mat = _shift_rows_matrix(m) if m > SUBLANES else None
    valid = _valid_rows(m, c, first_lo, hi)
    lane = lax.broadcasted_iota(jnp.int32, (1, CHUNK), 1)
    head_lane = (lane >= LANE_DECAY) & (lane < LANE_DECAY + GDN_HEADS)
    sm = sm_ref[...]
    beta_all = _sigmoid(sm)
    la_all = (-jnp.exp(lp_ref[ROW_GDN_ALOG:ROW_GDN_ALOG + 1, :])
              * _softplus(sm + lp_ref[ROW_GDN_DTB:ROW_GDN_DTB + 1, :]))
    betas = [jnp.where(valid, beta_all[b * m:(b + 1) * m], 0.0) for b in range(nseq)]
    la, stride = _pack_lanes([jnp.where(valid & head_lane, la_all[b * m:(b + 1) * m], 0.0) for b in range(nseq)])
    cum, cum_t, e_cum, e_rem, e_last = _decay_terms(la)
    row, col = _iotas(m)
    levels = max(1, (min(m, hi) - 1).bit_length())

    ys = [None] * nseq

    def joining_mask(shift):
        same_big = lax.shift_right_logical(row, shift + 1) == lax.shift_right_logical(col, shift + 1)
        same_small = lax.shift_right_logical(row, shift) == lax.shift_right_logical(col, shift)
        return same_big & jnp.logical_not(same_small) & (row > col)

    def group(b0):
        units = []
        for b in range(b0, b0 + seq_group):
            rows_b = slice(b * m, (b + 1) * m)
            cur = jnp.concatenate([q_ref[rows_b, :], k_ref[rows_b, :], v_ref[rows_b, :]], axis=1)
            a_b = _silu(_causal_conv(ext.at[b], cur, cw_ref, shift_mat))
            yield
            for h in range(GDN_HEADS):
                ln = b * stride + LANE_DECAY + h
                q_h = a_b[:, h * GDN_DK:(h + 1) * GDN_DK]
                k_h = a_b[:, GROUP_W + h * GDN_DK:GROUP_W + (h + 1) * GDN_DK]
                v_h = a_b[:, 2 * GROUP_W + h * GDN_DV:2 * GROUP_W + (h + 1) * GDN_DV]
                q_h = q_h * lax.rsqrt(jnp.sum(q_h * q_h, axis=-1, keepdims=True) + 1e-6) * (GDN_DK ** -0.5)
                k_h = k_h * lax.rsqrt(jnp.sum(k_h * k_h, axis=-1, keepdims=True) + 1e-6)
                k_h = jnp.where(valid, k_h, 0.0)
                b_h = _col(betas[b], LANE_BETA + h, GDN_DK)
                k_b = k_h * b_h
                decay = _decay_matrix(cum, cum_t, ln)
                k_pad = _pad_rows(k_h)
                units.append(dict(
                    b=b, h=h, ln=ln, q=q_h, k=k_h, k_b=k_b, decay=decay,
                    a=jnp.where(row > col, _dot_nt(k_b, k_pad) * decay, 0.0),
                    v_b=_pad_rows(jnp.where(valid, v_h, 0.0) * b_h),
                    qk=_dot_nt(q_h, k_pad) * decay))
                yield
        eye = (row == col).astype(F32)
        mask = joining_mask(0)
        ts = [eye - jnp.where(mask, u["a"], 0.0) for u in units]
        for shift in range(1, levels):
            mask = joining_mask(shift)
            inner = [_dot(jnp.where(mask, u["a"], 0.0), _pad_rows(t)) for u, t in zip(units, ts)]
            yield
            ts = [t - _dot(t, _pad_rows(x)) for t, x in zip(ts, inner)]
            yield
        us = [_dot(t, u["v_b"]) for t, u in zip(ts, units)]
        ws = [_dot(t, _pad_rows(u["k_b"] * _col(e_cum, u["ln"], GDN_DK))) for t, u in zip(ts, units)]
        yield
        ss = [s_scr[u["b"], u["h"]] for u in units]
        v_news = [x - _dot(w, s_h) for x, w, s_h in zip(us, ws, ss)]
        yield
        os_ = [_dot(u["q"] * _col(e_cum, u["ln"], GDN_DK), s_h) + _dot(u["qk"], _pad_rows(v_new))
               for u, s_h, v_new in zip(units, ss, v_news)]
        yield
        for u, s_h, v_new in zip(units, ss, v_news):
            ln = u["ln"]
            s_scr[u["b"], u["h"]] = (s_h * e_last[:, ln:ln + 1]
                                     + _dot_tn(u["k"] * _col(e_rem, ln, GDN_DK), v_new))
        yield
        for i in range(seq_group):
            heads = [_rms(o, nw_ref[...]) for o in os_[i * GDN_HEADS:(i + 1) * GDN_HEADS]]
            rows_b = slice((b0 + i) * m, (b0 + i + 1) * m)
            ys[b0 + i] = jnp.concatenate(heads, axis=1) * _silu(g_ref[rows_b, :])

    def finish():
        y_ref[...] = jnp.concatenate(ys, axis=0).astype(y_ref.dtype)

        @pl.when(c == nchunks - 1)
        def _():
            so_ref[...] = s_scr[...]

    return [group(b0) for b0 in range(0, nseq, seq_group)], finish


def _swa_stages(nseq, seq_group, rows, layer, is_prompt, *refs):
    if is_prompt:
        q_ref, g_ref, kvc_ref, kvp_ref, sink_ref, y_ref = refs
    else:
        q_ref, g_ref, kvc_ref, kp_ref, vp_ref, sink_ref, y_ref = refs
    n = pl.program_id(1)
    m = rows
    if is_prompt:
        lo_prev = jnp.where(n == 0, CHUNK, jnp.where(n == 1, PROMPT_PAD, 0))
        lo_cur = jnp.where(n == 0, PROMPT_PAD, 0)
    else:
        lo_prev, lo_cur = 0, 0
    qi = lax.broadcasted_iota(jnp.int32, (m, 2 * CHUNK), 0)
    kj = lax.broadcasted_iota(jnp.int32, (m, 2 * CHUNK), 1)
    dist = qi + WINDOW - kj
    key_ok = ((kj < CHUNK) & (kj >= lo_prev)) | (kj >= CHUNK + lo_cur)
    visible = (dist >= 0) & (dist <= WINDOW) & key_ok
    dist_f = dist.astype(F32)

    grp = SWA_HEADS // SWA_KV_HEADS
    ys = [None] * nseq

    def group(b0):
        units = []
        for b in range(b0, b0 + seq_group):
            q = q_ref[b * m:(b + 1) * m, :]
            kvc = _pad_rows(kvc_ref[b * m:(b + 1) * m, :])
            if is_prompt:
                k_prev = kvp_ref[b * CHUNK:(b + 1) * CHUNK, 0:KV_W]
                v_prev = kvp_ref[b * CHUNK:(b + 1) * CHUNK, KV_W:2 * KV_W]
            else:
                k_prev, v_prev = kp_ref[b], vp_ref[b]
            k_all = jnp.concatenate([k_prev, kvc[:, 0:KV_W]], axis=0)
            v_all = jnp.concatenate([v_prev, kvc[:, KV_W:2 * KV_W]], axis=0)
            for h in range(SWA_HEADS):
                kvh = h // grp
                units.append(dict(h=h, q=q[:, h * SWA_HEAD_DIM:(h + 1) * SWA_HEAD_DIM],
                                  k=k_all[:, kvh * SWA_HEAD_DIM:(kvh + 1) * SWA_HEAD_DIM],
                                  v=v_all[:, kvh * SWA_HEAD_DIM:(kvh + 1) * SWA_HEAD_DIM]))
        ss = [jnp.where(visible,
                        _dot_nt(u["q"], u["k"]) * (SWA_HEAD_DIM ** -0.5)
                        - 2.0 ** (-8.0 * (u["h"] + 1) / SWA_HEADS) * dist_f, -1e30) for u in units]
        yield
        mxs = [jnp.maximum(jnp.max(s, axis=-1, keepdims=True), sink_ref[layer, u["h"]])
               for s, u in zip(ss, units)]
        es = [jnp.exp(s - mx) for s, mx in zip(ss, mxs)]
        yield
        dens = [jnp.sum(e, axis=-1, keepdims=True) + jnp.exp(sink_ref[layer, u["h"]] - mx)
                for e, mx, u in zip(es, mxs, units)]
        os_ = [_dot(e, u["v"]) / den for e, u, den in zip(es, units, dens)]
        yield
        for i in range(seq_group):
            rows_b = slice((b0 + i) * m, (b0 + i + 1) * m)
            ys[b0 + i] = (jnp.concatenate(os_[i * SWA_HEADS:(i + 1) * SWA_HEADS], axis=1)
                          * _silu(g_ref[rows_b, :]))

    def finish():
        y_ref[...] = jnp.concatenate(ys, axis=0).astype(y_ref.dtype)

    return [group(b0) for b0 in range(0, nseq, seq_group)], finish


def _mixer_kernel(parts, *refs):
    starts = [0, sum(p[1] for p in parts), sum(p[1] + p[2] for p in parts)]
    gens, finishes = [], []
    for stages, n_in, n_out, n_scr in parts:
        mine = []
        for k, cnt in enumerate((n_in, n_out, n_scr)):
            mine += refs[starts[k]:starts[k] + cnt]
            starts[k] += cnt
        g, fin = stages(*mine)
        gens.append(g)
        finishes.append(fin)
    order = []
    for i in range(max(len(g) for g in gens)):
        order += [g[i] for g in gens if i < len(g)]
    _run_staggered(order, 1)
    for fin in finishes:
        fin()


def _geometry(sample):
    if sample:
        return (DEC_BATCH // SAMPLE_SEQS, 1), SAMPLE_SEQS, SUBLANES, N_SAMPLE, 0, DEC_SEQ
    return (1, PROMPT_CHUNKS), BATCH, CHUNK, N_PROMPT, PROMPT_PAD, CHUNK


def _rowblk(sample):
    return (lambda blk, c: blk) if sample else (lambda blk, c: c)


def _pcol(nrows, rowblk, width, col):
    assert col % width == 0
    return pl.BlockSpec((nrows, width), lambda blk, c: (rowblk(blk, c), col // width))


def _layer_param(shape, layer):
    return pl.BlockSpec((None,) + shape, lambda blk, c: (layer,) + (0,) * len(shape))


def _layer_seqs(nseq, shape, layer):
    return pl.BlockSpec((None, nseq) + shape, lambda blk, c: (layer, blk) + (0,) * len(shape))


def _seqs_out(nseq, shape):
    return pl.BlockSpec((nseq,) + shape, lambda blk, c: (blk,) + (0,) * len(shape))


_SMEM = pl.BlockSpec(memory_space=pltpu.SMEM)
_MIXER_PARAMS = pltpu.CompilerParams(dimension_semantics=("arbitrary", "arbitrary"),
                                     vmem_limit_bytes=LARGE_VMEM_LIMIT)


def _state_output(sample, nseq, state, layer, acc, in_specs, args):
    if not sample:
        return _seqs_out(nseq, state), jax.ShapeDtypeStruct((nseq,) + state, F32), {}
    in_specs.append(pl.BlockSpec(memory_space=pl.ANY))
    args.append(acc)
    spec = pl.BlockSpec((None, nseq) + state, lambda blk, c: (layer, blk) + (0,) * len(state))
    return spec, jax.ShapeDtypeStruct(acc.shape, F32), {len(args) - 1: 1}


def _ssd(p, prm, layer, init, acc=None):
    sample = init is not None
    grid, nseq, rows, total, first_lo, hi = _geometry(sample)
    rb, n = _rowblk(sample), nseq * rows
    state = (SSD_HEADS, SSD_HEAD_DIM, SSD_STATE) if sample else (SSD_HEADS, SSD_STATE, SSD_HEAD_DIM)
    in_specs = [_pcol(n, rb, GROUP_W, COL_Z), _pcol(n, rb, GROUP_W, COL_X),
                _pcol(n, rb, GROUP_W, COL_BC), _pcol(n, rb, CHUNK, COL_SMALL)]
    args = [p, p, p, p]
    if sample:
        in_specs += [_layer_seqs(nseq, state, layer), _layer_seqs(nseq, (SUBLANES, SSD_CONV_W), layer)]
        args += list(init)
    state_spec, state_shape, aliases = _state_output(sample, nseq, state, layer, acc, in_specs, args)
    in_specs += [_layer_param((CONV_K, SSD_CONV_W), layer), _layer_param((1, SSD_CONV_W), layer),
                 _layer_param((SUBLANES, CHUNK), layer), _layer_param((1, GROUP_W), layer), _SMEM]
    args += [prm["ssd_conv_w"], prm["ssd_conv_b"], prm["lanes"], prm["ssd_norm"], prm["ssd_d"]]
    return dict(
        name="ssd", grid=grid,
        stages=functools.partial(_ssd_stages, nseq, rows, layer, sample, sample, grid[1], first_lo, hi),
        in_specs=in_specs, args=args,
        out_specs=[pl.BlockSpec((n, GROUP_W), lambda blk, c: (rb(blk, c), 0)), state_spec],
        out_shape=[jax.ShapeDtypeStruct((total, GROUP_W), BF16), state_shape],
        aliases=aliases,
        scratch=[pltpu.VMEM((nseq,) + state, F32), pltpu.VMEM((nseq, 2 * SUBLANES, SSD_CONV_W), F32)])


def _ret(p, prm, layer, init, acc=None):
    sample = init is not None
    grid, nseq, rows, total, first_lo, hi = _geometry(sample)
    rb, n = _rowblk(sample), nseq * rows
    state = (RET_HEADS, RET_DK, RET_DV)
    in_specs = [_pcol(n, rb, GROUP_W, COL_QKR), _pcol(n, rb, GROUP_W, COL_VR), _pcol(n, rb, GROUP_W, COL_GR)]
    args = [p, p, p]
    if sample:
        in_specs += [_layer_seqs(nseq, state, layer)]
        args += list(init)
    state_spec, state_shape, aliases = _state_output(sample, nseq, state, layer, acc, in_specs, args)
    in_specs += [_layer_param((SUBLANES, CHUNK), layer), _layer_param((1, GROUP_W), layer)]
    args += [prm["lanes"], prm["ret_norm"]]
    return dict(
        name="ret", grid=grid,
        stages=functools.partial(_ret_stages, nseq, nseq if sample else 1, rows, sample, grid[1], first_lo, hi),
        in_specs=in_specs, args=args,
        out_specs=[pl.BlockSpec((n, GROUP_W), lambda blk, c: (rb(blk, c), 0)), state_spec],
        out_shape=[jax.ShapeDtypeStruct((total, GROUP_W), BF16), state_shape],
        aliases=aliases,
        scratch=[pltpu.VMEM((nseq,) + state, F32)])


def _gdn(p, prm, layer, init, acc=None):
    sample = init is not None
    grid, nseq, rows, total, first_lo, hi = _geometry(sample)
    rb, n = _rowblk(sample), nseq * rows
    state = (GDN_HEADS, GDN_DK, GDN_DV)
    in_specs = [_pcol(n, rb, GROUP_W, COL_QD), _pcol(n, rb, GROUP_W, COL_KD), _pcol(n, rb, GROUP_W, COL_VD),
                _pcol(n, rb, GROUP_W, COL_GD), _pcol(n, rb, CHUNK, COL_SMALL)]
    args = [p, p, p, p, p]
    if sample:
        in_specs += [_layer_seqs(nseq, state, layer), _layer_seqs(nseq, (SUBLANES, GDN_CONV_W), layer)]
        args += list(init)
    state_spec, state_shape, aliases = _state_output(sample, nseq, state, layer, acc, in_specs, args)
    in_specs += [_layer_param((CONV_K, GDN_CONV_W), layer), _layer_param((SUBLANES, CHUNK), layer),
                 _layer_param((1, GDN_DV), layer)]
    args += [prm["gdn_conv_w"], prm["lanes"], prm["gdn_norm"]]
    return dict(
        name="gdn", grid=grid,
        stages=functools.partial(_gdn_stages, nseq, 2 if sample else 1, rows, sample, grid[1], first_lo, hi),
        in_specs=in_specs, args=args,
        out_specs=[pl.BlockSpec((n, GROUP_W), lambda blk, c: (rb(blk, c), 0)), state_spec],
        out_shape=[jax.ShapeDtypeStruct((total, GROUP_W), BF16), state_shape],
        aliases=aliases,
        scratch=[pltpu.VMEM((nseq,) + state, F32), pltpu.VMEM((nseq, 2 * SUBLANES, GDN_CONV_W), F32)])


def _swa(p, prm, layer, cache):
    sample = cache is not None
    grid, nseq, rows, total, _, _ = _geometry(sample)
    rb, n = _rowblk(sample), nseq * rows
    in_specs = [_pcol(n, rb, GROUP_W, COL_QA), _pcol(n, rb, GROUP_W, COL_GA), _pcol(n, rb, 2 * KV_W, COL_KV)]
    args = [p, p, p]
    if sample:
        in_specs += [_layer_seqs(nseq, (WINDOW, KV_W), layer), _layer_seqs(nseq, (WINDOW, KV_W), layer)]
        args += list(cache)
    else:
        in_specs += [pl.BlockSpec((n, 2 * KV_W), lambda blk, c: (jnp.maximum(c - 1, 0), COL_KV // (2 * KV_W)))]
        args += [p]
    in_specs += [_SMEM]
    args += [prm["swa_sinks"]]
    return dict(
        name="swa", grid=grid,
        stages=functools.partial(_swa_stages, nseq, nseq if sample else 1, rows, layer, not sample),
        in_specs=in_specs, args=args,
        out_specs=[pl.BlockSpec((n, GROUP_W), lambda blk, c: (rb(blk, c), 0))],
        out_shape=[jax.ShapeDtypeStruct((total, GROUP_W), BF16)],
        aliases={}, scratch=[])


def _run_mixers(parts, suffix):
    kernel_parts = tuple((p["stages"], len(p["in_specs"]), len(p["out_specs"]), len(p["scratch"]))
                         for p in parts)
    aliases, n_in, n_out = {}, 0, 0
    for p in parts:
        aliases.update({n_in + k: n_out + v for k, v in p["aliases"].items()})
        n_in += len(p["in_specs"])
        n_out += len(p["out_specs"])
    res = pl.pallas_call(
        functools.partial(_mixer_kernel, kernel_parts),
        grid=parts[0]["grid"],
        in_specs=[s for p in parts for s in p["in_specs"]],
        out_specs=[s for p in parts for s in p["out_specs"]],
        out_shape=[s for p in parts for s in p["out_shape"]],
        scratch_shapes=[s for p in parts for s in p["scratch"]],
        input_output_aliases=aliases,
        compiler_params=_MIXER_PARAMS,
        name="_".join(p["name"] for p in parts) + suffix,
    )(*[a for p in parts for a in p["args"]])
    outs, k = [], 0
    for p in parts:
        outs.append(list(res[k:k + len(p["out_specs"])]))
        k += len(p["out_specs"])
    return outs


def _lane_table(ssd_dt_bias, ssd_a_log, gdn_dt_bias, gdn_a_log):
    def put(v, lane):
        return jnp.pad(v.astype(F32), ((0, 0), (lane, CHUNK - lane - v.shape[1])))
    log_gamma = jnp.log1p(-jnp.exp2(-5.0 - jnp.arange(RET_HEADS, dtype=F32)))
    rows = [put(ssd_dt_bias, LANE_DT), put(ssd_a_log, LANE_DT), put(gdn_dt_bias, LANE_DECAY),
            put(gdn_a_log, LANE_DECAY), put(jnp.broadcast_to(log_gamma, (DEPTH, RET_HEADS)), 0)]
    rows += [jnp.zeros((DEPTH, CHUNK), F32)] * (SUBLANES - len(rows))
    return jnp.stack(rows, axis=1)


def _conv_state_pad(state):
    return jnp.pad(state, ((0, 0), (0, 0), (SUBLANES - (CONV_K - 1), 0), (0, 0)))


def kernel(x_prompt, x_sample, state_ssd, state_ssd_conv, cache_swa_k, cache_swa_v, state_ret, state_gdn,
           state_gdn_conv, meta_tokens, pre_norm, post_norm, w_in, w_out, ssd_conv_w, ssd_conv_b, ssd_dt_bias,
           ssd_a_log, ssd_d, ssd_norm, swa_sinks, ret_norm, gdn_conv_w, gdn_dt_bias, gdn_a_log, gdn_norm):
    xs = jnp.pad(x_sample, ((0, 0), (0, SUBLANES - DEC_SEQ), (0, 0))).reshape(N_SAMPLE, D_MODEL)
    mask_np = np.ones((PROMPT_CHUNKS, BATCH, CHUNK), np.float32)
    mask_np[0, :, :PROMPT_PAD] = 0.0
    mask_p = jnp.asarray(mask_np.reshape(N_PROMPT, 1))
    mask_s = jnp.asarray(np.tile(np.arange(SUBLANES) < DEC_SEQ, DEC_BATCH)[:, None], F32)

    w_in_p = _wprep(jnp.swapaxes(w_in, 1, 2))
    w_out_b = w_out.astype(BF16)
    pre_g = pre_norm.reshape(DEPTH, 1, D_MODEL)
    post_g = post_norm.reshape(DEPTH, 1, D_MODEL)
    prm = dict(
        lanes=_lane_table(ssd_dt_bias, ssd_a_log, gdn_dt_bias, gdn_a_log),
        ssd_conv_w=ssd_conv_w, ssd_conv_b=ssd_conv_b.reshape(DEPTH, 1, SSD_CONV_W),
        ssd_norm=ssd_norm.reshape(DEPTH, 1, GROUP_W), ssd_d=ssd_d, swa_sinks=swa_sinks,
        ret_norm=ret_norm.reshape(DEPTH, 1, GROUP_W), gdn_conv_w=gdn_conv_w,
        gdn_norm=gdn_norm.reshape(DEPTH, 1, GDN_DV))
    ssd_c0 = _conv_state_pad(state_ssd_conv)
    gdn_c0 = _conv_state_pad(state_gdn_conv)
    cache_k = cache_swa_k.reshape(DEPTH, DEC_BATCH, WINDOW, KV_W)
    cache_v = cache_swa_v.reshape(DEPTH, DEC_BATCH, WINDOW, KV_W)

    names = ("ssd", "ssd_conv", "swa_k", "swa_v", "ret", "gdn", "gdn_conv")
    p_out = {k: [] for k in names}
    s_out = {k: [] for k in ("ssd_conv", "swa_k", "swa_v", "gdn_conv")}
    tail = slice(CHUNK - (CONV_K - 1), CHUNK)
    state_ssd_t = jnp.swapaxes(state_ssd, 3, 4)
    s_ssd_t = jnp.zeros(state_ssd_t.shape, F32)
    s_ret = jnp.zeros(state_ret.shape, F32)
    s_gdn = jnp.zeros(state_gdn.shape, F32)
    xp, hp = _embed_prompt(x_prompt, meta_tokens, pre_g)
    hs = _prenorm(xs, pre_g, 0, N_SAMPLE)
    for l in range(DEPTH):
        pp = _inproj(hp, w_in_p, l, INPROJ_TM)
        (ya, p_ssd), (yd, p_gdn), (yb,), (yc, p_ret) = _run_mixers(
            [_ssd(pp, prm, l, None), _gdn(pp, prm, l, None), _swa(pp, prm, l, None), _ret(pp, prm, l, None)],
            "_prompt")
        xp, hp = _outproj((ya, yb, yc, yd), w_out_b, xp, post_g, mask_p, pre_g, l, PROMPT_TM,
                          by_sequence=l == DEPTH - 1)

        ps = _inproj(hs, w_in_p, l, N_SAMPLE)
        (ya, s_ssd_t), (yd, s_gdn), (yb,), (yc, s_ret) = _run_mixers(
            [_ssd(ps, prm, l, (state_ssd_t, ssd_c0), s_ssd_t), _gdn(ps, prm, l, (state_gdn, gdn_c0), s_gdn),
             _swa(ps, prm, l, (cache_k, cache_v)), _ret(ps, prm, l, (state_ret,), s_ret)],
            "_sample")
        xs, hs = _outproj((ya, yb, yc, yd), w_out_b, xs, post_g, mask_s, pre_g, l, N_SAMPLE)

        pp4 = pp.reshape(PROMPT_CHUNKS, BATCH, CHUNK, IN_W_PAD)

        def last_chunk(r0, c0, c1):
            return lax.slice(pp4, (PROMPT_CHUNKS - 1, 0, r0, c0), (PROMPT_CHUNKS, BATCH, CHUNK, c1))[0]

        ps3 = ps.reshape(DEC_BATCH, SUBLANES, IN_W_PAD)
        p_out["ssd"].append(p_ssd)
        p_out["ssd_conv"].append(last_chunk(tail.start, COL_X, COL_X + SSD_CONV_W))
        p_out["swa_k"].append(last_chunk(0, COL_KV, COL_KV + KV_W))
        p_out["swa_v"].append(last_chunk(0, COL_KV + KV_W, COL_KV + 2 * KV_W))
        p_out["ret"].append(p_ret)
        p_out["gdn"].append(p_gdn)
        p_out["gdn_conv"].append(last_chunk(tail.start, COL_QD, COL_QD + GDN_CONV_W))
        s_out["ssd_conv"].append(ps3[:, 1:DEC_SEQ, COL_X:COL_X + SSD_CONV_W])
        s_out["swa_k"].append(ps3[:, :DEC_SEQ, COL_KV:COL_KV + KV_W])
        s_out["swa_v"].append(ps3[:, :DEC_SEQ, COL_KV + KV_W:COL_KV + 2 * KV_W])
        s_out["gdn_conv"].append(ps3[:, 1:DEC_SEQ, COL_QD:COL_QD + GDN_CONV_W])

    p_st = {k: jnp.stack(v) for k, v in p_out.items()}
    s_st = {k: jnp.stack(v) for k, v in s_out.items()}
    kv_shape = (SWA_KV_HEADS, SWA_HEAD_DIM)
    y_prompt = xp.reshape(BATCH, SEQ, D_MODEL)
    y_sample = xs.reshape(DEC_BATCH, SUBLANES, D_MODEL)[:, :DEC_SEQ]
    return (
        y_prompt, y_sample,
        p_st["ssd"], p_st["ssd_conv"],
        p_st["swa_k"].reshape((DEPTH, BATCH, WINDOW) + kv_shape),
        p_st["swa_v"].reshape((DEPTH, BATCH, WINDOW) + kv_shape),
        p_st["ret"], p_st["gdn"], p_st["gdn_conv"],
        jnp.swapaxes(s_ssd_t, 3, 4), s_st["ssd_conv"],
        jnp.concatenate([cache_swa_k[:, :, DEC_SEQ:], s_st["swa_k"].reshape((DEPTH, DEC_BATCH, DEC_SEQ) + kv_shape)], axis=2),
        jnp.concatenate([cache_swa_v[:, :, DEC_SEQ:], s_st["swa_v"].reshape((DEPTH, DEC_BATCH, DEC_SEQ) + kv_shape)], axis=2),
        s_ret, s_gdn, s_st["gdn_conv"],
    )
```

```python
import functools

import jax
import jax.numpy as jnp
import numpy as np
from jax import lax
from jax.experimental import pallas as pl
from jax.experimental.pallas import tpu as pltpu

F32 = jnp.float32
BF16 = jnp.bfloat16
HIGHEST = lax.Precision.HIGHEST

D_MODEL = 2048
BATCH = 4
SEQ = 2048
DEPTH = 4
DEC_BATCH = 32
DEC_SEQ = 4
N_META = 16
GROUP_W = 512
CONV_K = 4
NORM_EPS = 1e-6
WINDOW = 128

SSD_HEADS, SSD_HEAD_DIM, SSD_GROUPS, SSD_STATE = 8, 64, 2, 128
SWA_HEADS, SWA_KV_HEADS, SWA_HEAD_DIM = 8, 2, 64
RET_HEADS, RET_DK, RET_DV = 4, 64, 128
GDN_HEADS, GDN_DK, GDN_DV = 4, 128, 128
SSD_CONV_W = GROUP_W + 2 * SSD_GROUPS * SSD_STATE
GDN_CONV_W = 3 * GROUP_W
KV_W = SWA_KV_HEADS * SWA_HEAD_DIM

CHUNK = 128
SUBLANES = 8
PROMPT_PAD = CHUNK - N_META
PROMPT_ROWS = PROMPT_PAD + N_META + SEQ
PROMPT_CHUNKS = PROMPT_ROWS // CHUNK
N_PROMPT = BATCH * PROMPT_ROWS
N_SAMPLE = DEC_BATCH * SUBLANES
SAMPLE_SEQS = 8
MIXERS_PER_WAVE = 2

IN_W = 6416
IN_W_PAD = 6528
COL_Z, COL_X, COL_BC, COL_QA, COL_GA, COL_VR, COL_GR = 0, 512, 1024, 1536, 2048, 2560, 3072
COL_QD, COL_KD, COL_VD, COL_GD, COL_QKR, COL_KV, COL_SMALL = 3584, 4096, 4608, 5120, 5632, 6144, 6400
LANE_DT, LANE_BETA, LANE_DECAY = 0, 8, 12
ROW_SSD_DTB, ROW_SSD_ALOG, ROW_GDN_DTB, ROW_GDN_ALOG, ROW_LOG_GAMMA = 0, 1, 2, 3, 4

W_SEGMENTS = ((0, 1536), (1544, 2056), (2312, 2824), (3336, 3848), (3848, 4360), (4360, 5896), (5896, 6408),
              (2824, 3336), (2056, 2312))
W_BLOCKS = IN_W_PAD // CHUNK
W_PER_STEP = 3
W_STEPS = W_BLOCKS // W_PER_STEP
assert W_STEPS * W_PER_STEP == W_BLOCKS

PROJ_TN = 2176
PROMPT_TM = 512
INPROJ_TM = 1088
VMEM_LIMIT = 48 * 1024 * 1024
LARGE_VMEM_LIMIT = 56 * 1024 * 1024


def _sigmoid(x):
    return 1.0 / (1.0 + jnp.exp(-x))


def _silu(x):
    return x * _sigmoid(x)


def _softplus(x):
    return jnp.maximum(x, 0.0) + jnp.log1p(jnp.exp(-jnp.abs(x)))


def _dot(a, b):
    return jnp.dot(a.astype(BF16), b.astype(BF16), preferred_element_type=F32)


def _dot_nt(a, b):
    return lax.dot_general(a.astype(BF16), b.astype(BF16), (((1,), (1,)), ((), ())),
                           preferred_element_type=F32)


def _dot_tn(a, b):
    return lax.dot_general(a.astype(BF16), b.astype(BF16), (((0,), (0,)), ((), ())),
                           preferred_element_type=F32)


def _dot_f32(a, b):
    return jnp.dot(a, b, precision=HIGHEST, preferred_element_type=F32)


def _pad_rows(a, rows=CHUNK):
    if a.shape[0] == rows:
        return a
    return jnp.concatenate([a, jnp.zeros((rows - a.shape[0], a.shape[1]), a.dtype)], axis=0)


def _col(a, lane, width):
    return jnp.broadcast_to(a[:, lane:lane + 1], (a.shape[0], width))


def _iotas(m):
    row = lax.broadcasted_iota(jnp.int32, (m, CHUNK), 0)
    col = lax.broadcasted_iota(jnp.int32, (m, CHUNK), 1)
    return row, col


def _valid_rows(m, chunk_idx, first_lo, hi):
    r = lax.broadcasted_iota(jnp.int32, (m, 1), 0)
    lo = jnp.where(chunk_idx == 0, first_lo, 0)
    return (r >= lo) & (r < hi)


def _pack_lanes(per_seq):
    stride = CHUNK // len(per_seq)
    out = per_seq[0]
    for b in range(1, len(per_seq)):
        out = out + pltpu.roll(per_seq[b], b * stride, 1)
    return out, stride


def _decay_terms(la):
    m = la.shape[0]
    row, col = _iotas(m)
    lower = (row >= col).astype(F32)
    cum = _dot_f32(lower, _pad_rows(la))
    last = cum[m - 1:m, :]
    return cum, _pad_rows(cum).T, jnp.exp(cum), jnp.exp(last - cum), jnp.exp(last)


def _decay_matrix(cum, cum_t, lane):
    m = cum.shape[0]
    row, col = _iotas(m)
    keep = row >= col
    seg = _col(cum, lane, CHUNK) - jnp.broadcast_to(cum_t[lane:lane + 1, :], (m, CHUNK))
    return jnp.where(keep, jnp.exp(jnp.where(keep, seg, 0.0)), 0.0)


def _shift_rows_matrix(rows):
    r = lax.broadcasted_iota(jnp.int32, (rows, rows), 0)
    c = lax.broadcasted_iota(jnp.int32, (rows, rows), 1)
    blocks = [jnp.where(r - c == CONV_K - 1 - j, 1.0, 0.0) for j in range(CONV_K - 1)]
    return jnp.concatenate(blocks, axis=0).astype(BF16)


def _causal_conv(ext_ref, cur, w_ref, shift_mat):
    rows = cur.shape[0]
    w_last = w_ref[CONV_K - 1:CONV_K, :]
    ext_ref[SUBLANES:2 * SUBLANES, :] = cur[0:SUBLANES]
    head = cur[0:SUBLANES] * w_last
    for j in range(CONV_K - 1):
        start = SUBLANES - (CONV_K - 1) + j
        head = head + ext_ref[start:start + SUBLANES, :] * w_ref[j:j + 1, :]
    ext_ref[0:SUBLANES, :] = cur[rows - SUBLANES:rows]
    if rows == SUBLANES:
        return head
    shifted = jnp.dot(shift_mat, cur.astype(BF16), preferred_element_type=F32)
    acc = cur * w_last
    for j in range(CONV_K - 1):
        acc = acc + shifted[j * rows:(j + 1) * rows] * w_ref[j:j + 1, :]
    return jnp.concatenate([head, acc[SUBLANES:]], axis=0)


def _rms(x, w):
    ms = jnp.mean(x * x, axis=-1, keepdims=True)
    return x * lax.rsqrt(ms + NORM_EPS) * w


def _prenorm_kernel(x_ref, g_ref, h_ref):
    h_ref[...] = _rms(x_ref[...], g_ref[...]).astype(BF16)


def _prenorm(x, g_all, layer, tm):
    n = x.shape[0]
    return pl.pallas_call(
        _prenorm_kernel,
        grid=(n // tm,),
        in_specs=[pl.BlockSpec((tm, D_MODEL), lambda i: (i, 0)),
                  pl.BlockSpec((None, 1, D_MODEL), lambda i: (layer, 0, 0))],
        out_specs=pl.BlockSpec((tm, D_MODEL), lambda i: (i, 0)),
        out_shape=jax.ShapeDtypeStruct((n, D_MODEL), BF16),
        compiler_params=pltpu.CompilerParams(dimension_semantics=("arbitrary",)),
        name="prenorm",
    )(x, g_all)


def _wprep_kernel(offs_ref, a0_ref, a1_ref, a2_ref, b_ref, o_ref):
    j = pl.program_id(1)
    o_ref[0:CHUNK, :] = a0_ref[...].astype(BF16)
    o_ref[CHUNK:2 * CHUNK, :] = a1_ref[...].astype(BF16)

    @pl.when(j < W_STEPS - 1)
    def _():
        o_ref[2 * CHUNK:3 * CHUNK, :] = a2_ref[...].astype(BF16)

    @pl.when(j == W_STEPS - 1)
    def _():
        small = jnp.concatenate([a2_ref[0:SUBLANES, :], b_ref[...],
                                 jnp.zeros((CHUNK - 2 * SUBLANES, D_MODEL), F32)], axis=0)
        o_ref[2 * CHUNK:3 * CHUNK, :] = small.astype(BF16)


def _wprep(w_t):
    offs = np.concatenate([np.arange(a, b, CHUNK) for a, b in W_SEGMENTS] + [[1536]]).astype(np.int32)
    assert offs.shape[0] == W_BLOCKS and not (offs % SUBLANES).any()
    offs = offs // SUBLANES
    def source_block(k):
        return pl.BlockSpec(
            (pl.Element(CHUNK), pl.Element(D_MODEL)),
            lambda l, j, offs: ((l * (IN_W // SUBLANES) + offs[W_PER_STEP * j + k]) * SUBLANES, 0))

    grid_spec = pltpu.PrefetchScalarGridSpec(
        num_scalar_prefetch=1,
        grid=(DEPTH, W_STEPS),
        in_specs=[source_block(k) for k in range(W_PER_STEP)]
        + [pl.BlockSpec((SUBLANES, D_MODEL), lambda l, j, offs: ((l * IN_W + 6408) // SUBLANES, 0))],
        out_specs=pl.BlockSpec((None, W_PER_STEP * CHUNK, D_MODEL), lambda l, j, offs: (l, j, 0)),
    )
    w_rows = w_t.reshape(DEPTH * IN_W, D_MODEL)
    return pl.pallas_call(
        _wprep_kernel, grid_spec=grid_spec,
        out_shape=jax.ShapeDtypeStruct((DEPTH, IN_W_PAD, D_MODEL), BF16),
        compiler_params=pltpu.CompilerParams(dimension_semantics=("arbitrary", "arbitrary")),
        name="wprep",
    )(jnp.asarray(offs), w_rows, w_rows, w_rows, w_rows)


def _inproj_kernel(h_ref, w_ref, o_ref):
    o_ref[...] = lax.dot_general(h_ref[...], w_ref[...], (((1,), (1,)), ((), ())),
                                 preferred_element_type=F32)


def _inproj(h, w_all, layer, tm):
    n = h.shape[0]
    return pl.pallas_call(
        _inproj_kernel,
        grid=(IN_W_PAD // PROJ_TN, n // tm),
        in_specs=[
            pl.BlockSpec((tm, D_MODEL), lambda j, i: (i, 0)),
            pl.BlockSpec((None, PROJ_TN, D_MODEL), lambda j, i: (layer, j, 0)),
        ],
        out_specs=pl.BlockSpec((tm, PROJ_TN), lambda j, i: (i, j)),
        out_shape=jax.ShapeDtypeStruct((n, IN_W_PAD), F32),
        compiler_params=pltpu.CompilerParams(
            dimension_semantics=("arbitrary", "arbitrary"), vmem_limit_bytes=LARGE_VMEM_LIMIT),
        name="inproj",
    )(h, w_all)


def _embed_prompt_kernel(x_ref, meta_ref, g_ref, o_ref, h_ref):
    c = pl.program_id(0)

    @pl.when(c == 0)
    def _():
        head = jnp.concatenate([jnp.zeros((PROMPT_PAD, D_MODEL), F32), meta_ref[...]], axis=0)
        for b in range(BATCH):
            o_ref[b * CHUNK:(b + 1) * CHUNK, :] = head

    @pl.when(c > 0)
    def _():
        for b in range(BATCH):
            o_ref[b * CHUNK:(b + 1) * CHUNK, :] = x_ref[b]

    h_ref[...] = _rms(o_ref[...], g_ref[...]).astype(BF16)


def _embed_prompt(x_prompt, meta_tokens, g_all):
    x4 = x_prompt.reshape(BATCH, SEQ // CHUNK, CHUNK, D_MODEL)
    row = pl.BlockSpec((BATCH * CHUNK, D_MODEL), lambda c: (c, 0))
    return pl.pallas_call(
        _embed_prompt_kernel,
        grid=(PROMPT_CHUNKS,),
        in_specs=[pl.BlockSpec((BATCH, None, CHUNK, D_MODEL), lambda c: (0, jnp.maximum(c - 1, 0), 0, 0)),
                  pl.BlockSpec((N_META, D_MODEL), lambda c: (0, 0)),
                  pl.BlockSpec((None, 1, D_MODEL), lambda c: (0, 0, 0))],
        out_specs=[row, row],
        out_shape=[jax.ShapeDtypeStruct((N_PROMPT, D_MODEL), F32),
                   jax.ShapeDtypeStruct((N_PROMPT, D_MODEL), BF16)],
        compiler_params=pltpu.CompilerParams(dimension_semantics=("arbitrary",), vmem_limit_bytes=VMEM_LIMIT),
        name="embed_prompt",
    )(x4, meta_tokens, g_all)


def _outproj_kernel(with_next, by_sequence, *refs):
    if with_next:
        ya_ref, yb_ref, yc_ref, yd_ref, w_ref, x_ref, g_ref, m_ref, gn_ref, o_ref, h_ref = refs
    else:
        ya_ref, yb_ref, yc_ref, yd_ref, w_ref, x_ref, g_ref, m_ref, o_ref = refs
    acc = None
    for g, y_ref in enumerate((ya_ref, yb_ref, yc_ref, yd_ref)):
        part = jnp.dot(y_ref[...], w_ref[g * GROUP_W:(g + 1) * GROUP_W, :], preferred_element_type=F32)
        acc = part if acc is None else acc + part
    x_new = jnp.where(m_ref[...] > 0.0, x_ref[...] + _rms(acc, g_ref[...]), 0.0)
    if by_sequence:
        for b in range(BATCH):
            o_ref[b] = x_new[b * CHUNK:(b + 1) * CHUNK]
    else:
        o_ref[...] = x_new
    if with_next:
        h_ref[...] = _rms(x_new, gn_ref[...]).astype(BF16)


def _outproj(ys, w_all, x, post_all, rowmask, pre_all, layer, tm, by_sequence=False):
    n = x.shape[0]
    with_next = layer + 1 < DEPTH
    yspec = pl.BlockSpec((tm, GROUP_W), lambda i: (i, 0))
    row = pl.BlockSpec((tm, D_MODEL), lambda i: (i, 0))
    in_specs = [yspec, yspec, yspec, yspec,
                pl.BlockSpec((None, D_MODEL, D_MODEL), lambda i: (layer, 0, 0)),
                row,
                pl.BlockSpec((None, 1, D_MODEL), lambda i: (layer, 0, 0)),
                pl.BlockSpec((tm, 1), lambda i: (i, 0))]
    args = list(ys) + [w_all, x, post_all, rowmask]
    if by_sequence:
        assert tm == BATCH * CHUNK and not with_next
        out_specs = [pl.BlockSpec((BATCH, None, CHUNK, D_MODEL), lambda i: (0, jnp.maximum(i - 1, 0), 0, 0))]
        out_shape = [jax.ShapeDtypeStruct((BATCH, SEQ // CHUNK, CHUNK, D_MODEL), F32)]
    else:
        out_specs = [row]
        out_shape = [jax.ShapeDtypeStruct((n, D_MODEL), F32)]
    if with_next:
        in_specs.append(pl.BlockSpec((None, 1, D_MODEL), lambda i: (layer + 1, 0, 0)))
        args.append(pre_all)
        out_specs.append(row)
        out_shape.append(jax.ShapeDtypeStruct((n, D_MODEL), BF16))
    res = pl.pallas_call(
        functools.partial(_outproj_kernel, with_next, by_sequence),
        grid=(n // tm,), in_specs=in_specs, out_specs=out_specs, out_shape=out_shape,
        compiler_params=pltpu.CompilerParams(
            dimension_semantics=("arbitrary",), vmem_limit_bytes=VMEM_LIMIT),
        name="outproj",
    )(*args)
    return (res[0], res[1]) if with_next else (res[0], None)


def _init_state(c, has_init, s_scr, s0_ref, ext=None, c0_ref=None):
    @pl.when(c == 0)
    def _():
        if has_init:
            s_scr[...] = s0_ref[...]
            if ext is not None:
                ext[:, 0:SUBLANES, :] = c0_ref[...]
        else:
            s_scr[...] = jnp.zeros_like(s_scr)
            if ext is not None:
                ext[:, 0:SUBLANES, :] = jnp.zeros((ext.shape[0], SUBLANES, ext.shape[2]), F32)


def _run_staggered(groups, skew):
    pending, live, tick = list(groups), [], 0
    while pending or live:
        if pending and tick % skew == 0:
            live.append(pending.pop(0))
        for gen in list(live):
            if next(gen, "done") == "done":
                live.remove(gen)
        tick += 1


def _ssd_stages(nseq, rows, layer, has_init, state_t, nchunks, first_lo, hi, *refs):
    if has_init:
        (z_ref, x_ref, bc_ref, sm_ref, s0_ref, c0_ref, _, cw_ref, cb_ref, lp_ref, nw_ref, d_ref,
         y_ref, so_ref, s_scr, ext) = refs
    else:
        (z_ref, x_ref, bc_ref, sm_ref, cw_ref, cb_ref, lp_ref, nw_ref, d_ref,
         y_ref, so_ref, s_scr, ext) = refs
        s0_ref = c0_ref = None
    c = pl.program_id(1)
    m = rows
    _init_state(c, has_init, s_scr, s0_ref, ext, c0_ref)

    valid = _valid_rows(m, c, first_lo, hi)
    lane = lax.broadcasted_iota(jnp.int32, (1, CHUNK), 1)
    head_lane = (lane >= LANE_DT) & (lane < LANE_DT + SSD_HEADS)
    dt_all = _softplus(sm_ref[...] + lp_ref[ROW_SSD_DTB:ROW_SSD_DTB + 1, :])
    la_all = -jnp.exp(lp_ref[ROW_SSD_ALOG:ROW_SSD_ALOG + 1, :]) * dt_all
    dts = [jnp.where(valid, dt_all[b * m:(b + 1) * m], 0.0) for b in range(nseq)]
    la, stride = _pack_lanes([jnp.where(valid & head_lane, la_all[b * m:(b + 1) * m], 0.0) for b in range(nseq)])
    cum, cum_t, e_cum, e_rem, e_last = _decay_terms(la)

    shift_mat = _shift_rows_matrix(m) if m > SUBLANES else None
    hpg = SSD_HEADS // SSD_GROUPS
    c_off = GROUP_W + SSD_GROUPS * SSD_STATE
    gw = GROUP_W // SSD_GROUPS
    ys = [None] * nseq

    def sequence(b):
        rows_b = slice(b * m, (b + 1) * m)
        cur = jnp.concatenate([x_ref[rows_b, :], bc_ref[rows_b, :]], axis=1)
        a_b = _silu(_causal_conv(ext.at[b], cur, cw_ref, shift_mat) + cb_ref[...])
        yield
        scores = []
        for g in range(SSD_GROUPS):
            b_g = a_b[:, GROUP_W + g * SSD_STATE:GROUP_W + (g + 1) * SSD_STATE]
            c_g = a_b[:, c_off + g * SSD_STATE:c_off + (g + 1) * SSD_STATE]
            scores.append((_dot_nt(c_g, _pad_rows(b_g)), b_g, c_g))
        yield
        heads = []
        for h in range(SSD_HEADS):
            sc, b_g, c_g = scores[h // hpg]
            ln = b * stride + LANE_DT + h
            x_h = a_b[:, h * SSD_HEAD_DIM:(h + 1) * SSD_HEAD_DIM]
            v_h = x_h * _col(dts[b], LANE_DT + h, SSD_HEAD_DIM)
            s_h = s_scr[b, h]
            v_w = v_h * _col(e_rem, ln, SSD_HEAD_DIM)
            if state_t:
                from_state = _dot_nt(c_g, s_h)
                s_scr[b, h] = s_h * e_last[:, ln:ln + 1] + _dot_tn(v_w, b_g)
            else:
                from_state = _dot(c_g, s_h)
                s_scr[b, h] = s_h * e_last[:, ln:ln + 1] + _dot_tn(b_g, v_w)
            o = (_dot(sc * _decay_matrix(cum, cum_t, ln), _pad_rows(v_h))
                 + from_state * _col(e_cum, ln, SSD_HEAD_DIM))
            heads.append(o + d_ref[layer, h] * x_h)
            yield
        y = jnp.concatenate(heads, axis=1) * _silu(z_ref[rows_b, :])
        outs = [_rms(y[:, g * gw:(g + 1) * gw], nw_ref[:, g * gw:(g + 1) * gw]) for g in range(SSD_GROUPS)]
        ys[b] = jnp.concatenate(outs, axis=1)

    def finish():
        y_ref[...] = jnp.concatenate(ys, axis=0).astype(y_ref.dtype)

        @pl.when(c == nchunks - 1)
        def _():
            so_ref[...] = s_scr[...]

    return [sequence(b) for b in range(nseq)], finish


def _ret_stages(nseq, seq_group, rows, has_init, nchunks, first_lo, hi, *refs):
    if has_init:
        qk_ref, v_ref, g_ref, s0_ref, _, lp_ref, nw_ref, y_ref, so_ref, s_scr = refs
    else:
        qk_ref, v_ref, g_ref, lp_ref, nw_ref, y_ref, so_ref, s_scr = refs
        s0_ref = None
    c = pl.program_id(1)
    m = rows
    _init_state(c, has_init, s_scr, s0_ref)

    valid = _valid_rows(m, c, first_lo, hi)
    lane = lax.broadcasted_iota(jnp.int32, (1, CHUNK), 1)
    log_gamma = jnp.broadcast_to(lp_ref[ROW_LOG_GAMMA:ROW_LOG_GAMMA + 1, :], (m, CHUNK))
    la_one = jnp.where(valid & (lane < RET_HEADS), log_gamma, 0.0)
    la, stride = _pack_lanes([la_one] * nseq)
    cum, cum_t, e_cum, e_rem, e_last = _decay_terms(la)

    qk_all = qk_ref[...]
    v_all = v_ref[...]
    all_units = []
    for b in range(nseq):
        qk = qk_all[b * m:(b + 1) * m]
        v = jnp.where(valid, v_all[b * m:(b + 1) * m], 0.0)
        for h in range(RET_HEADS):
            q_h = qk[:, h * RET_DK:(h + 1) * RET_DK]
            k_h = qk[:, (RET_HEADS + h) * RET_DK:(RET_HEADS + h + 1) * RET_DK] * (RET_DK ** -0.5)
            s_h = s_scr[b, h]
            all_units.append(dict(b=b, h=h, ln=b * stride + h, k=k_h, v=v[:, h * RET_DV:(h + 1) * RET_DV],
                                  s=s_h, raw=_dot_nt(q_h, _pad_rows(k_h)), q_s=_dot(q_h, s_h)))
    ys = [None] * nseq

    def group(b0):
        units = all_units[b0 * RET_HEADS:(b0 + seq_group) * RET_HEADS]
        scs = [u["raw"] * _decay_matrix(cum, cum_t, u["ln"]) for u in units]
        yield
        os_ = [_dot(sc, _pad_rows(u["v"])) + u["q_s"] * _col(e_cum, u["ln"], RET_DV)
               for sc, u in zip(scs, units)]
        yield
        for u in units:
            ln = u["ln"]
            s_scr[u["b"], u["h"]] = (u["s"] * e_last[:, ln:ln + 1]
                                     + _dot_tn(u["k"] * _col(e_rem, ln, RET_DK), u["v"]))
        yield
        for i in range(seq_group):
            heads = [_rms(os_[i * RET_HEADS + h], nw_ref[:, h * RET_DV:(h + 1) * RET_DV])
                     for h in range(RET_HEADS)]
            rows_b = slice((b0 + i) * m, (b0 + i + 1) * m)
            ys[b0 + i] = jnp.concatenate(heads, axis=1) * _silu(g_ref[rows_b, :])

    def finish():
        y_ref[...] = jnp.concatenate(ys, axis=0).astype(y_ref.dtype)

        @pl.when(c == nchunks - 1)
        def _():
            so_ref[...] = s_scr[...]

    return [group(b0) for b0 in range(0, nseq, seq_group)], finish


def _gdn_stages(nseq, seq_group, rows, has_init, nchunks, first_lo, hi, *refs):
    if has_init:
        (q_ref, k_ref, v_ref, g_ref, sm_ref, s0_ref, c0_ref, _, cw_ref, lp_ref, nw_ref,
         y_ref, so_ref, s_scr, ext) = refs
    else:
        (q_ref, k_ref, v_ref, g_ref, sm_ref, cw_ref, lp_ref, nw_ref,
         y_ref, so_ref, s_scr, ext) = refs
        s0_ref = c0_ref = None
    c = pl.program_id(1)
    m = rows
    _init_state(c, has_init, s_scr, s0_ref, ext, c0_ref)

    shift_mat = _shift_rows_matrix(m) if m > SUBLANES else None
    valid = _valid_rows(m, c, first_lo, hi)
    lane = lax.broadcasted_iota(jnp.int32, (1, CHUNK), 1)
    head_lane = (lane >= LANE_DECAY) & (lane < LANE_DECAY + GDN_HEADS)
    sm = sm_ref[...]
    beta_all = _sigmoid(sm)
    la_all = (-jnp.exp(lp_ref[ROW_GDN_ALOG:ROW_GDN_ALOG + 1, :])
              * _softplus(sm + lp_ref[ROW_GDN_DTB:ROW_GDN_DTB + 1, :]))
    betas = [jnp.where(valid, beta_all[b * m:(b + 1) * m], 0.0) for b in range(nseq)]
    la, stride = _pack_lanes([jnp.where(valid & head_lane, la_all[b * m:(b + 1) * m], 0.0) for b in range(nseq)])
    cum, cum_t, e_cum, e_rem, e_last = _decay_terms(la)
    row, col = _iotas(m)
    levels = max(1, (min(m, hi) - 1).bit_length())

    ys = [None] * nseq

    def joining_mask(shift):
        same_big = lax.shift_right_logical(row, shift + 1) == lax.shift_right_logical(col, shift + 1)
        same_small = lax.shift_right_logical(row, shift) == lax.shift_right_logical(col, shift)
        return same_big & jnp.logical_not(same_small) & (row > col)

    def group(b0):
        units = []
        for b in range(b0, b0 + seq_group):
            rows_b = slice(b * m, (b + 1) * m)
            cur = jnp.concatenate([q_ref[rows_b, :], k_ref[rows_b, :], v_ref[rows_b, :]], axis=1)
            a_b = _silu(_causal_conv(ext.at[b], cur, cw_ref, shift_mat))
            yield
            for h in range(GDN_HEADS):
                ln = b * stride + LANE_DECAY + h
                q_h = a_b[:, h * GDN_DK:(h + 1) * GDN_DK]
                k_h = a_b[:, GROUP_W + h * GDN_DK:GROUP_W + (h + 1) * GDN_DK]
                v_h = a_b[:, 2 * GROUP_W + h * GDN_DV:2 * GROUP_W + (h + 1) * GDN_DV]
                q_h = q_h * lax.rsqrt(jnp.sum(q_h * q_h, axis=-1, keepdims=True) + 1e-6) * (GDN_DK ** -0.5)
                k_h = k_h * lax.rsqrt(jnp.sum(k_h * k_h, axis=-1, keepdims=True) + 1e-6)
                k_h = jnp.where(valid, k_h, 0.0)
                b_h = _col(betas[b], LANE_BETA + h, GDN_DK)
                k_b = k_h * b_h
                decay = _decay_matrix(cum, cum_t, ln)
                k_pad = _pad_rows(k_h)
                units.append(dict(
                    b=b, h=h, ln=ln, q=q_h, k=k_h, k_b=k_b, decay=decay,
                    a=jnp.where(row > col, _dot_nt(k_b, k_pad) * decay, 0.0),
                    v_b=_pad_rows(jnp.where(valid, v_h, 0.0) * b_h),
                    qk=_dot_nt(q_h, k_pad) * decay))
                yield
        eye = (row == col).astype(F32)
        mask = joining_mask(0)
        ts = [eye - jnp.where(mask, u["a"], 0.0) for u in units]
        for shift in range(1, levels):
            mask = joining_mask(shift)
            inner = [_dot(jnp.where(mask, u["a"], 0.0), _pad_rows(t)) for u, t in zip(units, ts)]
            yield
            ts = [t - _dot(t, _pad_rows(x)) for t, x in zip(ts, inner)]
            yield
        us = [_dot(t, u["v_b"]) for t, u in zip(ts, units)]
        ws = [_dot(t, _pad_rows(u["k_b"] * _col(e_cum, u["ln"], GDN_DK))) for t, u in zip(ts, units)]
        yield
        ss = [s_scr[u["b"], u["h"]] for u in units]
        v_news = [x - _dot(w, s_h) for x, w, s_h in zip(us, ws, ss)]
        yield
        os_ = [_dot(u["q"] * _col(e_cum, u["ln"], GDN_DK), s_h) + _dot(u["qk"], _pad_rows(v_new))
               for u, s_h, v_new in zip(units, ss, v_news)]
        yield
        for u, s_h, v_new in zip(units, ss, v_news):
            ln = u["ln"]
            s_scr[u["b"], u["h"]] = (s_h * e_last[:, ln:ln + 1]
                                     + _dot_tn(u["k"] * _col(e_rem, ln, GDN_DK), v_new))
        yield
        for i in range(seq_group):
            heads = [_rms(o, nw_ref[...]) for o in os_[i * GDN_HEADS:(i + 1) * GDN_HEADS]]
            rows_b = slice((b0 + i) * m, (b0 + i + 1) * m)
            ys[b0 + i] = jnp.concatenate(heads, axis=1) * _silu(g_ref[rows_b, :])

    def finish():
        y_ref[...] = jnp.concatenate(ys, axis=0).astype(y_ref.dtype)

        @pl.when(c == nchunks - 1)
        def _():
            so_ref[...] = s_scr[...]

    return [group(b0) for b0 in range(0, nseq, seq_group)], finish


def _swa_stages(nseq, seq_group, rows, layer, is_prompt, *refs):
    if is_prompt:
        q_ref, g_ref, kvc_ref, kvp_ref, sink_ref, y_ref = refs
    else:
        q_ref, g_ref, kvc_ref, kp_ref, vp_ref, sink_ref, y_ref = refs
    n = pl.program_id(1)
    m = rows
    if is_prompt:
        lo_prev = jnp.where(n == 0, CHUNK, jnp.where(n == 1, PROMPT_PAD, 0))
        lo_cur = jnp.where(n == 0, PROMPT_PAD, 0)
    else:
        lo_prev, lo_cur = 0, 0
    qi = lax.broadcasted_iota(jnp.int32, (m, 2 * CHUNK), 0)
    kj = lax.broadcasted_iota(jnp.int32, (m, 2 * CHUNK), 1)
    dist = qi + WINDOW - kj
    key_ok = ((kj < CHUNK) & (kj >= lo_prev)) | (kj >= CHUNK + lo_cur)
    visible = (dist >= 0) & (dist <= WINDOW) & key_ok
    dist_f = dist.astype(F32)

    grp = SWA_HEADS // SWA_KV_HEADS
    ys = [None] * nseq

    def group(b0):
        units = []
        for b in range(b0, b0 + seq_group):
            q = q_ref[b * m:(b + 1) * m, :]
            kvc = _pad_rows(kvc_ref[b * m:(b + 1) * m, :])
            if is_prompt:
                k_prev = kvp_ref[b * CHUNK:(b + 1) * CHUNK, 0:KV_W]
                v_prev = kvp_ref[b * CHUNK:(b + 1) * CHUNK, KV_W:2 * KV_W]
            else:
                k_prev, v_prev = kp_ref[b], vp_ref[b]
            k_all = jnp.concatenate([k_prev, kvc[:, 0:KV_W]], axis=0)
            v_all = jnp.concatenate([v_prev, kvc[:, KV_W:2 * KV_W]], axis=0)
            for h in range(SWA_HEADS):
                kvh = h // grp
                units.append(dict(h=h, q=q[:, h * SWA_HEAD_DIM:(h + 1) * SWA_HEAD_DIM],
                                  k=k_all[:, kvh * SWA_HEAD_DIM:(kvh + 1) * SWA_HEAD_DIM],
                                  v=v_all[:, kvh * SWA_HEAD_DIM:(kvh + 1) * SWA_HEAD_DIM]))
        ss = [jnp.where(visible,
                        _dot_nt(u["q"], u["k"]) * (SWA_HEAD_DIM ** -0.5)
                        - 2.0 ** (-8.0 * (u["h"] + 1) / SWA_HEADS) * dist_f, -1e30) for u in units]
        yield
        mxs = [jnp.maximum(jnp.max(s, axis=-1, keepdims=True), sink_ref[layer, u["h"]])
               for s, u in zip(ss, units)]
        es = [jnp.exp(s - mx) for s, mx in zip(ss, mxs)]
        yield
        dens = [jnp.sum(e, axis=-1, keepdims=True) + jnp.exp(sink_ref[layer, u["h"]] - mx)
                for e, mx, u in zip(es, mxs, units)]
        os_ = [_dot(e, u["v"]) / den for e, u, den in zip(es, units, dens)]
        yield
        for i in range(seq_group):
            rows_b = slice((b0 + i) * m, (b0 + i + 1) * m)
            ys[b0 + i] = (jnp.concatenate(os_[i * SWA_HEADS:(i + 1) * SWA_HEADS], axis=1)
                          * _silu(g_ref[rows_b, :]))

    def finish():
        y_ref[...] = jnp.concatenate(ys, axis=0).astype(y_ref.dtype)

    return [group(b0) for b0 in range(0, nseq, seq_group)], finish


def _mixer_kernel(parts, *refs):
    starts = [0, sum(p[1] for p in parts), sum(p[1] + p[2] for p in parts)]
    gens, finishes = [], []
    for stages, n_in, n_out, n_scr in parts:
        mine = []
        for k, cnt in enumerate((n_in, n_out, n_scr)):
            mine += refs[starts[k]:starts[k] + cnt]
            starts[k] += cnt
        g, fin = stages(*mine)
        gens.append(g)
        finishes.append(fin)
    for w in range(0, len(gens), MIXERS_PER_WAVE):
        wave = gens[w:w + MIXERS_PER_WAVE]
        order = []
        for i in range(max(len(g) for g in wave)):
            order += [g[i] for g in wave if i < len(g)]
        _run_staggered(order, 1)
    for fin in finishes:
        fin()


def _geometry(sample):
    if sample:
        return (DEC_BATCH // SAMPLE_SEQS, 1), SAMPLE_SEQS, SUBLANES, N_SAMPLE, 0, DEC_SEQ
    return (1, PROMPT_CHUNKS), BATCH, CHUNK, N_PROMPT, PROMPT_PAD, CHUNK


def _rowblk(sample):
    return (lambda blk, c: blk) if sample else (lambda blk, c: c)


def _pcol(nrows, rowblk, width, col):
    assert col % width == 0
    return pl.BlockSpec((nrows, width), lambda blk, c: (rowblk(blk, c), col // width))


def _layer_param(shape, layer):
    return pl.BlockSpec((None,) + shape, lambda blk, c: (layer,) + (0,) * len(shape))


def _layer_seqs(nseq, shape, layer):
    return pl.BlockSpec((None, nseq) + shape, lambda blk, c: (layer, blk) + (0,) * len(shape))


def _seqs_out(nseq, shape):
    return pl.BlockSpec((nseq,) + shape, lambda blk, c: (blk,) + (0,) * len(shape))


_SMEM = pl.BlockSpec(memory_space=pltpu.SMEM)
_MIXER_PARAMS = pltpu.CompilerParams(dimension_semantics=("arbitrary", "arbitrary"),
                                     vmem_limit_bytes=LARGE_VMEM_LIMIT)


def _state_output(sample, nseq, state, layer, acc, in_specs, args):
    if not sample:
        return _seqs_out(nseq, state), jax.ShapeDtypeStruct((nseq,) + state, F32), {}
    in_specs.append(pl.BlockSpec(memory_space=pl.ANY))
    args.append(acc)
    spec = pl.BlockSpec((None, nseq) + state, lambda blk, c: (layer, blk) + (0,) * len(state))
    return spec, jax.ShapeDtypeStruct(acc.shape, F32), {len(args) - 1: 1}


def _ssd(p, prm, layer, init, acc=None):
    sample = init is not None
    grid, nseq, rows, total, first_lo, hi = _geometry(sample)
    rb, n = _rowblk(sample), nseq * rows
    state = (SSD_HEADS, SSD_HEAD_DIM, SSD_STATE) if sample else (SSD_HEADS, SSD_STATE, SSD_HEAD_DIM)
    in_specs = [_pcol(n, rb, GROUP_W, COL_Z), _pcol(n, rb, GROUP_W, COL_X),
                _pcol(n, rb, GROUP_W, COL_BC), _pcol(n, rb, CHUNK, COL_SMALL)]
    args = [p, p, p, p]
    if sample:
        in_specs += [_layer_seqs(nseq, state, layer), _layer_seqs(nseq, (SUBLANES, SSD_CONV_W), layer)]
        args += list(init)
    state_spec, state_shape, aliases = _state_output(sample, nseq, state, layer, acc, in_specs, args)
    in_specs += [_layer_param((CONV_K, SSD_CONV_W), layer), _layer_param((1, SSD_CONV_W), layer),
                 _layer_param((SUBLANES, CHUNK), layer), _layer_param((1, GROUP_W), layer), _SMEM]
    args += [prm["ssd_conv_w"], prm["ssd_conv_b"], prm["lanes"], prm["ssd_norm"], prm["ssd_d"]]
    return dict(
        name="ssd", grid=grid,
        stages=functools.partial(_ssd_stages, nseq, rows, layer, sample, sample, grid[1], first_lo, hi),
        in_specs=in_specs, args=args,
        out_specs=[pl.BlockSpec((n, GROUP_W), lambda blk, c: (rb(blk, c), 0)), state_spec],
        out_shape=[jax.ShapeDtypeStruct((total, GROUP_W), BF16), state_shape],
        aliases=aliases,
        scratch=[pltpu.VMEM((nseq,) + state, F32), pltpu.VMEM((nseq, 2 * SUBLANES, SSD_CONV_W), F32)])


def _ret(p, prm, layer, init, acc=None):
    sample = init is not None
    grid, nseq, rows, total, first_lo, hi = _geometry(sample)
    rb, n = _rowblk(sample), nseq * rows
    state = (RET_HEADS, RET_DK, RET_DV)
    in_specs = [_pcol(n, rb, GROUP_W, COL_QKR), _pcol(n, rb, GROUP_W, COL_VR), _pcol(n, rb, GROUP_W, COL_GR)]
    args = [p, p, p]
    if sample:
        in_specs += [_layer_seqs(nseq, state, layer)]
        args += list(init)
    state_spec, state_shape, aliases = _state_output(sample, nseq, state, layer, acc, in_specs, args)
    in_specs += [_layer_param((SUBLANES, CHUNK), layer), _layer_param((1, GROUP_W), layer)]
    args += [prm["lanes"], prm["ret_norm"]]
    return dict(
        name="ret", grid=grid,
        stages=functools.partial(_ret_stages, nseq, nseq if sample else 1, rows, sample, grid[1], first_lo, hi),
        in_specs=in_specs, args=args,
        out_specs=[pl.BlockSpec((n, GROUP_W), lambda blk, c: (rb(blk, c), 0)), state_spec],
        out_shape=[jax.ShapeDtypeStruct((total, GROUP_W), BF16), state_shape],
        aliases=aliases,
        scratch=[pltpu.VMEM((nseq,) + state, F32)])


def _gdn(p, prm, layer, init, acc=None):
    sample = init is not None
    grid, nseq, rows, total, first_lo, hi = _geometry(sample)
    rb, n = _rowblk(sample), nseq * rows
    state = (GDN_HEADS, GDN_DK, GDN_DV)
    in_specs = [_pcol(n, rb, GROUP_W, COL_QD), _pcol(n, rb, GROUP_W, COL_KD), _pcol(n, rb, GROUP_W, COL_VD),
                _pcol(n, rb, GROUP_W, COL_GD), _pcol(n, rb, CHUNK, COL_SMALL)]
    args = [p, p, p, p, p]
    if sample:
        in_specs += [_layer_seqs(nseq, state, layer), _layer_seqs(nseq, (SUBLANES, GDN_CONV_W), layer)]
        args += list(init)
    state_spec, state_shape, aliases = _state_output(sample, nseq, state, layer, acc, in_specs, args)
    in_specs += [_layer_param((CONV_K, GDN_CONV_W), layer), _layer_param((SUBLANES, CHUNK), layer),
                 _layer_param((1, GDN_DV), layer)]
    args += [prm["gdn_conv_w"], prm["lanes"], prm["gdn_norm"]]
    return dict(
        name="gdn", grid=grid,
        stages=functools.partial(_gdn_stages, nseq, 2 if sample else 1, rows, sample, grid[1], first_lo, hi),
        in_specs=in_specs, args=args,
        out_specs=[pl.BlockSpec((n, GROUP_W), lambda blk, c: (rb(blk, c), 0)), state_spec],
        out_shape=[jax.ShapeDtypeStruct((total, GROUP_W), BF16), state_shape],
        aliases=aliases,
        scratch=[pltpu.VMEM((nseq,) + state, F32), pltpu.VMEM((nseq, 2 * SUBLANES, GDN_CONV_W), F32)])


def _swa(p, prm, layer, cache):
    sample = cache is not None
    grid, nseq, rows, total, _, _ = _geometry(sample)
    rb, n = _rowblk(sample), nseq * rows
    in_specs = [_pcol(n, rb, GROUP_W, COL_QA), _pcol(n, rb, GROUP_W, COL_GA), _pcol(n, rb, 2 * KV_W, COL_KV)]
    args = [p, p, p]
    if sample:
        in_specs += [_layer_seqs(nseq, (WINDOW, KV_W), layer), _layer_seqs(nseq, (WINDOW, KV_W), layer)]
        args += list(cache)
    else:
        in_specs += [pl.BlockSpec((n, 2 * KV_W), lambda blk, c: (jnp.maximum(c - 1, 0), COL_KV // (2 * KV_W)))]
        args += [p]
    in_specs += [_SMEM]
    args += [prm["swa_sinks"]]
    return dict(
        name="swa", grid=grid,
        stages=functools.partial(_swa_stages, nseq, nseq if sample else 1, rows, layer, not sample),
        in_specs=in_specs, args=args,
        out_specs=[pl.BlockSpec((n, GROUP_W), lambda blk, c: (rb(blk, c), 0))],
        out_shape=[jax.ShapeDtypeStruct((total, GROUP_W), BF16)],
        aliases={}, scratch=[])


def _run_mixers(parts, suffix):
    kernel_parts = tuple((p["stages"], len(p["in_specs"]), len(p["out_specs"]), len(p["scratch"]))
                         for p in parts)
    aliases, n_in, n_out = {}, 0, 0
    for p in parts:
        aliases.update({n_in + k: n_out + v for k, v in p["aliases"].items()})
        n_in += len(p["in_specs"])
        n_out += len(p["out_specs"])
    res = pl.pallas_call(
        functools.partial(_mixer_kernel, kernel_parts),
        grid=parts[0]["grid"],
        in_specs=[s for p in parts for s in p["in_specs"]],
        out_specs=[s for p in parts for s in p["out_specs"]],
        out_shape=[s for p in parts for s in p["out_shape"]],
        scratch_shapes=[s for p in parts for s in p["scratch"]],
        input_output_aliases=aliases,
        compiler_params=_MIXER_PARAMS,
        name="_".join(p["name"] for p in parts) + suffix,
    )(*[a for p in parts for a in p["args"]])
    outs, k = [], 0
    for p in parts:
        outs.append(list(res[k:k + len(p["out_specs"])]))
        k += len(p["out_specs"])
    return outs


def _lane_table(ssd_dt_bias, ssd_a_log, gdn_dt_bias, gdn_a_log):
    def put(v, lane):
        return jnp.pad(v.astype(F32), ((0, 0), (lane, CHUNK - lane - v.shape[1])))
    log_gamma = jnp.log1p(-jnp.exp2(-5.0 - jnp.arange(RET_HEADS, dtype=F32)))
    rows = [put(ssd_dt_bias, LANE_DT), put(ssd_a_log, LANE_DT), put(gdn_dt_bias, LANE_DECAY),
            put(gdn_a_log, LANE_DECAY), put(jnp.broadcast_to(log_gamma, (DEPTH, RET_HEADS)), 0)]
    rows += [jnp.zeros((DEPTH, CHUNK), F32)] * (SUBLANES - len(rows))
    return jnp.stack(rows, axis=1)


def _conv_state_pad(state):
    return jnp.pad(state, ((0, 0), (0, 0), (SUBLANES - (CONV_K - 1), 0), (0, 0)))


def kernel(x_prompt, x_sample, state_ssd, state_ssd_conv, cache_swa_k, cache_swa_v, state_ret, state_gdn,
           state_gdn_conv, meta_tokens, pre_norm, post_norm, w_in, w_out, ssd_conv_w, ssd_conv_b, ssd_dt_bias,
           ssd_a_log, ssd_d, ssd_norm, swa_sinks, ret_norm, gdn_conv_w, gdn_dt_bias, gdn_a_log, gdn_norm):
    xs = jnp.pad(x_sample, ((0, 0), (0, SUBLANES - DEC_SEQ), (0, 0))).reshape(N_SAMPLE, D_MODEL)
    mask_np = np.ones((PROMPT_CHUNKS, BATCH, CHUNK), np.float32)
    mask_np[0, :, :PROMPT_PAD] = 0.0
    mask_p = jnp.asarray(mask_np.reshape(N_PROMPT, 1))
    mask_s = jnp.asarray(np.tile(np.arange(SUBLANES) < DEC_SEQ, DEC_BATCH)[:, None], F32)

    w_in_p = _wprep(jnp.swapaxes(w_in, 1, 2))
    w_out_b = w_out.astype(BF16)
    pre_g = pre_norm.reshape(DEPTH, 1, D_MODEL)
    post_g = post_norm.reshape(DEPTH, 1, D_MODEL)
    prm = dict(
        lanes=_lane_table(ssd_dt_bias, ssd_a_log, gdn_dt_bias, gdn_a_log),
        ssd_conv_w=ssd_conv_w, ssd_conv_b=ssd_conv_b.reshape(DEPTH, 1, SSD_CONV_W),
        ssd_norm=ssd_norm.reshape(DEPTH, 1, GROUP_W), ssd_d=ssd_d, swa_sinks=swa_sinks,
        ret_norm=ret_norm.reshape(DEPTH, 1, GROUP_W), gdn_conv_w=gdn_conv_w,
        gdn_norm=gdn_norm.reshape(DEPTH, 1, GDN_DV))
    ssd_c0 = _conv_state_pad(state_ssd_conv)
    gdn_c0 = _conv_state_pad(state_gdn_conv)
    cache_k = cache_swa_k.reshape(DEPTH, DEC_BATCH, WINDOW, KV_W)
    cache_v = cache_swa_v.reshape(DEPTH, DEC_BATCH, WINDOW, KV_W)

    names = ("ssd", "ssd_conv", "swa_k", "swa_v", "ret", "gdn", "gdn_conv")
    p_out = {k: [] for k in names}
    s_out = {k: [] for k in ("ssd_conv", "swa_k", "swa_v", "gdn_conv")}
    tail = slice(CHUNK - (CONV_K - 1), CHUNK)
    state_ssd_t = jnp.swapaxes(state_ssd, 3, 4)
    s_ssd_t = jnp.zeros(state_ssd_t.shape, F32)
    s_ret = jnp.zeros(state_ret.shape, F32)
    s_gdn = jnp.zeros(state_gdn.shape, F32)
    xp, hp = _embed_prompt(x_prompt, meta_tokens, pre_g)
    hs = _prenorm(xs, pre_g, 0, N_SAMPLE)
    for l in range(DEPTH):
        pp = _inproj(hp, w_in_p, l, INPROJ_TM)
        (yd, p_gdn), (ya, p_ssd), (yc, p_ret), (yb,) = _run_mixers(
            [_gdn(pp, prm, l, None), _ssd(pp, prm, l, None), _ret(pp, prm, l, None), _swa(pp, prm, l, None)],
            "_prompt")
        xp, hp = _outproj((ya, yb, yc, yd), w_out_b, xp, post_g, mask_p, pre_g, l, PROMPT_TM,
                          by_sequence=l == DEPTH - 1)

        ps = _inproj(hs, w_in_p, l, N_SAMPLE)
        (yd, s_gdn), (ya, s_ssd_t), (yc, s_ret), (yb,) = _run_mixers(
            [_gdn(ps, prm, l, (state_gdn, gdn_c0), s_gdn), _ssd(ps, prm, l, (state_ssd_t, ssd_c0), s_ssd_t),
             _ret(ps, prm, l, (state_ret,), s_ret), _swa(ps, prm, l, (cache_k, cache_v))],
            "_sample")
        xs, hs = _outproj((ya, yb, yc, yd), w_out_b, xs, post_g, mask_s, pre_g, l, N_SAMPLE)

        pp4 = pp.reshape(PROMPT_CHUNKS, BATCH, CHUNK, IN_W_PAD)

        def last_chunk(r0, c0, c1):
            return lax.slice(pp4, (PROMPT_CHUNKS - 1, 0, r0, c0), (PROMPT_CHUNKS, BATCH, CHUNK, c1))[0]

        ps3 = ps.reshape(DEC_BATCH, SUBLANES, IN_W_PAD)
        p_out["ssd"].append(p_ssd)
        p_out["ssd_conv"].append(last_chunk(tail.start, COL_X, COL_X + SSD_CONV_W))
        p_out["swa_k"].append(last_chunk(0, COL_KV, COL_KV + KV_W))
        p_out["swa_v"].append(last_chunk(0, COL_KV + KV_W, COL_KV + 2 * KV_W))
        p_out["ret"].append(p_ret)
        p_out["gdn"].append(p_gdn)
        p_out["gdn_conv"].append(last_chunk(tail.start, COL_QD, COL_QD + GDN_CONV_W))
        s_out["ssd_conv"].append(ps3[:, 1:DEC_SEQ, COL_X:COL_X + SSD_CONV_W])
        s_out["swa_k"].append(ps3[:, :DEC_SEQ, COL_KV:COL_KV + KV_W])
        s_out["swa_v"].append(ps3[:, :DEC_SEQ, COL_KV + KV_W:COL_KV + 2 * KV_W])
        s_out["gdn_conv"].append(ps3[:, 1:DEC_SEQ, COL_QD:COL_QD + GDN_CONV_W])

    p_st = {k: jnp.stack(v) for k, v in p_out.items()}
    s_st = {k: jnp.stack(v) for k, v in s_out.items()}
    kv_shape = (SWA_KV_HEADS, SWA_HEAD_DIM)
    y_prompt = xp.reshape(BATCH, SEQ, D_MODEL)
    y_sample = xs.reshape(DEC_BATCH, SUBLANES, D_MODEL)[:, :DEC_SEQ]
    return (
        y_prompt, y_sample,
        p_st["ssd"], p_st["ssd_conv"],
        p_st["swa_k"].reshape((DEPTH, BATCH, WINDOW) + kv_shape),
        p_st["swa_v"].reshape((DEPTH, BATCH, WINDOW) + kv_shape),
        p_st["ret"], p_st["gdn"], p_st["gdn_conv"],
        jnp.swapaxes(s_ssd_t, 3, 4), s_st["ssd_conv"],
        jnp.concatenate([cache_swa_k[:, :, DEC_SEQ:], s_st["swa_k"].reshape((DEPTH, DEC_BATCH, DEC_SEQ) + kv_shape)], axis=2),
        jnp.concatenate([cache_swa_v[:, :, DEC_SEQ:], s_st["swa_v"].reshape((DEPTH, DEC_BATCH, DEC_SEQ) + kv_shape)], axis=2),
        s_ret, s_gdn, s_st["gdn_conv"],
    )
```

```python
import functools

import jax
import jax.numpy as jnp
import numpy as np
from jax import lax
from jax.experimental import pallas as pl
from jax.experimental.pallas import tpu as pltpu

F32 = jnp.float32
BF16 = jnp.bfloat16
HIGHEST = lax.Precision.HIGHEST

D_MODEL = 2048
BATCH = 4
SEQ = 2048
DEPTH = 4
DEC_BATCH = 32
DEC_SEQ = 4
N_META = 16
GROUP_W = 512
CONV_K = 4
NORM_EPS = 1e-6
WINDOW = 128

SSD_HEADS, SSD_HEAD_DIM, SSD_GROUPS, SSD_STATE = 8, 64, 2, 128
SWA_HEADS, SWA_KV_HEADS, SWA_HEAD_DIM = 8, 2, 64
RET_HEADS, RET_DK, RET_DV = 4, 64, 128
GDN_HEADS, GDN_DK, GDN_DV = 4, 128, 128
SSD_CONV_W = GROUP_W + 2 * SSD_GROUPS * SSD_STATE
GDN_CONV_W = 3 * GROUP_W
KV_W = SWA_KV_HEADS * SWA_HEAD_DIM

CHUNK = 128
SUBLANES = 8
PROMPT_PAD = CHUNK - N_META
PROMPT_ROWS = PROMPT_PAD + N_META + SEQ
PROMPT_CHUNKS = PROMPT_ROWS // CHUNK
N_PROMPT = BATCH * PROMPT_ROWS
N_SAMPLE = DEC_BATCH * SUBLANES
SAMPLE_SEQS = 8
MIXERS_PER_WAVE = 2

IN_W = 6416
IN_W_PAD = 6528
COL_Z, COL_X, COL_BC, COL_QA, COL_GA, COL_VR, COL_GR = 0, 512, 1024, 1536, 2048, 2560, 3072
COL_QD, COL_KD, COL_VD, COL_GD, COL_QKR, COL_KV, COL_SMALL = 3584, 4096, 4608, 5120, 5632, 6144, 6400
LANE_DT, LANE_BETA, LANE_DECAY = 0, 8, 12
ROW_SSD_DTB, ROW_SSD_ALOG, ROW_GDN_DTB, ROW_GDN_ALOG, ROW_LOG_GAMMA = 0, 1, 2, 3, 4

W_SEGMENTS = ((0, 1536), (1544, 2056), (2312, 2824), (3336, 3848), (3848, 4360), (4360, 5896), (5896, 6408),
              (2824, 3336), (2056, 2312))
W_BLOCKS = IN_W_PAD // CHUNK
W_PER_STEP = 3
W_STEPS = W_BLOCKS // W_PER_STEP
assert W_STEPS * W_PER_STEP == W_BLOCKS

PROJ_TN = 2176
PROMPT_TM = 512
INPROJ_TM = 1088
VMEM_LIMIT = 48 * 1024 * 1024
LARGE_VMEM_LIMIT = 56 * 1024 * 1024


def _sigmoid(x):
    return 1.0 / (1.0 + jnp.exp(-x))


def _silu(x):
    return x * _sigmoid(x)


def _softplus(x):
    return jnp.maximum(x, 0.0) + jnp.log1p(jnp.exp(-jnp.abs(x)))


def _dot(a, b):
    return jnp.dot(a.astype(BF16), b.astype(BF16), preferred_element_type=F32)


def _dot_nt(a, b):
    return lax.dot_general(a.astype(BF16), b.astype(BF16), (((1,), (1,)), ((), ())),
                           preferred_element_type=F32)


def _dot_tn(a, b):
    return lax.dot_general(a.astype(BF16), b.astype(BF16), (((0,), (0,)), ((), ())),
                           preferred_element_type=F32)


def _dot_f32(a, b):
    return jnp.dot(a, b, precision=HIGHEST, preferred_element_type=F32)


def _pad_rows(a, rows=CHUNK):
    if a.shape[0] == rows:
        return a
    return jnp.concatenate([a, jnp.zeros((rows - a.shape[0], a.shape[1]), a.dtype)], axis=0)


def _col(a, lane, width):
    return jnp.broadcast_to(a[:, lane:lane + 1], (a.shape[0], width))


def _iotas(m):
    row = lax.broadcasted_iota(jnp.int32, (m, CHUNK), 0)
    col = lax.broadcasted_iota(jnp.int32, (m, CHUNK), 1)
    return row, col


def _valid_rows(m, chunk_idx, first_lo, hi):
    r = lax.broadcasted_iota(jnp.int32, (m, 1), 0)
    lo = jnp.where(chunk_idx == 0, first_lo, 0)
    return (r >= lo) & (r < hi)


def _pack_lanes(per_seq):
    stride = CHUNK // len(per_seq)
    out = per_seq[0]
    for b in range(1, len(per_seq)):
        out = out + pltpu.roll(per_seq[b], b * stride, 1)
    return out, stride


def _decay_terms(la):
    m = la.shape[0]
    row, col = _iotas(m)
    lower = (row >= col).astype(F32)
    cum = _dot_f32(lower, _pad_rows(la))
    last = cum[m - 1:m, :]
    return cum, _pad_rows(cum).T, jnp.exp(cum), jnp.exp(last - cum), jnp.exp(last)


def _decay_matrix(cum, cum_t, lane):
    m = cum.shape[0]
    row, col = _iotas(m)
    keep = row >= col
    seg = _col(cum, lane, CHUNK) - jnp.broadcast_to(cum_t[lane:lane + 1, :], (m, CHUNK))
    return jnp.where(keep, jnp.exp(jnp.where(keep, seg, 0.0)), 0.0)


def _shift_rows_matrix(rows):
    r = lax.broadcasted_iota(jnp.int32, (rows, rows), 0)
    c = lax.broadcasted_iota(jnp.int32, (rows, rows), 1)
    blocks = [jnp.where(r - c == CONV_K - 1 - j, 1.0, 0.0) for j in range(CONV_K - 1)]
    return jnp.concatenate(blocks, axis=0).astype(BF16)


def _causal_conv(ext_ref, cur, w_ref, shift_mat):
    rows = cur.shape[0]
    w_last = w_ref[CONV_K - 1:CONV_K, :]
    ext_ref[SUBLANES:2 * SUBLANES, :] = cur[0:SUBLANES]
    head = cur[0:SUBLANES] * w_last
    for j in range(CONV_K - 1):
        start = SUBLANES - (CONV_K - 1) + j
        head = head + ext_ref[start:start + SUBLANES, :] * w_ref[j:j + 1, :]
    ext_ref[0:SUBLANES, :] = cur[rows - SUBLANES:rows]
    if rows == SUBLANES:
        return head
    shifted = jnp.dot(shift_mat, cur.astype(BF16), preferred_element_type=F32)
    acc = cur * w_last
    for j in range(CONV_K - 1):
        acc = acc + shifted[j * rows:(j + 1) * rows] * w_ref[j:j + 1, :]
    return jnp.concatenate([head, acc[SUBLANES:]], axis=0)


def _rms(x, w):
    ms = jnp.mean(x * x, axis=-1, keepdims=True)
    return x * lax.rsqrt(ms + NORM_EPS) * w


def _prenorm_kernel(x_ref, g_ref, h_ref):
    h_ref[...] = _rms(x_ref[...], g_ref[...]).astype(BF16)


def _prenorm(x, g_all, layer, tm):
    n = x.shape[0]
    return pl.pallas_call(
        _prenorm_kernel,
        grid=(n // tm,),
        in_specs=[pl.BlockSpec((tm, D_MODEL), lambda i: (i, 0)),
                  pl.BlockSpec((None, 1, D_MODEL), lambda i: (layer, 0, 0))],
        out_specs=pl.BlockSpec((tm, D_MODEL), lambda i: (i, 0)),
        out_shape=jax.ShapeDtypeStruct((n, D_MODEL), BF16),
        compiler_params=pltpu.CompilerParams(dimension_semantics=("arbitrary",)),
        name="prenorm",
    )(x, g_all)


def _wprep_kernel(offs_ref, a0_ref, a1_ref, a2_ref, b_ref, o_ref):
    j = pl.program_id(1)
    o_ref[0:CHUNK, :] = a0_ref[...].astype(BF16)
    o_ref[CHUNK:2 * CHUNK, :] = a1_ref[...].astype(BF16)

    @pl.when(j < W_STEPS - 1)
    def _():
        o_ref[2 * CHUNK:3 * CHUNK, :] = a2_ref[...].astype(BF16)

    @pl.when(j == W_STEPS - 1)
    def _():
        small = jnp.concatenate([a2_ref[0:SUBLANES, :], b_ref[...],
                                 jnp.zeros((CHUNK - 2 * SUBLANES, D_MODEL), F32)], axis=0)
        o_ref[2 * CHUNK:3 * CHUNK, :] = small.astype(BF16)


def _wprep(w_t):
    offs = np.concatenate([np.arange(a, b, CHUNK) for a, b in W_SEGMENTS] + [[1536]]).astype(np.int32)
    assert offs.shape[0] == W_BLOCKS and not (offs % SUBLANES).any()
    offs = offs // SUBLANES
    def source_block(k):
        return pl.BlockSpec(
            (pl.Element(CHUNK), pl.Element(D_MODEL)),
            lambda l, j, offs: ((l * (IN_W // SUBLANES) + offs[W_PER_STEP * j + k]) * SUBLANES, 0))

    grid_spec = pltpu.PrefetchScalarGridSpec(
        num_scalar_prefetch=1,
        grid=(DEPTH, W_STEPS),
        in_specs=[source_block(k) for k in range(W_PER_STEP)]
        + [pl.BlockSpec((SUBLANES, D_MODEL), lambda l, j, offs: ((l * IN_W + 6408) // SUBLANES, 0))],
        out_specs=pl.BlockSpec((None, W_PER_STEP * CHUNK, D_MODEL), lambda l, j, offs: (l, j, 0)),
    )
    w_rows = w_t.reshape(DEPTH * IN_W, D_MODEL)
    return pl.pallas_call(
        _wprep_kernel, grid_spec=grid_spec,
        out_shape=jax.ShapeDtypeStruct((DEPTH, IN_W_PAD, D_MODEL), BF16),
        compiler_params=pltpu.CompilerParams(dimension_semantics=("arbitrary", "arbitrary")),
        name="wprep",
    )(jnp.asarray(offs), w_rows, w_rows, w_rows, w_rows)


def _inproj_kernel(h_ref, w_ref, o_ref):
    o_ref[...] = lax.dot_general(h_ref[...], w_ref[...], (((1,), (1,)), ((), ())),
                                 preferred_element_type=F32)


def _inproj(h, w_all, layer, tm):
    n = h.shape[0]
    return pl.pallas_call(
        _inproj_kernel,
        grid=(IN_W_PAD // PROJ_TN, n // tm),
        in_specs=[
            pl.BlockSpec((tm, D_MODEL), lambda j, i: (i, 0)),
            pl.BlockSpec((None, PROJ_TN, D_MODEL), lambda j, i: (layer, j, 0)),
        ],
        out_specs=pl.BlockSpec((tm, PROJ_TN), lambda j, i: (i, j)),
        out_shape=jax.ShapeDtypeStruct((n, IN_W_PAD), F32),
        compiler_params=pltpu.CompilerParams(
            dimension_semantics=("arbitrary", "arbitrary"), vmem_limit_bytes=LARGE_VMEM_LIMIT),
        name="inproj",
    )(h, w_all)


def _embed_prompt_kernel(x_ref, meta_ref, g_ref, o_ref, h_ref):
    c = pl.program_id(0)

    @pl.when(c == 0)
    def _():
        head = jnp.concatenate([jnp.zeros((PROMPT_PAD, D_MODEL), F32), meta_ref[...]], axis=0)
        for b in range(BATCH):
            o_ref[b * CHUNK:(b + 1) * CHUNK, :] = head

    @pl.when(c > 0)
    def _():
        for b in range(BATCH):
            o_ref[b * CHUNK:(b + 1) * CHUNK, :] = x_ref[b]

    h_ref[...] = _rms(o_ref[...], g_ref[...]).astype(BF16)


def _embed_prompt(x_prompt, meta_tokens, g_all):
    x4 = x_prompt.reshape(BATCH, SEQ // CHUNK, CHUNK, D_MODEL)
    row = pl.BlockSpec((BATCH * CHUNK, D_MODEL), lambda c: (c, 0))
    return pl.pallas_call(
        _embed_prompt_kernel,
        grid=(PROMPT_CHUNKS,),
        in_specs=[pl.BlockSpec((BATCH, None, CHUNK, D_MODEL), lambda c: (0, jnp.maximum(c - 1, 0), 0, 0)),
                  pl.BlockSpec((N_META, D_MODEL), lambda c: (0, 0)),
                  pl.BlockSpec((None, 1, D_MODEL), lambda c: (0, 0, 0))],
        out_specs=[row, row],
        out_shape=[jax.ShapeDtypeStruct((N_PROMPT, D_MODEL), F32),
                   jax.ShapeDtypeStruct((N_PROMPT, D_MODEL), BF16)],
        compiler_params=pltpu.CompilerParams(dimension_semantics=("arbitrary",), vmem_limit_bytes=VMEM_LIMIT),
        name="embed_prompt",
    )(x4, meta_tokens, g_all)


def _outproj_kernel(with_next, by_sequence, *refs):
    if with_next:
        ya_ref, yb_ref, yc_ref, yd_ref, w_ref, x_ref, g_ref, m_ref, gn_ref, o_ref, h_ref = refs
    else:
        ya_ref, yb_ref, yc_ref, yd_ref, w_ref, x_ref, g_ref, m_ref, o_ref = refs
    acc = None
    for g, y_ref in enumerate((ya_ref, yb_ref, yc_ref, yd_ref)):
        part = jnp.dot(y_ref[...], w_ref[g * GROUP_W:(g + 1) * GROUP_W, :], preferred_element_type=F32)
        acc = part if acc is None else acc + part
    x_new = jnp.where(m_ref[...] > 0.0, x_ref[...] + _rms(acc, g_ref[...]), 0.0)
    if by_sequence:
        for b in range(BATCH):
            o_ref[b] = x_new[b * CHUNK:(b + 1) * CHUNK]
    else:
        o_ref[...] = x_new
    if with_next:
        h_ref[...] = _rms(x_new, gn_ref[...]).astype(BF16)


def _outproj(ys, w_all, x, post_all, rowmask, pre_all, layer, tm, by_sequence=False):
    n = x.shape[0]
    with_next = layer + 1 < DEPTH
    yspec = pl.BlockSpec((tm, GROUP_W), lambda i: (i, 0))
    row = pl.BlockSpec((tm, D_MODEL), lambda i: (i, 0))
    in_specs = [yspec, yspec, yspec, yspec,
                pl.BlockSpec((None, D_MODEL, D_MODEL), lambda i: (layer, 0, 0)),
                row,
                pl.BlockSpec((None, 1, D_MODEL), lambda i: (layer, 0, 0)),
                pl.BlockSpec((tm, 1), lambda i: (i, 0))]
    args = list(ys) + [w_all, x, post_all, rowmask]
    if by_sequence:
        assert tm == BATCH * CHUNK and not with_next
        out_specs = [pl.BlockSpec((BATCH, None, CHUNK, D_MODEL), lambda i: (0, jnp.maximum(i - 1, 0), 0, 0))]
        out_shape = [jax.ShapeDtypeStruct((BATCH, SEQ // CHUNK, CHUNK, D_MODEL), F32)]
    else:
        out_specs = [row]
        out_shape = [jax.ShapeDtypeStruct((n, D_MODEL), F32)]
    if with_next:
        in_specs.append(pl.BlockSpec((None, 1, D_MODEL), lambda i: (layer + 1, 0, 0)))
        args.append(pre_all)
        out_specs.append(row)
        out_shape.append(jax.ShapeDtypeStruct((n, D_MODEL), BF16))
    res = pl.pallas_call(
        functools.partial(_outproj_kernel, with_next, by_sequence),
        grid=(n // tm,), in_specs=in_specs, out_specs=out_specs, out_shape=out_shape,
        compiler_params=pltpu.CompilerParams(
            dimension_semantics=("arbitrary",), vmem_limit_bytes=VMEM_LIMIT),
        name="outproj",
    )(*args)
    return (res[0], res[1]) if with_next else (res[0], None)


def _init_state(c, has_init, s_scr, s0_ref, ext=None, c0_ref=None):
    @pl.when(c == 0)
    def _():
        if has_init:
            s_scr[...] = s0_ref[...]
            if ext is not None:
                ext[:, 0:SUBLANES, :] = c0_ref[...]
        else:
            s_scr[...] = jnp.zeros_like(s_scr)
            if ext is not None:
                ext[:, 0:SUBLANES, :] = jnp.zeros((ext.shape[0], SUBLANES, ext.shape[2]), F32)


def _run_staggered(groups, skew):
    pending, live, tick = list(groups), [], 0
    while pending or live:
        if pending and tick % skew == 0:
            live.append(pending.pop(0))
        for gen in list(live):
            if next(gen, "done") == "done":
                live.remove(gen)
        tick += 1


def _ssd_stages(nseq, rows, layer, has_init, state_t, nchunks, first_lo, hi, *refs):
    if has_init:
        (z_ref, x_ref, bc_ref, sm_ref, s0_ref, c0_ref, _, cw_ref, cb_ref, lp_ref, nw_ref, d_ref,
         y_ref, so_ref, s_scr, ext) = refs
    else:
        (z_ref, x_ref, bc_ref, sm_ref, cw_ref, cb_ref, lp_ref, nw_ref, d_ref,
         y_ref, so_ref, s_scr, ext) = refs
        s0_ref = c0_ref = None
    c = pl.program_id(1)
    m = rows
    _init_state(c, has_init, s_scr, s0_ref, ext, c0_ref)

    valid = _valid_rows(m, c, first_lo, hi)
    lane = lax.broadcasted_iota(jnp.int32, (1, CHUNK), 1)
    head_lane = (lane >= LANE_DT) & (lane < LANE_DT + SSD_HEADS)
    dt_all = _softplus(sm_ref[...] + lp_ref[ROW_SSD_DTB:ROW_SSD_DTB + 1, :])
    la_all = -jnp.exp(lp_ref[ROW_SSD_ALOG:ROW_SSD_ALOG + 1, :]) * dt_all
    dts = [jnp.where(valid, dt_all[b * m:(b + 1) * m], 0.0) for b in range(nseq)]
    la, stride = _pack_lanes([jnp.where(valid & head_lane, la_all[b * m:(b + 1) * m], 0.0) for b in range(nseq)])
    cum, cum_t, e_cum, e_rem, e_last = _decay_terms(la)

    shift_mat = _shift_rows_matrix(m) if m > SUBLANES else None
    hpg = SSD_HEADS // SSD_GROUPS
    c_off = GROUP_W + SSD_GROUPS * SSD_STATE
    gw = GROUP_W // SSD_GROUPS
    ys = [None] * nseq

    def sequence(b):
        rows_b = slice(b * m, (b + 1) * m)
        cur = jnp.concatenate([x_ref[rows_b, :], bc_ref[rows_b, :]], axis=1)
        a_b = _silu(_causal_conv(ext.at[b], cur, cw_ref, shift_mat) + cb_ref[...])
        yield
        scores = []
        for g in range(SSD_GROUPS):
            b_g = a_b[:, GROUP_W + g * SSD_STATE:GROUP_W + (g + 1) * SSD_STATE]
            c_g = a_b[:, c_off + g * SSD_STATE:c_off + (g + 1) * SSD_STATE]
            scores.append((_dot_nt(c_g, _pad_rows(b_g)), b_g, c_g))
        yield
        heads = []
        for h in range(SSD_HEADS):
            sc, b_g, c_g = scores[h // hpg]
            ln = b * stride + LANE_DT + h
            x_h = a_b[:, h * SSD_HEAD_DIM:(h + 1) * SSD_HEAD_DIM]
            v_h = x_h * _col(dts[b], LANE_DT + h, SSD_HEAD_DIM)
            s_h = s_scr[b, h]
            v_w = v_h * _col(e_rem, ln, SSD_HEAD_DIM)
            if state_t:
                from_state = _dot_nt(c_g, s_h)
                s_scr[b, h] = s_h * e_last[:, ln:ln + 1] + _dot_tn(v_w, b_g)
            else:
                from_state = _dot(c_g, s_h)
                s_scr[b, h] = s_h * e_last[:, ln:ln + 1] + _dot_tn(b_g, v_w)
            o = (_dot(sc * _decay_matrix(cum, cum_t, ln), _pad_rows(v_h))
                 + from_state * _col(e_cum, ln, SSD_HEAD_DIM))
            heads.append(o + d_ref[layer, h] * x_h)
            yield
        y = jnp.concatenate(heads, axis=1) * _silu(z_ref[rows_b, :])
        outs = [_rms(y[:, g * gw:(g + 1) * gw], nw_ref[:, g * gw:(g + 1) * gw]) for g in range(SSD_GROUPS)]
        ys[b] = jnp.concatenate(outs, axis=1)

    def finish():
        y_ref[...] = jnp.concatenate(ys, axis=0).astype(y_ref.dtype)

        @pl.when(c == nchunks - 1)
        def _():
            so_ref[...] = s_scr[...]

    return [sequence(b) for b in range(nseq)], finish


def _ret_stages(nseq, seq_group, rows, has_init, nchunks, first_lo, hi, *refs):
    if has_init:
        qk_ref, v_ref, g_ref, s0_ref, _, lp_ref, nw_ref, y_ref, so_ref, s_scr = refs
    else:
        qk_ref, v_ref, g_ref, lp_ref, nw_ref, y_ref, so_ref, s_scr = refs
        s0_ref = None
    c = pl.program_id(1)
    m = rows
    _init_state(c, has_init, s_scr, s0_ref)

    valid = _valid_rows(m, c, first_lo, hi)
    lane = lax.broadcasted_iota(jnp.int32, (1, CHUNK), 1)
    log_gamma = jnp.broadcast_to(lp_ref[ROW_LOG_GAMMA:ROW_LOG_GAMMA + 1, :], (m, CHUNK))
    la_one = jnp.where(valid & (lane < RET_HEADS), log_gamma, 0.0)
    la, stride = _pack_lanes([la_one] * nseq)
    cum, cum_t, e_cum, e_rem, e_last = _decay_terms(la)

    qk_all = qk_ref[...]
    v_all = v_ref[...]
    all_units = []
    for b in range(nseq):
        qk = qk_all[b * m:(b + 1) * m]
        v = jnp.where(valid, v_all[b * m:(b + 1) * m], 0.0)
        for h in range(RET_HEADS):
            q_h = qk[:, h * RET_DK:(h + 1) * RET_DK]
            k_h = qk[:, (RET_HEADS + h) * RET_DK:(RET_HEADS + h + 1) * RET_DK] * (RET_DK ** -0.5)
            s_h = s_scr[b, h]
            all_units.append(dict(b=b, h=h, ln=b * stride + h, k=k_h, v=v[:, h * RET_DV:(h + 1) * RET_DV],
                                  s=s_h, raw=_dot_nt(q_h, _pad_rows(k_h)), q_s=_dot(q_h, s_h)))
    ys = [None] * nseq

    def group(b0):
        units = all_units[b0 * RET_HEADS:(b0 + seq_group) * RET_HEADS]
        scs = [u["raw"] * _decay_matrix(cum, cum_t, u["ln"]) for u in units]
        yield
        os_ = [_dot(sc, _pad_rows(u["v"])) + u["q_s"] * _col(e_cum, u["ln"], RET_DV)
               for sc, u in zip(scs, units)]
        yield
        for u in units:
            ln = u["ln"]
            s_scr[u["b"], u["h"]] = (u["s"] * e_last[:, ln:ln + 1]
                                     + _dot_tn(u["k"] * _col(e_rem, ln, RET_DK), u["v"]))
        yield
        for i in range(seq_group):
            heads = [_rms(os_[i * RET_HEADS + h], nw_ref[:, h * RET_DV:(h + 1) * RET_DV])
                     for h in range(RET_HEADS)]
            rows_b = slice((b0 + i) * m, (b0 + i + 1) * m)
            ys[b0 + i] = jnp.concatenate(heads, axis=1) * _silu(g_ref[rows_b, :])

    def finish():
        y_ref[...] = jnp.concatenate(ys, axis=0).astype(y_ref.dtype)

        @pl.when(c == nchunks - 1)
        def _():
            so_ref[...] = s_scr[...]

    return [group(b0) for b0 in range(0, nseq, seq_group)], finish


def _gdn_stages(nseq, seq_group, rows, has_init, nchunks, first_lo, hi, *refs):
    if has_init:
        (q_ref, k_ref, v_ref, g_ref, sm_ref, s0_ref, c0_ref, _, cw_ref, lp_ref, nw_ref,
         y_ref, so_ref, s_scr, ext) = refs
    else:
        (q_ref, k_ref, v_ref, g_ref, sm_ref, cw_ref, lp_ref, nw_ref,
         y_ref, so_ref, s_scr, ext) = refs
        s0_ref = c0_ref = None
    c = pl.program_id(1)
    m = rows
    _init_state(c, has_init, s_scr, s0_ref, ext, c0_ref)

    shift_mat = _shift_rows_matrix(m) if m > SUBLANES else None
    valid = _valid_rows(m, c, first_lo, hi)
    lane = lax.broadcasted_iota(jnp.int32, (1, CHUNK), 1)
    head_lane = (lane >= LANE_DECAY) & (lane < LANE_DECAY + GDN_HEADS)
    sm = sm_ref[...]
    beta_all = _sigmoid(sm)
    la_all = (-jnp.exp(lp_ref[ROW_GDN_ALOG:ROW_GDN_ALOG + 1, :])
              * _softplus(sm + lp_ref[ROW_GDN_DTB:ROW_GDN_DTB + 1, :]))
    betas = [jnp.where(valid, beta_all[b * m:(b + 1) * m], 0.0) for b in range(nseq)]
    la, stride = _pack_lanes([jnp.where(valid & head_lane, la_all[b * m:(b + 1) * m], 0.0) for b in range(nseq)])
    cum, cum_t, e_cum, e_rem, e_last = _decay_terms(la)
    row, col = _iotas(m)
    levels = max(1, (min(m, hi) - 1).bit_length())

    ys = [None] * nseq

    def joining_mask(shift):
        same_big = lax.shift_right_logical(row, shift + 1) == lax.shift_right_logical(col, shift + 1)
        same_small = lax.shift_right_logical(row, shift) == lax.shift_right_logical(col, shift)
        return same_big & jnp.logical_not(same_small) & (row > col)

    def group(b0):
        units = []
        for b in range(b0, b0 + seq_group):
            rows_b = slice(b * m, (b + 1) * m)
            cur = jnp.concatenate([q_ref[rows_b, :], k_ref[rows_b, :], v_ref[rows_b, :]], axis=1)
            a_b = _silu(_causal_conv(ext.at[b], cur, cw_ref, shift_mat))
            yield
            for h in range(GDN_HEADS):
                ln = b * stride + LANE_DECAY + h
                q_h = a_b[:, h * GDN_DK:(h + 1) * GDN_DK]
                k_h = a_b[:, GROUP_W + h * GDN_DK:GROUP_W + (h + 1) * GDN_DK]
                v_h = a_b[:, 2 * GROUP_W + h * GDN_DV:2 * GROUP_W + (h + 1) * GDN_DV]
                q_h = q_h * lax.rsqrt(jnp.sum(q_h * q_h, axis=-1, keepdims=True) + 1e-6) * (GDN_DK ** -0.5)
                k_h = k_h * lax.rsqrt(jnp.sum(k_h * k_h, axis=-1, keepdims=True) + 1e-6)
                k_h = jnp.where(valid, k_h, 0.0)
                b_h = _col(betas[b], LANE_BETA + h, GDN_DK)
                k_b = k_h * b_h
                decay = _decay_matrix(cum, cum_t, ln)
                k_pad = _pad_rows(k_h)
                units.append(dict(
                    b=b, h=h, ln=ln, q=q_h, k=k_h, k_b=k_b, decay=decay,
                    a=jnp.where(row > col, _dot_nt(k_b, k_pad) * decay, 0.0),
                    v_b=_pad_rows(jnp.where(valid, v_h, 0.0) * b_h),
                    qk=_dot_nt(q_h, k_pad) * decay))
                yield
        eye = (row == col).astype(F32)
        mask = joining_mask(0)
        ts = [eye - jnp.where(mask, u["a"], 0.0) for u in units]
        for shift in range(1, levels):
            mask = joining_mask(shift)
            inner = [_dot(jnp.where(mask, u["a"], 0.0), _pad_rows(t)) for u, t in zip(units, ts)]
            yield
            ts = [t - _dot(t, _pad_rows(x)) for t, x in zip(ts, inner)]
            yield
        us = [_dot(t, u["v_b"]) for t, u in zip(ts, units)]
        ws = [_dot(t, _pad_rows(u["k_b"] * _col(e_cum, u["ln"], GDN_DK))) for t, u in zip(ts, units)]
        yield
        ss = [s_scr[u["b"], u["h"]] for u in units]
        v_news = [x - _dot(w, s_h) for x, w, s_h in zip(us, ws, ss)]
        yield
        os_ = [_dot(u["q"] * _col(e_cum, u["ln"], GDN_DK), s_h) + _dot(u["qk"], _pad_rows(v_new))
               for u, s_h, v_new in zip(units, ss, v_news)]
        yield
        for u, s_h, v_new in zip(units, ss, v_news):
            ln = u["ln"]
            s_scr[u["b"], u["h"]] = (s_h * e_last[:, ln:ln + 1]
                                     + _dot_tn(u["k"] * _col(e_rem, ln, GDN_DK), v_new))
        yield
        for i in range(seq_group):
            heads = [_rms(o, nw_ref[...]) for o in os_[i * GDN_HEADS:(i + 1) * GDN_HEADS]]
            rows_b = slice((b0 + i) * m, (b0 + i + 1) * m)
            ys[b0 + i] = jnp.concatenate(heads, axis=1) * _silu(g_ref[rows_b, :])

    def finish():
        y_ref[...] = jnp.concatenate(ys, axis=0).astype(y_ref.dtype)

        @pl.when(c == nchunks - 1)
        def _():
            so_ref[...] = s_scr[...]

    return [group(b0) for b0 in range(0, nseq, seq_group)], finish


def _swa_stages(nseq, seq_group, rows, layer, is_prompt, *refs):
    if is_prompt:
        q_ref, g_ref, kvc_ref, kvp_ref, sink_ref, y_ref = refs
    else:
        q_ref, g_ref, kvc_ref, kp_ref, vp_ref, sink_ref, y_ref = refs
    n = pl.program_id(1)
    m = rows
    if is_prompt:
        lo_prev = jnp.where(n == 0, CHUNK, jnp.where(n == 1, PROMPT_PAD, 0))
        lo_cur = jnp.where(n == 0, PROMPT_PAD, 0)
    else:
        lo_prev, lo_cur = 0, 0
    qi = lax.broadcasted_iota(jnp.int32, (m, 2 * CHUNK), 0)
    kj = lax.broadcasted_iota(jnp.int32, (m, 2 * CHUNK), 1)
    dist = qi + WINDOW - kj
    key_ok = ((kj < CHUNK) & (kj >= lo_prev)) | (kj >= CHUNK + lo_cur)
    visible = (dist >= 0) & (dist <= WINDOW) & key_ok
    dist_f = dist.astype(F32)

    grp = SWA_HEADS // SWA_KV_HEADS
    ys = [None] * nseq

    def group(b0):
        units = []
        for b in range(b0, b0 + seq_group):
            q = q_ref[b * m:(b + 1) * m, :]
            kvc = _pad_rows(kvc_ref[b * m:(b + 1) * m, :])
            if is_prompt:
                k_prev = kvp_ref[b * CHUNK:(b + 1) * CHUNK, 0:KV_W]
                v_prev = kvp_ref[b * CHUNK:(b + 1) * CHUNK, KV_W:2 * KV_W]
            else:
                k_prev, v_prev = kp_ref[b], vp_ref[b]
            k_all = jnp.concatenate([k_prev, kvc[:, 0:KV_W]], axis=0)
            v_all = jnp.concatenate([v_prev, kvc[:, KV_W:2 * KV_W]], axis=0)
            for h in range(SWA_HEADS):
                kvh = h // grp
                units.append(dict(h=h, q=q[:, h * SWA_HEAD_DIM:(h + 1) * SWA_HEAD_DIM],
                                  k=k_all[:, kvh * SWA_HEAD_DIM:(kvh + 1) * SWA_HEAD_DIM],
                                  v=v_all[:, kvh * SWA_HEAD_DIM:(kvh + 1) * SWA_HEAD_DIM]))
        ss = [jnp.where(visible,
                        _dot_nt(u["q"], u["k"]) * (SWA_HEAD_DIM ** -0.5)
                        - 2.0 ** (-8.0 * (u["h"] + 1) / SWA_HEADS) * dist_f, -1e30) for u in units]
        yield
        mxs = [jnp.maximum(jnp.max(s, axis=-1, keepdims=True), sink_ref[layer, u["h"]])
               for s, u in zip(ss, units)]
        es = [jnp.exp(s - mx) for s, mx in zip(ss, mxs)]
        yield
        dens = [jnp.sum(e, axis=-1, keepdims=True) + jnp.exp(sink_ref[layer, u["h"]] - mx)
                for e, mx, u in zip(es, mxs, units)]
        os_ = [_dot(e, u["v"]) / den for e, u, den in zip(es, units, dens)]
        yield
        for i in range(seq_group):
            rows_b = slice((b0 + i) * m, (b0 + i + 1) * m)
            ys[b0 + i] = (jnp.concatenate(os_[i * SWA_HEADS:(i + 1) * SWA_HEADS], axis=1)
                          * _silu(g_ref[rows_b, :]))

    def finish():
        y_ref[...] = jnp.concatenate(ys, axis=0).astype(y_ref.dtype)

    return [group(b0) for b0 in range(0, nseq, seq_group)], finish


def _mixer_kernel(parts, *refs):
    starts = [0, sum(p[1] for p in parts), sum(p[1] + p[2] for p in parts)]
    gens, finishes = [], []
    for stages, n_in, n_out, n_scr in parts:
        mine = []
        for k, cnt in enumerate((n_in, n_out, n_scr)):
            mine += refs[starts[k]:starts[k] + cnt]
            starts[k] += cnt
        g, fin = stages(*mine)
        gens.append(g)
        finishes.append(fin)
    for w in range(0, len(gens), MIXERS_PER_WAVE):
        wave = gens[w:w + MIXERS_PER_WAVE]
        order = []
        for i in range(max(len(g) for g in wave)):
            order += [g[i] for g in wave if i < len(g)]
        _run_staggered(order, 1)
    for fin in finishes:
        fin()


def _geometry(sample):
    if sample:
        return (DEC_BATCH // SAMPLE_SEQS, 1), SAMPLE_SEQS, SUBLANES, N_SAMPLE, 0, DEC_SEQ
    return (1, PROMPT_CHUNKS), BATCH, CHUNK, N_PROMPT, PROMPT_PAD, CHUNK


def _rowblk(sample):
    return (lambda blk, c: blk) if sample else (lambda blk, c: c)


def _pcol(nrows, rowblk, width, col):
    assert col % width == 0
    return pl.BlockSpec((nrows, width), lambda blk, c: (rowblk(blk, c), col // width))


def _layer_param(shape, layer):
    return pl.BlockSpec((None,) + shape, lambda blk, c: (layer,) + (0,) * len(shape))


def _layer_seqs(nseq, shape, layer):
    return pl.BlockSpec((None, nseq) + shape, lambda blk, c: (layer, blk) + (0,) * len(shape))


def _seqs_out(nseq, shape):
    return pl.BlockSpec((nseq,) + shape, lambda blk, c: (blk,) + (0,) * len(shape))


_SMEM = pl.BlockSpec(memory_space=pltpu.SMEM)
_MIXER_PARAMS = pltpu.CompilerParams(dimension_semantics=("arbitrary", "arbitrary"),
                                     vmem_limit_bytes=LARGE_VMEM_LIMIT)


def _state_output(sample, nseq, state, layer, acc, in_specs, args):
    if not sample:
        return _seqs_out(nseq, state), jax.ShapeDtypeStruct((nseq,) + state, F32), {}
    in_specs.append(pl.BlockSpec(memory_space=pl.ANY))
    args.append(acc)
    spec = pl.BlockSpec((None, nseq) + state, lambda blk, c: (layer, blk) + (0,) * len(state))
    return spec, jax.ShapeDtypeStruct(acc.shape, F32), {len(args) - 1: 1}


def _ssd(p, prm, layer, init, acc=None):
    sample = init is not None
    grid, nseq, rows, total, first_lo, hi = _geometry(sample)
    rb, n = _rowblk(sample), nseq * rows
    state = (SSD_HEADS, SSD_HEAD_DIM, SSD_STATE) if sample else (SSD_HEADS, SSD_STATE, SSD_HEAD_DIM)
    in_specs = [_pcol(n, rb, GROUP_W, COL_Z), _pcol(n, rb, GROUP_W, COL_X),
                _pcol(n, rb, GROUP_W, COL_BC), _pcol(n, rb, CHUNK, COL_SMALL)]
    args = [p, p, p, p]
    if sample:
        in_specs += [_layer_seqs(nseq, state, layer), _layer_seqs(nseq, (SUBLANES, SSD_CONV_W), layer)]
        args += list(init)
    state_spec, state_shape, aliases = _state_output(sample, nseq, state, layer, acc, in_specs, args)
    in_specs += [_layer_param((CONV_K, SSD_CONV_W), layer), _layer_param((1, SSD_CONV_W), layer),
                 _layer_param((SUBLANES, CHUNK), layer), _layer_param((1, GROUP_W), layer), _SMEM]
    args += [prm["ssd_conv_w"], prm["ssd_conv_b"], prm["lanes"], prm["ssd_norm"], prm["ssd_d"]]
    return dict(
        name="ssd", grid=grid,
        stages=functools.partial(_ssd_stages, nseq, rows, layer, sample, sample, grid[1], first_lo, hi),
        in_specs=in_specs, args=args,
        out_specs=[pl.BlockSpec((n, GROUP_W), lambda blk, c: (rb(blk, c), 0)), state_spec],
        out_shape=[jax.ShapeDtypeStruct((total, GROUP_W), BF16), state_shape],
        aliases=aliases,
        scratch=[pltpu.VMEM((nseq,) + state, F32), pltpu.VMEM((nseq, 2 * SUBLANES, SSD_CONV_W), F32)])


def _ret(p, prm, layer, init, acc=None):
    sample = init is not None
    grid, nseq, rows, total, first_lo, hi = _geometry(sample)
    rb, n = _rowblk(sample), nseq * rows
    state = (RET_HEADS, RET_DK, RET_DV)
    in_specs = [_pcol(n, rb, GROUP_W, COL_QKR), _pcol(n, rb, GROUP_W, COL_VR), _pcol(n, rb, GROUP_W, COL_GR)]
    args = [p, p, p]
    if sample:
        in_specs += [_layer_seqs(nseq, state, layer)]
        args += list(init)
    state_spec, state_shape, aliases = _state_output(sample, nseq, state, layer, acc, in_specs, args)
    in_specs += [_layer_param((SUBLANES, CHUNK), layer), _layer_param((1, GROUP_W), layer)]
    args += [prm["lanes"], prm["ret_norm"]]
    return dict(
        name="ret", grid=grid,
        stages=functools.partial(_ret_stages, nseq, nseq if sample else 1, rows, sample, grid[1], first_lo, hi),
        in_specs=in_specs, args=args,
        out_specs=[pl.BlockSpec((n, GROUP_W), lambda blk, c: (rb(blk, c), 0)), state_spec],
        out_shape=[jax.ShapeDtypeStruct((total, GROUP_W), BF16), state_shape],
        aliases=aliases,
        scratch=[pltpu.VMEM((nseq,) + state, F32)])


def _gdn(p, prm, layer, init, acc=None):
    sample = init is not None
    grid, nseq, rows, total, first_lo, hi = _geometry(sample)
    rb, n = _rowblk(sample), nseq * rows
    state = (GDN_HEADS, GDN_DK, GDN_DV)
    in_specs = [_pcol(n, rb, GROUP_W, COL_QD), _pcol(n, rb, GROUP_W, COL_KD), _pcol(n, rb, GROUP_W, COL_VD),
                _pcol(n, rb, GROUP_W, COL_GD), _pcol(n, rb, CHUNK, COL_SMALL)]
    args = [p, p, p, p, p]
    if sample:
        in_specs += [_layer_seqs(nseq, state, layer), _layer_seqs(nseq, (SUBLANES, GDN_CONV_W), layer)]
        args += list(init)
    state_spec, state_shape, aliases = _state_output(sample, nseq, state, layer, acc, in_specs, args)
    in_specs += [_layer_param((CONV_K, GDN_CONV_W), layer), _layer_param((SUBLANES, CHUNK), layer),
                 _layer_param((1, GDN_DV), layer)]
    args += [prm["gdn_conv_w"], prm["lanes"], prm["gdn_norm"]]
    return dict(
        name="gdn", grid=grid,
        stages=functools.partial(_gdn_stages, nseq, 2 if sample else 1, rows, sample, grid[1], first_lo, hi),
        in_specs=in_specs, args=args,
        out_specs=[pl.BlockSpec((n, GROUP_W), lambda blk, c: (rb(blk, c), 0)), state_spec],
        out_shape=[jax.ShapeDtypeStruct((total, GROUP_W), BF16), state_shape],
        aliases=aliases,
        scratch=[pltpu.VMEM((nseq,) + state, F32), pltpu.VMEM((nseq, 2 * SUBLANES, GDN_CONV_W), F32)])


def _swa(p, prm, layer, cache):
    sample = cache is not None
    grid, nseq, rows, total, _, _ = _geometry(sample)
    rb, n = _rowblk(sample), nseq * rows
    in_specs = [_pcol(n, rb, GROUP_W, COL_QA), _pcol(n, rb, GROUP_W, COL_GA), _pcol(n, rb, 2 * KV_W, COL_KV)]
    args = [p, p, p]
    if sample:
        in_specs += [_layer_seqs(nseq, (WINDOW, KV_W), layer), _layer_seqs(nseq, (WINDOW, KV_W), layer)]
        args += list(cache)
    else:
        in_specs += [pl.BlockSpec((n, 2 * KV_W), lambda blk, c: (jnp.maximum(c - 1, 0), COL_KV // (2 * KV_W)))]
        args += [p]
    in_specs += [_SMEM]
    args += [prm["swa_sinks"]]
    return dict(
        name="swa", grid=grid,
        stages=functools.partial(_swa_stages, nseq, nseq if sample else 1, rows, layer, not sample),
        in_specs=in_specs, args=args,
        out_specs=[pl.BlockSpec((n, GROUP_W), lambda blk, c: (rb(blk, c), 0))],
        out_shape=[jax.ShapeDtypeStruct((total, GROUP_W), BF16)],
        aliases={}, scratch=[])


def _run_mixers(parts, suffix):
    kernel_parts = tuple((p["stages"], len(p["in_specs"]), len(p["out_specs"]), len(p["scratch"]))
                         for p in parts)
    aliases, n_in, n_out = {}, 0, 0
    for p in parts:
        aliases.update({n_in + k: n_out + v for k, v in p["aliases"].items()})
        n_in += len(p["in_specs"])
        n_out += len(p["out_specs"])
    res = pl.pallas_call(
        functools.partial(_mixer_kernel, kernel_parts),
        grid=parts[0]["grid"],
        in_specs=[s for p in parts for s in p["in_specs"]],
        out_specs=[s for p in parts for s in p["out_specs"]],
        out_shape=[s for p in parts for s in p["out_shape"]],
        scratch_shapes=[s for p in parts for s in p["scratch"]],
        input_output_aliases=aliases,
        compiler_params=_MIXER_PARAMS,
        name="_".join(p["name"] for p in parts) + suffix,
    )(*[a for p in parts for a in p["args"]])
    outs, k = [], 0
    for p in parts:
        outs.append(list(res[k:k + len(p["out_specs"])]))
        k += len(p["out_specs"])
    return outs


def _lane_table(ssd_dt_bias, ssd_a_log, gdn_dt_bias, gdn_a_log):
    def put(v, lane):
        return jnp.pad(v.astype(F32), ((0, 0), (lane, CHUNK - lane - v.shape[1])))
    log_gamma = jnp.log1p(-jnp.exp2(-5.0 - jnp.arange(RET_HEADS, dtype=F32)))
    rows = [put(ssd_dt_bias, LANE_DT), put(ssd_a_log, LANE_DT), put(gdn_dt_bias, LANE_DECAY),
            put(gdn_a_log, LANE_DECAY), put(jnp.broadcast_to(log_gamma, (DEPTH, RET_HEADS)), 0)]
    rows += [jnp.zeros((DEPTH, CHUNK), F32)] * (SUBLANES - len(rows))
    return jnp.stack(rows, axis=1)


def _conv_state_pad(state):
    return jnp.pad(state, ((0, 0), (0, 0), (SUBLANES - (CONV_K - 1), 0), (0, 0)))


def kernel(x_prompt, x_sample, state_ssd, state_ssd_conv, cache_swa_k, cache_swa_v, state_ret, state_gdn,
           state_gdn_conv, meta_tokens, pre_norm, post_norm, w_in, w_out, ssd_conv_w, ssd_conv_b, ssd_dt_bias,
           ssd_a_log, ssd_d, ssd_norm, swa_sinks, ret_norm, gdn_conv_w, gdn_dt_bias, gdn_a_log, gdn_norm):
    xs = jnp.pad(x_sample, ((0, 0), (0, SUBLANES - DEC_SEQ), (0, 0))).reshape(N_SAMPLE, D_MODEL)
    mask_np = np.ones((PROMPT_CHUNKS, BATCH, CHUNK), np.float32)
    mask_np[0, :, :PROMPT_PAD] = 0.0
    mask_p = jnp.asarray(mask_np.reshape(N_PROMPT, 1))
    mask_s = jnp.asarray(np.tile(np.arange(SUBLANES) < DEC_SEQ, DEC_BATCH)[:, None], F32)

    w_in_p = _wprep(jnp.swapaxes(w_in, 1, 2))
    w_out_b = w_out.astype(BF16)
    pre_g = pre_norm.reshape(DEPTH, 1, D_MODEL)
    post_g = post_norm.reshape(DEPTH, 1, D_MODEL)
    prm = dict(
        lanes=_lane_table(ssd_dt_bias, ssd_a_log, gdn_dt_bias, gdn_a_log),
        ssd_conv_w=ssd_conv_w, ssd_conv_b=ssd_conv_b.reshape(DEPTH, 1, SSD_CONV_W),
        ssd_norm=ssd_norm.reshape(DEPTH, 1, GROUP_W), ssd_d=ssd_d, swa_sinks=swa_sinks,
        ret_norm=ret_norm.reshape(DEPTH, 1, GROUP_W), gdn_conv_w=gdn_conv_w,
        gdn_norm=gdn_norm.reshape(DEPTH, 1, GDN_DV))
    ssd_c0 = _conv_state_pad(state_ssd_conv)
    gdn_c0 = _conv_state_pad(state_gdn_conv)
    cache_k = cache_swa_k.reshape(DEPTH, DEC_BATCH, WINDOW, KV_W)
    cache_v = cache_swa_v.reshape(DEPTH, DEC_BATCH, WINDOW, KV_W)

    names = ("ssd", "ssd_conv", "swa_k", "swa_v", "ret", "gdn", "gdn_conv")
    p_out = {k: [] for k in names}
    s_out = {k: [] for k in ("ssd_conv", "swa_k", "swa_v", "gdn_conv")}
    tail = slice(CHUNK - (CONV_K - 1), CHUNK)
    state_ssd_t = jnp.swapaxes(state_ssd, 3, 4)
    s_ssd_t = jnp.zeros(state_ssd_t.shape, F32)
    s_ret = jnp.zeros(state_ret.shape, F32)
    s_gdn = jnp.zeros(state_gdn.shape, F32)
    xp, hp = _embed_prompt(x_prompt, meta_tokens, pre_g)
    hs = _prenorm(xs, pre_g, 0, N_SAMPLE)
    for l in range(DEPTH):
        pp = _inproj(hp, w_in_p, l, INPROJ_TM)
        (yd, p_gdn), (yb,), (ya, p_ssd), (yc, p_ret) = _run_mixers(
            [_gdn(pp, prm, l, None), _swa(pp, prm, l, None), _ssd(pp, prm, l, None), _ret(pp, prm, l, None)],
            "_prompt")
        xp, hp = _outproj((ya, yb, yc, yd), w_out_b, xp, post_g, mask_p, pre_g, l, PROMPT_TM,
                          by_sequence=l == DEPTH - 1)

        ps = _inproj(hs, w_in_p, l, N_SAMPLE)
        (yd, s_gdn), (ya, s_ssd_t), (yc, s_ret), (yb,) = _run_mixers(
            [_gdn(ps, prm, l, (state_gdn, gdn_c0), s_gdn), _ssd(ps, prm, l, (state_ssd_t, ssd_c0), s_ssd_t),
             _ret(ps, prm, l, (state_ret,), s_ret), _swa(ps, prm, l, (cache_k, cache_v))],
            "_sample")
        xs, hs = _outproj((ya, yb, yc, yd), w_out_b, xs, post_g, mask_s, pre_g, l, N_SAMPLE)

        pp4 = pp.reshape(PROMPT_CHUNKS, BATCH, CHUNK, IN_W_PAD)

        def last_chunk(r0, c0, c1):
            return lax.slice(pp4, (PROMPT_CHUNKS - 1, 0, r0, c0), (PROMPT_CHUNKS, BATCH, CHUNK, c1))[0]

        ps3 = ps.reshape(DEC_BATCH, SUBLANES, IN_W_PAD)
        p_out["ssd"].append(p_ssd)
        p_out["ssd_conv"].append(last_chunk(tail.start, COL_X, COL_X + SSD_CONV_W))
        p_out["swa_k"].append(last_chunk(0, COL_KV, COL_KV + KV_W))
        p_out["swa_v"].append(last_chunk(0, COL_KV + KV_W, COL_KV + 2 * KV_W))
        p_out["ret"].append(p_ret)
        p_out["gdn"].append(p_gdn)
        p_out["gdn_conv"].append(last_chunk(tail.start, COL_QD, COL_QD + GDN_CONV_W))
        s_out["ssd_conv"].append(ps3[:, 1:DEC_SEQ, COL_X:COL_X + SSD_CONV_W])
        s_out["swa_k"].append(ps3[:, :DEC_SEQ, COL_KV:COL_KV + KV_W])
        s_out["swa_v"].append(ps3[:, :DEC_SEQ, COL_KV + KV_W:COL_KV + 2 * KV_W])
        s_out["gdn_conv"].append(ps3[:, 1:DEC_SEQ, COL_QD:COL_QD + GDN_CONV_W])

    p_st = {k: jnp.stack(v) for k, v in p_out.items()}
    s_st = {k: jnp.stack(v) for k, v in s_out.items()}
    kv_shape = (SWA_KV_HEADS, SWA_HEAD_DIM)
    y_prompt = xp.reshape(BATCH, SEQ, D_MODEL)
    y_sample = xs.reshape(DEC_BATCH, SUBLANES, D_MODEL)[:, :DEC_SEQ]
    return (
        y_prompt, y_sample,
        p_st["ssd"], p_st["ssd_conv"],
        p_st["swa_k"].reshape((DEPTH, BATCH, WINDOW) + kv_shape),
        p_st["swa_v"].reshape((DEPTH, BATCH, WINDOW) + kv_shape),
        p_st["ret"], p_st["gdn"], p_st["gdn_conv"],
        jnp.swapaxes(s_ssd_t, 3, 4), s_st["ssd_conv"],
        jnp.concatenate([cache_swa_k[:, :, DEC_SEQ:], s_st["swa_k"].reshape((DEPTH, DEC_BATCH, DEC_SEQ) + kv_shape)], axis=2),
        jnp.concatenate([cache_swa_v[:, :, DEC_SEQ:], s_st["swa_v"].reshape((DEPTH, DEC_BATCH, DEC_SEQ) + kv_shape)], axis=2),
        s_ret, s_gdn, s_st["gdn_conv"],
    )
```

```python
import functools

import jax
import jax.numpy as jnp
import numpy as np
from jax import lax
from jax.experimental import pallas as pl
from jax.experimental.pallas import tpu as pltpu

F32 = jnp.float32
BF16 = jnp.bfloat16
HIGHEST = lax.Precision.HIGHEST

D_MODEL = 2048
BATCH = 4
SEQ = 2048
DEPTH = 4
DEC_BATCH = 32
DEC_SEQ = 4
N_META = 16
GROUP_W = 512
CONV_K = 4
NORM_EPS = 1e-6
WINDOW = 128

SSD_HEADS, SSD_HEAD_DIM, SSD_GROUPS, SSD_STATE = 8, 64, 2, 128
SWA_HEADS, SWA_KV_HEADS, SWA_HEAD_DIM = 8, 2, 64
RET_HEADS, RET_DK, RET_DV = 4, 64, 128
GDN_HEADS, GDN_DK, GDN_DV = 4, 128, 128
SSD_CONV_W = GROUP_W + 2 * SSD_GROUPS * SSD_STATE
GDN_CONV_W = 3 * GROUP_W
KV_W = SWA_KV_HEADS * SWA_HEAD_DIM

CHUNK = 128
SUBLANES = 8
PROMPT_PAD = CHUNK - N_META
PROMPT_ROWS = PROMPT_PAD + N_META + SEQ
PROMPT_CHUNKS = PROMPT_ROWS // CHUNK
N_PROMPT = BATCH * PROMPT_ROWS
N_SAMPLE = DEC_BATCH * SUBLANES
SAMPLE_SEQS = 8
MIXERS_PER_WAVE = 2

IN_W = 6416
IN_W_PAD = 6528
COL_Z, COL_X, COL_BC, COL_QA, COL_GA, COL_VR, COL_GR = 0, 512, 1024, 1536, 2048, 2560, 3072
COL_QD, COL_KD, COL_VD, COL_GD, COL_QKR, COL_KV, COL_SMALL = 3584, 4096, 4608, 5120, 5632, 6144, 6400
LANE_DT, LANE_BETA, LANE_DECAY = 0, 8, 12
ROW_SSD_DTB, ROW_SSD_ALOG, ROW_GDN_DTB, ROW_GDN_ALOG, ROW_LOG_GAMMA = 0, 1, 2, 3, 4

W_SEGMENTS = ((0, 1536), (1544, 2056), (2312, 2824), (3336, 3848), (3848, 4360), (4360, 5896), (5896, 6408),
              (2824, 3336), (2056, 2312))
W_BLOCKS = IN_W_PAD // CHUNK
W_PER_STEP = 3
W_STEPS = W_BLOCKS // W_PER_STEP
assert W_STEPS * W_PER_STEP == W_BLOCKS

PROJ_TN = 2176
PROMPT_TM = 512
INPROJ_TM = 1088
VMEM_LIMIT = 48 * 1024 * 1024
LARGE_VMEM_LIMIT = 56 * 1024 * 1024


def _sigmoid(x):
    return 1.0 / (1.0 + jnp.exp(-x))


def _silu(x):
    return x * _sigmoid(x)


def _softplus(x):
    return jnp.maximum(x, 0.0) + jnp.log1p(jnp.exp(-jnp.abs(x)))


def _dot(a, b):
    return jnp.dot(a.astype(BF16), b.astype(BF16), preferred_element_type=F32)


def _dot_nt(a, b):
    return lax.dot_general(a.astype(BF16), b.astype(BF16), (((1,), (1,)), ((), ())),
                           preferred_element_type=F32)


def _dot_tn(a, b):
    return lax.dot_general(a.astype(BF16), b.astype(BF16), (((0,), (0,)), ((), ())),
                           preferred_element_type=F32)


def _dot_f32(a, b):
    return jnp.dot(a, b, precision=HIGHEST, preferred_element_type=F32)


def _pad_rows(a, rows=CHUNK):
    if a.shape[0] == rows:
        return a
    return jnp.concatenate([a, jnp.zeros((rows - a.shape[0], a.shape[1]), a.dtype)], axis=0)


def _col(a, lane, width):
    return jnp.broadcast_to(a[:, lane:lane + 1], (a.shape[0], width))


def _iotas(m):
    row = lax.broadcasted_iota(jnp.int32, (m, CHUNK), 0)
    col = lax.broadcasted_iota(jnp.int32, (m, CHUNK), 1)
    return row, col


def _valid_rows(m, chunk_idx, first_lo, hi):
    r = lax.broadcasted_iota(jnp.int32, (m, 1), 0)
    lo = jnp.where(chunk_idx == 0, first_lo, 0)
    return (r >= lo) & (r < hi)


def _pack_lanes(per_seq):
    stride = CHUNK // len(per_seq)
    out = per_seq[0]
    for b in range(1, len(per_seq)):
        out = out + pltpu.roll(per_seq[b], b * stride, 1)
    return out, stride


def _decay_terms(la):
    m = la.shape[0]
    row, col = _iotas(m)
    lower = (row >= col).astype(F32)
    cum = _dot_f32(lower, _pad_rows(la))
    last = cum[m - 1:m, :]
    return cum, _pad_rows(cum).T, jnp.exp(cum), jnp.exp(last - cum), jnp.exp(last)


def _decay_matrix(cum, cum_t, lane):
    m = cum.shape[0]
    row, col = _iotas(m)
    keep = row >= col
    seg = _col(cum, lane, CHUNK) - jnp.broadcast_to(cum_t[lane:lane + 1, :], (m, CHUNK))
    return jnp.where(keep, jnp.exp(jnp.where(keep, seg, 0.0)), 0.0)


def _shift_rows_matrix(rows):
    r = lax.broadcasted_iota(jnp.int32, (rows, rows), 0)
    c = lax.broadcasted_iota(jnp.int32, (rows, rows), 1)
    blocks = [jnp.where(r - c == CONV_K - 1 - j, 1.0, 0.0) for j in range(CONV_K - 1)]
    return jnp.concatenate(blocks, axis=0).astype(BF16)


def _causal_conv(ext_ref, cur, w_ref, shift_mat):
    rows = cur.shape[0]
    w_last = w_ref[CONV_K - 1:CONV_K, :]
    ext_ref[SUBLANES:2 * SUBLANES, :] = cur[0:SUBLANES]
    head = cur[0:SUBLANES] * w_last
    for j in range(CONV_K - 1):
        start = SUBLANES - (CONV_K - 1) + j
        head = head + ext_ref[start:start + SUBLANES, :] * w_ref[j:j + 1, :]
    ext_ref[0:SUBLANES, :] = cur[rows - SUBLANES:rows]
    if rows == SUBLANES:
        return head
    shifted = jnp.dot(shift_mat, cur.astype(BF16), preferred_element_type=F32)
    acc = cur * w_last
    for j in range(CONV_K - 1):
        acc = acc + shifted[j * rows:(j + 1) * rows] * w_ref[j:j + 1, :]
    return jnp.concatenate([head, acc[SUBLANES:]], axis=0)


def _rms(x, w):
    ms = jnp.mean(x * x, axis=-1, keepdims=True)
    return x * lax.rsqrt(ms + NORM_EPS) * w


def _prenorm_kernel(x_ref, g_ref, h_ref):
    h_ref[...] = _rms(x_ref[...], g_ref[...]).astype(BF16)


def _prenorm(x, g_all, layer, tm):
    n = x.shape[0]
    return pl.pallas_call(
        _prenorm_kernel,
        grid=(n // tm,),
        in_specs=[pl.BlockSpec((tm, D_MODEL), lambda i: (i, 0)),
                  pl.BlockSpec((None, 1, D_MODEL), lambda i: (layer, 0, 0))],
        out_specs=pl.BlockSpec((tm, D_MODEL), lambda i: (i, 0)),
        out_shape=jax.ShapeDtypeStruct((n, D_MODEL), BF16),
        compiler_params=pltpu.CompilerParams(dimension_semantics=("arbitrary",)),
        name="prenorm",
    )(x, g_all)


def _wprep_kernel(offs_ref, a0_ref, a1_ref, a2_ref, b_ref, o_ref):
    j = pl.program_id(1)
    o_ref[0:CHUNK, :] = a0_ref[...].astype(BF16)
    o_ref[CHUNK:2 * CHUNK, :] = a1_ref[...].astype(BF16)

    @pl.when(j < W_STEPS - 1)
    def _():
        o_ref[2 * CHUNK:3 * CHUNK, :] = a2_ref[...].astype(BF16)

    @pl.when(j == W_STEPS - 1)
    def _():
        small = jnp.concatenate([a2_ref[0:SUBLANES, :], b_ref[...],
                                 jnp.zeros((CHUNK - 2 * SUBLANES, D_MODEL), F32)], axis=0)
        o_ref[2 * CHUNK:3 * CHUNK, :] = small.astype(BF16)


def _wprep(w_t):
    offs = np.concatenate([np.arange(a, b, CHUNK) for a, b in W_SEGMENTS] + [[1536]]).astype(np.int32)
    assert offs.shape[0] == W_BLOCKS and not (offs % SUBLANES).any()
    offs = offs // SUBLANES
    def source_block(k):
        return pl.BlockSpec(
            (pl.Element(CHUNK), pl.Element(D_MODEL)),
            lambda l, j, offs: ((l * (IN_W // SUBLANES) + offs[W_PER_STEP * j + k]) * SUBLANES, 0))

    grid_spec = pltpu.PrefetchScalarGridSpec(
        num_scalar_prefetch=1,
        grid=(DEPTH, W_STEPS),
        in_specs=[source_block(k) for k in range(W_PER_STEP)]
        + [pl.BlockSpec((SUBLANES, D_MODEL), lambda l, j, offs: ((l * IN_W + 6408) // SUBLANES, 0))],
        out_specs=pl.BlockSpec((None, W_PER_STEP * CHUNK, D_MODEL), lambda l, j, offs: (l, j, 0)),
    )
    w_rows = w_t.reshape(DEPTH * IN_W, D_MODEL)
    return pl.pallas_call(
        _wprep_kernel, grid_spec=grid_spec,
        out_shape=jax.ShapeDtypeStruct((DEPTH, IN_W_PAD, D_MODEL), BF16),
        compiler_params=pltpu.CompilerParams(dimension_semantics=("arbitrary", "arbitrary")),
        name="wprep",
    )(jnp.asarray(offs), w_rows, w_rows, w_rows, w_rows)


def _inproj_kernel(h_ref, w_ref, o_ref):
    o_ref[...] = lax.dot_general(h_ref[...], w_ref[...], (((1,), (1,)), ((), ())),
                                 preferred_element_type=F32)


def _inproj(h, w_all, layer, tm):
    n = h.shape[0]
    return pl.pallas_call(
        _inproj_kernel,
        grid=(IN_W_PAD // PROJ_TN, n // tm),
        in_specs=[
            pl.BlockSpec((tm, D_MODEL), lambda j, i: (i, 0)),
            pl.BlockSpec((None, PROJ_TN, D_MODEL), lambda j, i: (layer, j, 0)),
        ],
        out_specs=pl.BlockSpec((tm, PROJ_TN), lambda j, i: (i, j)),
        out_shape=jax.ShapeDtypeStruct((n, IN_W_PAD), F32),
        compiler_params=pltpu.CompilerParams(
            dimension_semantics=("arbitrary", "arbitrary"), vmem_limit_bytes=LARGE_VMEM_LIMIT),
        name="inproj",
    )(h, w_all)


def _embed_prompt_kernel(x_ref, meta_ref, g_ref, o_ref, h_ref):
    c = pl.program_id(0)

    @pl.when(c == 0)
    def _():
        head = jnp.concatenate([jnp.zeros((PROMPT_PAD, D_MODEL), F32), meta_ref[...]], axis=0)
        for b in range(BATCH):
            o_ref[b * CHUNK:(b + 1) * CHUNK, :] = head

    @pl.when(c > 0)
    def _():
        for b in range(BATCH):
            o_ref[b * CHUNK:(b + 1) * CHUNK, :] = x_ref[b]

    h_ref[...] = _rms(o_ref[...], g_ref[...]).astype(BF16)


def _embed_prompt(x_prompt, meta_tokens, g_all):
    x4 = x_prompt.reshape(BATCH, SEQ // CHUNK, CHUNK, D_MODEL)
    row = pl.BlockSpec((BATCH * CHUNK, D_MODEL), lambda c: (c, 0))
    return pl.pallas_call(
        _embed_prompt_kernel,
        grid=(PROMPT_CHUNKS,),
        in_specs=[pl.BlockSpec((BATCH, None, CHUNK, D_MODEL), lambda c: (0, jnp.maximum(c - 1, 0), 0, 0)),
                  pl.BlockSpec((N_META, D_MODEL), lambda c: (0, 0)),
                  pl.BlockSpec((None, 1, D_MODEL), lambda c: (0, 0, 0))],
        out_specs=[row, row],
        out_shape=[jax.ShapeDtypeStruct((N_PROMPT, D_MODEL), F32),
                   jax.ShapeDtypeStruct((N_PROMPT, D_MODEL), BF16)],
        compiler_params=pltpu.CompilerParams(dimension_semantics=("arbitrary",), vmem_limit_bytes=VMEM_LIMIT),
        name="embed_prompt",
    )(x4, meta_tokens, g_all)


def _outproj_kernel(with_next, by_sequence, *refs):
    if with_next:
        ya_ref, yb_ref, yc_ref, yd_ref, w_ref, x_ref, g_ref, m_ref, gn_ref, o_ref, h_ref = refs
    else:
        ya_ref, yb_ref, yc_ref, yd_ref, w_ref, x_ref, g_ref, m_ref, o_ref = refs
    acc = None
    for g, y_ref in enumerate((ya_ref, yb_ref, yc_ref, yd_ref)):
        part = jnp.dot(y_ref[...], w_ref[g * GROUP_W:(g + 1) * GROUP_W, :], preferred_element_type=F32)
        acc = part if acc is None else acc + part
    x_new = jnp.where(m_ref[...] > 0.0, x_ref[...] + _rms(acc, g_ref[...]), 0.0)
    if by_sequence:
        for b in range(BATCH):
            o_ref[b] = x_new[b * CHUNK:(b + 1) * CHUNK]
    else:
        o_ref[...] = x_new
    if with_next:
        h_ref[...] = _rms(x_new, gn_ref[...]).astype(BF16)


def _outproj(ys, w_all, x, post_all, rowmask, pre_all, layer, tm, by_sequence=False):
    n = x.shape[0]
    with_next = layer + 1 < DEPTH
    yspec = pl.BlockSpec((tm, GROUP_W), lambda i: (i, 0))
    row = pl.BlockSpec((tm, D_MODEL), lambda i: (i, 0))
    in_specs = [yspec, yspec, yspec, yspec,
                pl.BlockSpec((None, D_MODEL, D_MODEL), lambda i: (layer, 0, 0)),
                row,
                pl.BlockSpec((None, 1, D_MODEL), lambda i: (layer, 0, 0)),
                pl.BlockSpec((tm, 1), lambda i: (i, 0))]
    args = list(ys) + [w_all, x, post_all, rowmask]
    if by_sequence:
        assert tm == BATCH * CHUNK and not with_next
        out_specs = [pl.BlockSpec((BATCH, None, CHUNK, D_MODEL), lambda i: (0, jnp.maximum(i - 1, 0), 0, 0))]
        out_shape = [jax.ShapeDtypeStruct((BATCH, SEQ // CHUNK, CHUNK, D_MODEL), F32)]
    else:
        out_specs = [row]
        out_shape = [jax.ShapeDtypeStruct((n, D_MODEL), F32)]
    if with_next:
        in_specs.append(pl.BlockSpec((None, 1, D_MODEL), lambda i: (layer + 1, 0, 0)))
        args.append(pre_all)
        out_specs.append(row)
        out_shape.append(jax.ShapeDtypeStruct((n, D_MODEL), BF16))
    res = pl.pallas_call(
        functools.partial(_outproj_kernel, with_next, by_sequence),
        grid=(n // tm,), in_specs=in_specs, out_specs=out_specs, out_shape=out_shape,
        compiler_params=pltpu.CompilerParams(
            dimension_semantics=("arbitrary",), vmem_limit_bytes=VMEM_LIMIT),
        name="outproj",
    )(*args)
    return (res[0], res[1]) if with_next else (res[0], None)


def _init_state(c, has_init, s_scr, s0_ref, ext=None, c0_ref=None):
    @pl.when(c == 0)
    def _():
        if has_init:
            s_scr[...] = s0_ref[...]
            if ext is not None:
                ext[:, 0:SUBLANES, :] = c0_ref[...]
        else:
            s_scr[...] = jnp.zeros_like(s_scr)
            if ext is not None:
                ext[:, 0:SUBLANES, :] = jnp.zeros((ext.shape[0], SUBLANES, ext.shape[2]), F32)


def _run_staggered(groups, skew):
    pending, live, tick = list(groups), [], 0
    while pending or live:
        if pending and tick % skew == 0:
            live.append(pending.pop(0))
        for gen in list(live):
            if next(gen, "done") == "done":
                live.remove(gen)
        tick += 1


def _ssd_stages(nseq, rows, layer, has_init, state_t, nchunks, first_lo, hi, *refs):
    if has_init:
        (z_ref, x_ref, bc_ref, sm_ref, s0_ref, c0_ref, _, cw_ref, cb_ref, lp_ref, nw_ref, d_ref,
         y_ref, so_ref, s_scr, ext) = refs
    else:
        (z_ref, x_ref, bc_ref, sm_ref, cw_ref, cb_ref, lp_ref, nw_ref, d_ref,
         y_ref, so_ref, s_scr, ext) = refs
        s0_ref = c0_ref = None
    c = pl.program_id(1)
    m = rows
    _init_state(c, has_init, s_scr, s0_ref, ext, c0_ref)

    valid = _valid_rows(m, c, first_lo, hi)
    lane = lax.broadcasted_iota(jnp.int32, (1, CHUNK), 1)
    head_lane = (lane >= LANE_DT) & (lane < LANE_DT + SSD_HEADS)
    dt_all = _softplus(sm_ref[...] + lp_ref[ROW_SSD_DTB:ROW_SSD_DTB + 1, :])
    la_all = -jnp.exp(lp_ref[ROW_SSD_ALOG:ROW_SSD_ALOG + 1, :]) * dt_all
    dts = [jnp.where(valid, dt_all[b * m:(b + 1) * m], 0.0) for b in range(nseq)]
    la, stride = _pack_lanes([jnp.where(valid & head_lane, la_all[b * m:(b + 1) * m], 0.0) for b in range(nseq)])
    cum, cum_t, e_cum, e_rem, e_last = _decay_terms(la)

    shift_mat = _shift_rows_matrix(m) if m > SUBLANES else None
    hpg = SSD_HEADS // SSD_GROUPS
    c_off = GROUP_W + SSD_GROUPS * SSD_STATE
    gw = GROUP_W // SSD_GROUPS
    ys = [None] * nseq

    def sequence(b):
        rows_b = slice(b * m, (b + 1) * m)
        cur = jnp.concatenate([x_ref[rows_b, :], bc_ref[rows_b, :]], axis=1)
        a_b = _silu(_causal_conv(ext.at[b], cur, cw_ref, shift_mat) + cb_ref[...])
        yield
        scores = []
        for g in range(SSD_GROUPS):
            b_g = a_b[:, GROUP_W + g * SSD_STATE:GROUP_W + (g + 1) * SSD_STATE]
            c_g = a_b[:, c_off + g * SSD_STATE:c_off + (g + 1) * SSD_STATE]
            scores.append((_dot_nt(c_g, _pad_rows(b_g)), b_g, c_g))
        yield
        heads = []
        for h in range(SSD_HEADS):
            sc, b_g, c_g = scores[h // hpg]
            ln = b * stride + LANE_DT + h
            x_h = a_b[:, h * SSD_HEAD_DIM:(h + 1) * SSD_HEAD_DIM]
            v_h = x_h * _col(dts[b], LANE_DT + h, SSD_HEAD_DIM)
            s_h = s_scr[b, h]
            v_w = v_h * _col(e_rem, ln, SSD_HEAD_DIM)
            if state_t:
                from_state = _dot_nt(c_g, s_h)
                s_scr[b, h] = s_h * e_last[:, ln:ln + 1] + _dot_tn(v_w, b_g)
            else:
                from_state = _dot(c_g, s_h)
                s_scr[b, h] = s_h * e_last[:, ln:ln + 1] + _dot_tn(b_g, v_w)
            o = (_dot(sc * _decay_matrix(cum, cum_t, ln), _pad_rows(v_h))
                 + from_state * _col(e_cum, ln, SSD_HEAD_DIM))
            heads.append(o + d_ref[layer, h] * x_h)
            yield
        y = jnp.concatenate(heads, axis=1) * _silu(z_ref[rows_b, :])
        outs = [_rms(y[:, g * gw:(g + 1) * gw], nw_ref[:, g * gw:(g + 1) * gw]) for g in range(SSD_GROUPS)]
        ys[b] = jnp.concatenate(outs, axis=1)

    def finish():
        y_ref[...] = jnp.concatenate(ys, axis=0).astype(y_ref.dtype)

        @pl.when(c == nchunks - 1)
        def _():
            so_ref[...] = s_scr[...]

    return [sequence(b) for b in range(nseq)], finish


def _ret_stages(nseq, seq_group, rows, has_init, nchunks, first_lo, hi, *refs):
    if has_init:
        qk_ref, v_ref, g_ref, s0_ref, _, lp_ref, nw_ref, y_ref, so_ref, s_scr = refs
    else:
        qk_ref, v_ref, g_ref, lp_ref, nw_ref, y_ref, so_ref, s_scr = refs
        s0_ref = None
    c = pl.program_id(1)
    m = rows
    _init_state(c, has_init, s_scr, s0_ref)

    valid = _valid_rows(m, c, first_lo, hi)
    lane = lax.broadcasted_iota(jnp.int32, (1, CHUNK), 1)
    log_gamma = jnp.broadcast_to(lp_ref[ROW_LOG_GAMMA:ROW_LOG_GAMMA + 1, :], (m, CHUNK))
    la_one = jnp.where(valid & (lane < RET_HEADS), log_gamma, 0.0)
    la, stride = _pack_lanes([la_one] * nseq)
    cum, cum_t, e_cum, e_rem, e_last = _decay_terms(la)

    qk_all = qk_ref[...]
    v_all = v_ref[...]
    all_units = []
    for b in range(nseq):
        qk = qk_all[b * m:(b + 1) * m]
        v = jnp.where(valid, v_all[b * m:(b + 1) * m], 0.0)
        for h in range(RET_HEADS):
            q_h = qk[:, h * RET_DK:(h + 1) * RET_DK]
            k_h = qk[:, (RET_HEADS + h) * RET_DK:(RET_HEADS + h + 1) * RET_DK] * (RET_DK ** -0.5)
            s_h = s_scr[b, h]
            all_units.append(dict(b=b, h=h, ln=b * stride + h, k=k_h, v=v[:, h * RET_DV:(h + 1) * RET_DV],
                                  s=s_h, raw=_dot_nt(q_h, _pad_rows(k_h)), q_s=_dot(q_h, s_h)))
    ys = [None] * nseq

    def group(b0):
        units = all_units[b0 * RET_HEADS:(b0 + seq_group) * RET_HEADS]
        scs = [u["raw"] * _decay_matrix(cum, cum_t, u["ln"]) for u in units]
        yield
        os_ = [_dot(sc, _pad_rows(u["v"])) + u["q_s"] * _col(e_cum, u["ln"], RET_DV)
               for sc, u in zip(scs, units)]
        yield
        for u in units:
            ln = u["ln"]
            s_scr[u["b"], u["h"]] = (u["s"] * e_last[:, ln:ln + 1]
                                     + _dot_tn(u["k"] * _col(e_rem, ln, RET_DK), u["v"]))
        yield
        for i in range(seq_group):
            heads = [_rms(os_[i * RET_HEADS + h], nw_ref[:, h * RET_DV:(h + 1) * RET_DV])
                     for h in range(RET_HEADS)]
            rows_b = slice((b0 + i) * m, (b0 + i + 1) * m)
            ys[b0 + i] = jnp.concatenate(heads, axis=1) * _silu(g_ref[rows_b, :])

    def finish():
        y_ref[...] = jnp.concatenate(ys, axis=0).astype(y_ref.dtype)

        @pl.when(c == nchunks - 1)
        def _():
            so_ref[...] = s_scr[...]

    return [group(b0) for b0 in range(0, nseq, seq_group)], finish


def _gdn_stages(nseq, seq_group, rows, has_init, nchunks, first_lo, hi, *refs):
    if has_init:
        (q_ref, k_ref, v_ref, g_ref, sm_ref, s0_ref, c0_ref, _, cw_ref, lp_ref, nw_ref,
         y_ref, so_ref, s_scr, ext) = refs
    else:
        (q_ref, k_ref, v_ref, g_ref, sm_ref, cw_ref, lp_ref, nw_ref,
         y_ref, so_ref, s_scr, ext) = refs
        s0_ref = c0_ref = None
    c = pl.program_id(1)
    m = rows
    _init_state(c, has_init, s_scr, s0_ref, ext, c0_ref)

    shift_mat = _shift_rows_matrix(m) if m > SUBLANES else None
    valid = _valid_rows(m, c, first_lo, hi)
    lane = lax.broadcasted_iota(jnp.int32, (1, CHUNK), 1)
    head_lane = (lane >= LANE_DECAY) & (lane < LANE_DECAY + GDN_HEADS)
    sm = sm_ref[...]
    beta_all = _sigmoid(sm)
    la_all = (-jnp.exp(lp_ref[ROW_GDN_ALOG:ROW_GDN_ALOG + 1, :])
              * _softplus(sm + lp_ref[ROW_GDN_DTB:ROW_GDN_DTB + 1, :]))
    betas = [jnp.where(valid, beta_all[b * m:(b + 1) * m], 0.0) for b in range(nseq)]
    la, stride = _pack_lanes([jnp.where(valid & head_lane, la_all[b * m:(b + 1) * m], 0.0) for b in range(nseq)])
    cum, cum_t, e_cum, e_rem, e_last = _decay_terms(la)
    row, col = _iotas(m)
    levels = max(1, (min(m, hi) - 1).bit_length())

    ys = [None] * nseq

    def joining_mask(shift):
        same_big = lax.shift_right_logical(row, shift + 1) == lax.shift_right_logical(col, shift + 1)
        same_small = lax.shift_right_logical(row, shift) == lax.shift_right_logical(col, shift)
        return same_big & jnp.logical_not(same_small) & (row > col)

    def group(b0):
        units = []
        for b in range(b0, b0 + seq_group):
            rows_b = slice(b * m, (b + 1) * m)
            cur = jnp.concatenate([q_ref[rows_b, :], k_ref[rows_b, :], v_ref[rows_b, :]], axis=1)
            a_b = _silu(_causal_conv(ext.at[b], cur, cw_ref, shift_mat))
            yield
            for h in range(GDN_HEADS):
                ln = b * stride + LANE_DECAY + h
                q_h = a_b[:, h * GDN_DK:(h + 1) * GDN_DK]
                k_h = a_b[:, GROUP_W + h * GDN_DK:GROUP_W + (h + 1) * GDN_DK]
                v_h = a_b[:, 2 * GROUP_W + h * GDN_DV:2 * GROUP_W + (h + 1) * GDN_DV]
                q_h = q_h * lax.rsqrt(jnp.sum(q_h * q_h, axis=-1, keepdims=True) + 1e-6) * (GDN_DK ** -0.5)
                k_h = k_h * lax.rsqrt(jnp.sum(k_h * k_h, axis=-1, keepdims=True) + 1e-6)
                k_h = jnp.where(valid, k_h, 0.0)
                b_h = _col(betas[b], LANE_BETA + h, GDN_DK)
                k_b = k_h * b_h
                decay = _decay_matrix(cum, cum_t, ln)
                k_pad = _pad_rows(k_h)
                units.append(dict(
                    b=b, h=h, ln=ln, q=q_h, k=k_h, k_b=k_b, decay=decay,
                    a=jnp.where(row > col, _dot_nt(k_b, k_pad) * decay, 0.0),
                    v_b=_pad_rows(jnp.where(valid, v_h, 0.0) * b_h),
                    qk=_dot_nt(q_h, k_pad) * decay))
                yield
        eye = (row == col).astype(F32)
        mask = joining_mask(0)
        ts = [eye - jnp.where(mask, u["a"], 0.0) for u in units]
        for shift in range(1, levels):
            mask = joining_mask(shift)
            inner = [_dot(jnp.where(mask, u["a"], 0.0), _pad_rows(t)) for u, t in zip(units, ts)]
            yield
            ts = [t - _dot(t, _pad_rows(x)) for t, x in zip(ts, inner)]
            yield
        us = [_dot(t, u["v_b"]) for t, u in zip(ts, units)]
        ws = [_dot(t, _pad_rows(u["k_b"] * _col(e_cum, u["ln"], GDN_DK))) for t, u in zip(ts, units)]
        yield
        ss = [s_scr[u["b"], u["h"]] for u in units]
        v_news = [x - _dot(w, s_h) for x, w, s_h in zip(us, ws, ss)]
        yield
        os_ = [_dot(u["q"] * _col(e_cum, u["ln"], GDN_DK), s_h) + _dot(u["qk"], _pad_rows(v_new))
               for u, s_h, v_new in zip(units, ss, v_news)]
        yield
        for u, s_h, v_new in zip(units, ss, v_news):
            ln = u["ln"]
            s_scr[u["b"], u["h"]] = (s_h * e_last[:, ln:ln + 1]
                                     + _dot_tn(u["k"] * _col(e_rem, ln, GDN_DK), v_new))
        yield
        for i in range(seq_group):
            heads = [_rms(o, nw_ref[...]) for o in os_[i * GDN_HEADS:(i + 1) * GDN_HEADS]]
            rows_b = slice((b0 + i) * m, (b0 + i + 1) * m)
            ys[b0 + i] = jnp.concatenate(heads, axis=1) * _silu(g_ref[rows_b, :])

    def finish():
        y_ref[...] = jnp.concatenate(ys, axis=0).astype(y_ref.dtype)

        @pl.when(c == nchunks - 1)
        def _():
            so_ref[...] = s_scr[...]

    return [group(b0) for b0 in range(0, nseq, seq_group)], finish


def _swa_stages(nseq, seq_group, rows, layer, is_prompt, *refs):
    if is_prompt:
        q_ref, g_ref, kvc_ref, kvp_ref, sink_ref, y_ref = refs
    else:
        q_ref, g_ref, kvc_ref, kp_ref, vp_ref, sink_ref, y_ref = refs
    n = pl.program_id(1)
    m = rows
    if is_prompt:
        lo_prev = jnp.where(n == 0, CHUNK, jnp.where(n == 1, PROMPT_PAD, 0))
        lo_cur = jnp.where(n == 0, PROMPT_PAD, 0)
    else:
        lo_prev, lo_cur = 0, 0
    qi = lax.broadcasted_iota(jnp.int32, (m, 2 * CHUNK), 0)
    kj = lax.broadcasted_iota(jnp.int32, (m, 2 * CHUNK), 1)
    dist = qi + WINDOW - kj
    key_ok = ((kj < CHUNK) & (kj >= lo_prev)) | (kj >= CHUNK + lo_cur)
    visible = (dist >= 0) & (dist <= WINDOW) & key_ok
    dist_f = dist.astype(F32)

    grp = SWA_HEADS // SWA_KV_HEADS
    ys = [None] * nseq

    def group(b0):
        units = []
        for b in range(b0, b0 + seq_group):
            q = q_ref[b * m:(b + 1) * m, :]
            kvc = _pad_rows(kvc_ref[b * m:(b + 1) * m, :])
            if is_prompt:
                k_prev = kvp_ref[b * CHUNK:(b + 1) * CHUNK, 0:KV_W]
                v_prev = kvp_ref[b * CHUNK:(b + 1) * CHUNK, KV_W:2 * KV_W]
            else:
                k_prev, v_prev = kp_ref[b], vp_ref[b]
            k_all = jnp.concatenate([k_prev, kvc[:, 0:KV_W]], axis=0)
            v_all = jnp.concatenate([v_prev, kvc[:, KV_W:2 * KV_W]], axis=0)
            for h in range(SWA_HEADS):
                kvh = h // grp
                units.append(dict(h=h, q=q[:, h * SWA_HEAD_DIM:(h + 1) * SWA_HEAD_DIM],
                                  k=k_all[:, kvh * SWA_HEAD_DIM:(kvh + 1) * SWA_HEAD_DIM],
                                  v=v_all[:, kvh * SWA_HEAD_DIM:(kvh + 1) * SWA_HEAD_DIM]))
        ss = [jnp.where(visible,
                        _dot_nt(u["q"], u["k"]) * (SWA_HEAD_DIM ** -0.5)
                        - 2.0 ** (-8.0 * (u["h"] + 1) / SWA_HEADS) * dist_f, -1e30) for u in units]
        yield
        mxs = [jnp.maximum(jnp.max(s, axis=-1, keepdims=True), sink_ref[layer, u["h"]])
               for s, u in zip(ss, units)]
        es = [jnp.exp(s - mx) for s, mx in zip(ss, mxs)]
        yield
        dens = [jnp.sum(e, axis=-1, keepdims=True) + jnp.exp(sink_ref[layer, u["h"]] - mx)
                for e, mx, u in zip(es, mxs, units)]
        os_ = [_dot(e, u["v"]) / den for e, u, den in zip(es, units, dens)]
        yield
        for i in range(seq_group):
            rows_b = slice((b0 + i) * m, (b0 + i + 1) * m)
            ys[b0 + i] = (jnp.concatenate(os_[i * SWA_HEADS:(i + 1) * SWA_HEADS], axis=1)
                          * _silu(g_ref[rows_b, :]))

    def finish():
        y_ref[...] = jnp.concatenate(ys, axis=0).astype(y_ref.dtype)

    return [group(b0) for b0 in range(0, nseq, seq_group)], finish


def _mixer_kernel(parts, *refs):
    starts = [0, sum(p[1] for p in parts), sum(p[1] + p[2] for p in parts)]
    gens, finishes = [], []
    for stages, n_in, n_out, n_scr in parts:
        mine = []
        for k, cnt in enumerate((n_in, n_out, n_scr)):
            mine += refs[starts[k]:starts[k] + cnt]
            starts[k] += cnt
        g, fin = stages(*mine)
        gens.append(g)
        finishes.append(fin)
    for w in range(0, len(gens), MIXERS_PER_WAVE):
        wave = gens[w:w + MIXERS_PER_WAVE]
        order = []
        for i in range(max(len(g) for g in wave)):
            order += [g[i] for g in wave if i < len(g)]
        _run_staggered(order, 1)
    for fin in finishes:
        fin()


def _geometry(sample):
    if sample:
        return (DEC_BATCH // SAMPLE_SEQS, 1), SAMPLE_SEQS, SUBLANES, N_SAMPLE, 0, DEC_SEQ
    return (1, PROMPT_CHUNKS), BATCH, CHUNK, N_PROMPT, PROMPT_PAD, CHUNK


def _rowblk(sample):
    return (lambda blk, c: blk) if sample else (lambda blk, c: c)


def _pcol(nrows, rowblk, width, col):
    assert col % width == 0
    return pl.BlockSpec((nrows, width), lambda blk, c: (rowblk(blk, c), col // width))


def _layer_param(shape, layer):
    return pl.BlockSpec((None,) + shape, lambda blk, c: (layer,) + (0,) * len(shape))


def _layer_seqs(nseq, shape, layer):
    return pl.BlockSpec((None, nseq) + shape, lambda blk, c: (layer, blk) + (0,) * len(shape))


def _seqs_out(nseq, shape):
    return pl.BlockSpec((nseq,) + shape, lambda blk, c: (blk,) + (0,) * len(shape))


_SMEM = pl.BlockSpec(memory_space=pltpu.SMEM)
_MIXER_PARAMS = pltpu.CompilerParams(dimension_semantics=("arbitrary", "arbitrary"),
                                     vmem_limit_bytes=LARGE_VMEM_LIMIT)


def _state_output(sample, nseq, state, layer, acc, in_specs, args):
    if not sample:
        return _seqs_out(nseq, state), jax.ShapeDtypeStruct((nseq,) + state, F32), {}
    in_specs.append(pl.BlockSpec(memory_space=pl.ANY))
    spec = pl.BlockSpec((None, nseq) + state, lambda blk, c: (layer, blk) + (0,) * len(state))
    shape = jax.ShapeDtypeStruct((DEPTH, DEC_BATCH) + state, F32)
    if layer == 0:
        args.append(jnp.zeros((SUBLANES, CHUNK), F32))
        return spec, shape, {}
    args.append(acc)
    return spec, shape, {len(args) - 1: 1}


def _ssd(p, prm, layer, init, acc=None):
    sample = init is not None
    grid, nseq, rows, total, first_lo, hi = _geometry(sample)
    rb, n = _rowblk(sample), nseq * rows
    state = (SSD_HEADS, SSD_HEAD_DIM, SSD_STATE) if sample else (SSD_HEADS, SSD_STATE, SSD_HEAD_DIM)
    in_specs = [_pcol(n, rb, GROUP_W, COL_Z), _pcol(n, rb, GROUP_W, COL_X),
                _pcol(n, rb, GROUP_W, COL_BC), _pcol(n, rb, CHUNK, COL_SMALL)]
    args = [p, p, p, p]
    if sample:
        in_specs += [_layer_seqs(nseq, state, layer), _layer_seqs(nseq, (SUBLANES, SSD_CONV_W), layer)]
        args += list(init)
    state_spec, state_shape, aliases = _state_output(sample, nseq, state, layer, acc, in_specs, args)
    in_specs += [_layer_param((CONV_K, SSD_CONV_W), layer), _layer_param((1, SSD_CONV_W), layer),
                 _layer_param((SUBLANES, CHUNK), layer), _layer_param((1, GROUP_W), layer), _SMEM]
    args += [prm["ssd_conv_w"], prm["ssd_conv_b"], prm["lanes"], prm["ssd_norm"], prm["ssd_d"]]
    return dict(
        name="ssd", grid=grid,
        stages=functools.partial(_ssd_stages, nseq, rows, layer, sample, sample, grid[1], first_lo, hi),
        in_specs=in_specs, args=args,
        out_specs=[pl.BlockSpec((n, GROUP_W), lambda blk, c: (rb(blk, c), 0)), state_spec],
        out_shape=[jax.ShapeDtypeStruct((total, GROUP_W), BF16), state_shape],
        aliases=aliases,
        scratch=[pltpu.VMEM((nseq,) + state, F32), pltpu.VMEM((nseq, 2 * SUBLANES, SSD_CONV_W), F32)])


def _ret(p, prm, layer, init, acc=None):
    sample = init is not None
    grid, nseq, rows, total, first_lo, hi = _geometry(sample)
    rb, n = _rowblk(sample), nseq * rows
    state = (RET_HEADS, RET_DK, RET_DV)
    in_specs = [_pcol(n, rb, GROUP_W, COL_QKR), _pcol(n, rb, GROUP_W, COL_VR), _pcol(n, rb, GROUP_W, COL_GR)]
    args = [p, p, p]
    if sample:
        in_specs += [_layer_seqs(nseq, state, layer)]
        args += list(init)
    state_spec, state_shape, aliases = _state_output(sample, nseq, state, layer, acc, in_specs, args)
    in_specs += [_layer_param((SUBLANES, CHUNK), layer), _layer_param((1, GROUP_W), layer)]
    args += [prm["lanes"], prm["ret_norm"]]
    return dict(
        name="ret", grid=grid,
        stages=functools.partial(_ret_stages, nseq, nseq if sample else 1, rows, sample, grid[1], first_lo, hi),
        in_specs=in_specs, args=args,
        out_specs=[pl.BlockSpec((n, GROUP_W), lambda blk, c: (rb(blk, c), 0)), state_spec],
        out_shape=[jax.ShapeDtypeStruct((total, GROUP_W), BF16), state_shape],
        aliases=aliases,
        scratch=[pltpu.VMEM((nseq,) + state, F32)])


def _gdn(p, prm, layer, init, acc=None):
    sample = init is not None
    grid, nseq, rows, total, first_lo, hi = _geometry(sample)
    rb, n = _rowblk(sample), nseq * rows
    state = (GDN_HEADS, GDN_DK, GDN_DV)
    in_specs = [_pcol(n, rb, GROUP_W, COL_QD), _pcol(n, rb, GROUP_W, COL_KD), _pcol(n, rb, GROUP_W, COL_VD),
                _pcol(n, rb, GROUP_W, COL_GD), _pcol(n, rb, CHUNK, COL_SMALL)]
    args = [p, p, p, p, p]
    if sample:
        in_specs += [_layer_seqs(nseq, state, layer), _layer_seqs(nseq, (SUBLANES, GDN_CONV_W), layer)]
        args += list(init)
    state_spec, state_shape, aliases = _state_output(sample, nseq, state, layer, acc, in_specs, args)
    in_specs += [_layer_param((CONV_K, GDN_CONV_W), layer), _layer_param((SUBLANES, CHUNK), layer),
                 _layer_param((1, GDN_DV), layer)]
    args += [prm["gdn_conv_w"], prm["lanes"], prm["gdn_norm"]]
    return dict(
        name="gdn", grid=grid,
        stages=functools.partial(_gdn_stages, nseq, 2 if sample else 1, rows, sample, grid[1], first_lo, hi),
        in_specs=in_specs, args=args,
        out_specs=[pl.BlockSpec((n, GROUP_W), lambda blk, c: (rb(blk, c), 0)), state_spec],
        out_shape=[jax.ShapeDtypeStruct((total, GROUP_W), BF16), state_shape],
        aliases=aliases,
        scratch=[pltpu.VMEM((nseq,) + state, F32), pltpu.VMEM((nseq, 2 * SUBLANES, GDN_CONV_W), F32)])


def _swa(p, prm, layer, cache):
    sample = cache is not None
    grid, nseq, rows, total, _, _ = _geometry(sample)
    rb, n = _rowblk(sample), nseq * rows
    in_specs = [_pcol(n, rb, GROUP_W, COL_QA), _pcol(n, rb, GROUP_W, COL_GA), _pcol(n, rb, 2 * KV_W, COL_KV)]
    args = [p, p, p]
    if sample:
        in_specs += [_layer_seqs(nseq, (WINDOW, KV_W), layer), _layer_seqs(nseq, (WINDOW, KV_W), layer)]
        args += list(cache)
    else:
        in_specs += [pl.BlockSpec((n, 2 * KV_W), lambda blk, c: (jnp.maximum(c - 1, 0), COL_KV // (2 * KV_W)))]
        args += [p]
    in_specs += [_SMEM]
    args += [prm["swa_sinks"]]
    return dict(
        name="swa", grid=grid,
        stages=functools.partial(_swa_stages, nseq, nseq if sample else 1, rows, layer, not sample),
        in_specs=in_specs, args=args,
        out_specs=[pl.BlockSpec((n, GROUP_W), lambda blk, c: (rb(blk, c), 0))],
        out_shape=[jax.ShapeDtypeStruct((total, GROUP_W), BF16)],
        aliases={}, scratch=[])


def _run_mixers(parts, suffix):
    kernel_parts = tuple((p["stages"], len(p["in_specs"]), len(p["out_specs"]), len(p["scratch"]))
                         for p in parts)
    aliases, n_in, n_out = {}, 0, 0
    for p in parts:
        aliases.update({n_in + k: n_out + v for k, v in p["aliases"].items()})
        n_in += len(p["in_specs"])
        n_out += len(p["out_specs"])
    res = pl.pallas_call(
        functools.partial(_mixer_kernel, kernel_parts),
        grid=parts[0]["grid"],
        in_specs=[s for p in parts for s in p["in_specs"]],
        out_specs=[s for p in parts for s in p["out_specs"]],
        out_shape=[s for p in parts for s in p["out_shape"]],
        scratch_shapes=[s for p in parts for s in p["scratch"]],
        input_output_aliases=aliases,
        compiler_params=_MIXER_PARAMS,
        name="_".join(p["name"] for p in parts) + suffix,
    )(*[a for p in parts for a in p["args"]])
    outs, k = [], 0
    for p in parts:
        outs.append(list(res[k:k + len(p["out_specs"])]))
        k += len(p["out_specs"])
    return outs


def _lane_table(ssd_dt_bias, ssd_a_log, gdn_dt_bias, gdn_a_log):
    def put(v, lane):
        return jnp.pad(v.astype(F32), ((0, 0), (lane, CHUNK - lane - v.shape[1])))
    log_gamma = jnp.log1p(-jnp.exp2(-5.0 - jnp.arange(RET_HEADS, dtype=F32)))
    rows = [put(ssd_dt_bias, LANE_DT), put(ssd_a_log, LANE_DT), put(gdn_dt_bias, LANE_DECAY),
            put(gdn_a_log, LANE_DECAY), put(jnp.broadcast_to(log_gamma, (DEPTH, RET_HEADS)), 0)]
    rows += [jnp.zeros((DEPTH, CHUNK), F32)] * (SUBLANES - len(rows))
    return jnp.stack(rows, axis=1)


def _conv_state_pad(state):
    return jnp.pad(state, ((0, 0), (0, 0), (SUBLANES - (CONV_K - 1), 0), (0, 0)))


def kernel(x_prompt, x_sample, state_ssd, state_ssd_conv, cache_swa_k, cache_swa_v, state_ret, state_gdn,
           state_gdn_conv, meta_tokens, pre_norm, post_norm, w_in, w_out, ssd_conv_w, ssd_conv_b, ssd_dt_bias,
           ssd_a_log, ssd_d, ssd_norm, swa_sinks, ret_norm, gdn_conv_w, gdn_dt_bias, gdn_a_log, gdn_norm):
    xs = jnp.pad(x_sample, ((0, 0), (0, SUBLANES - DEC_SEQ), (0, 0))).reshape(N_SAMPLE, D_MODEL)
    mask_np = np.ones((PROMPT_CHUNKS, BATCH, CHUNK), np.float32)
    mask_np[0, :, :PROMPT_PAD] = 0.0
    mask_p = jnp.asarray(mask_np.reshape(N_PROMPT, 1))
    mask_s = jnp.asarray(np.tile(np.arange(SUBLANES) < DEC_SEQ, DEC_BATCH)[:, None], F32)

    w_in_p = _wprep(jnp.swapaxes(w_in, 1, 2))
    w_out_b = w_out.astype(BF16)
    pre_g = pre_norm.reshape(DEPTH, 1, D_MODEL)
    post_g = post_norm.reshape(DEPTH, 1, D_MODEL)
    prm = dict(
        lanes=_lane_table(ssd_dt_bias, ssd_a_log, gdn_dt_bias, gdn_a_log),
        ssd_conv_w=ssd_conv_w, ssd_conv_b=ssd_conv_b.reshape(DEPTH, 1, SSD_CONV_W),
        ssd_norm=ssd_norm.reshape(DEPTH, 1, GROUP_W), ssd_d=ssd_d, swa_sinks=swa_sinks,
        ret_norm=ret_norm.reshape(DEPTH, 1, GROUP_W), gdn_conv_w=gdn_conv_w,
        gdn_norm=gdn_norm.reshape(DEPTH, 1, GDN_DV))
    ssd_c0 = _conv_state_pad(state_ssd_conv)
    gdn_c0 = _conv_state_pad(state_gdn_conv)
    cache_k = cache_swa_k.reshape(DEPTH, DEC_BATCH, WINDOW, KV_W)
    cache_v = cache_swa_v.reshape(DEPTH, DEC_BATCH, WINDOW, KV_W)

    names = ("ssd", "ssd_conv", "swa_k", "swa_v", "ret", "gdn", "gdn_conv")
    p_out = {k: [] for k in names}
    s_out = {k: [] for k in ("ssd_conv", "swa_k", "swa_v", "gdn_conv")}
    tail = slice(CHUNK - (CONV_K - 1), CHUNK)
    state_ssd_t = jnp.swapaxes(state_ssd, 3, 4)
    s_ssd_t = s_ret = s_gdn = None
    xp, hp = _embed_prompt(x_prompt, meta_tokens, pre_g)
    hs = _prenorm(xs, pre_g, 0, N_SAMPLE)
    for l in range(DEPTH):
        pp = _inproj(hp, w_in_p, l, INPROJ_TM)
        (yd, p_gdn), (yb,), (ya, p_ssd), (yc, p_ret) = _run_mixers(
            [_gdn(pp, prm, l, None), _swa(pp, prm, l, None), _ssd(pp, prm, l, None), _ret(pp, prm, l, None)],
            "_prompt")
        xp, hp = _outproj((ya, yb, yc, yd), w_out_b, xp, post_g, mask_p, pre_g, l, PROMPT_TM,
                          by_sequence=l == DEPTH - 1)

        ps = _inproj(hs, w_in_p, l, N_SAMPLE)
        (yd, s_gdn), (ya, s_ssd_t), (yc, s_ret), (yb,) = _run_mixers(
            [_gdn(ps, prm, l, (state_gdn, gdn_c0), s_gdn), _ssd(ps, prm, l, (state_ssd_t, ssd_c0), s_ssd_t),
             _ret(ps, prm, l, (state_ret,), s_ret), _swa(ps, prm, l, (cache_k, cache_v))],
            "_sample")
        xs, hs = _outproj((ya, yb, yc, yd), w_out_b, xs, post_g, mask_s, pre_g, l, N_SAMPLE)

        pp4 = pp.reshape(PROMPT_CHUNKS, BATCH, CHUNK, IN_W_PAD)

        def last_chunk(r0, c0, c1):
            return lax.slice(pp4, (PROMPT_CHUNKS - 1, 0, r0, c0), (PROMPT_CHUNKS, BATCH, CHUNK, c1))[0]

        ps3 = ps.reshape(DEC_BATCH, SUBLANES, IN_W_PAD)
        p_out["ssd"].append(p_ssd)
        p_out["ssd_conv"].append(last_chunk(tail.start, COL_X, COL_X + SSD_CONV_W))
        p_out["swa_k"].append(last_chunk(0, COL_KV, COL_KV + KV_W))
        p_out["swa_v"].append(last_chunk(0, COL_KV + KV_W, COL_KV + 2 * KV_W))
        p_out["ret"].append(p_ret)
        p_out["gdn"].append(p_gdn)
        p_out["gdn_conv"].append(last_chunk(tail.start, COL_QD, COL_QD + GDN_CONV_W))
        s_out["ssd_conv"].append(ps3[:, 1:DEC_SEQ, COL_X:COL_X + SSD_CONV_W])
        s_out["swa_k"].append(ps3[:, :DEC_SEQ, COL_KV:COL_KV + KV_W])
        s_out["swa_v"].append(ps3[:, :DEC_SEQ, COL_KV + KV_W:COL_KV + 2 * KV_W])
        s_out["gdn_conv"].append(ps3[:, 1:DEC_SEQ, COL_QD:COL_QD + GDN_CONV_W])

    p_st = {k: jnp.stack(v) for k, v in p_out.items()}
    s_st = {k: jnp.stack(v) for k, v in s_out.items()}
    kv_shape = (SWA_KV_HEADS, SWA_HEAD_DIM)
    y_prompt = xp.reshape(BATCH, SEQ, D_MODEL)
    y_sample = xs.reshape(DEC_BATCH, SUBLANES, D_MODEL)[:, :DEC_SEQ]
    return (
        y_prompt, y_sample,
        p_st["ssd"], p_st["ssd_conv"],
        p_st["swa_k"].reshape((DEPTH, BATCH, WINDOW) + kv_shape),
        p_st["swa_v"].reshape((DEPTH, BATCH, WINDOW) + kv_shape),
        p_st["ret"], p_st["gdn"], p_st["gdn_conv"],
        jnp.swapaxes(s_ssd_t, 3, 4), s_st["ssd_conv"],
        jnp.concatenate([cache_swa_k[:, :, DEC_SEQ:], s_st["swa_k"].reshape((DEPTH, DEC_BATCH, DEC_SEQ) + kv_shape)], axis=2),
        jnp.concatenate([cache_swa_v[:, :, DEC_SEQ:], s_st["swa_v"].reshape((DEPTH, DEC_BATCH, DEC_SEQ) + kv_shape)], axis=2),
        s_ret, s_gdn, s_st["gdn_conv"],
    )
```
